```python
import jax, jax.numpy as jnp
from jax import lax
import numpy as np

D_MODEL = 2048
BATCH = 4
SEQ = 4096
DEPTH = 1

MIX_WIDTH = D_MODEL
ATTN_WIDTH = MIX_WIDTH // 2
LRU_WIDTH = MIX_WIDTH - ATTN_WIDTH
HEAD_DIM = 64
N_Q_HEADS = ATTN_WIDTH // HEAD_DIM
N_KV_HEADS = 4
Q_PER_KV = N_Q_HEADS // N_KV_HEADS
WINDOW = 128
BLOCK = 128
ROPE_THETA = 10000.0
LRU_BLOCK = 64
N_LRU_BLOCKS = LRU_WIDTH // LRU_BLOCK
CONV_WIDTH = 4
CONV_PAD_LEFT = 2
CONV_PAD_RIGHT = CONV_WIDTH - 1 - CONV_PAD_LEFT
LRU_C = 8.0
PEER_HEADS = 8
PEER_QDIM = 256
PEER_HALF = PEER_QDIM // 2
N_KEYS = 128
N_EXPERTS = N_KEYS * N_KEYS
PEER_TOPK = 16
PEER_CHUNK = 128
EPS = 1e-6
NEG = -1e30

Q_COLS = N_Q_HEADS * HEAD_DIM
KV_COLS = N_KV_HEADS * HEAD_DIM
IN_COLS = Q_COLS + 2 * KV_COLS + 2 * LRU_WIDTH

kernel_name = "hymba_swa_rglru_peer_encoder"


def rms_norm(x, g):
    xf = x.astype(jnp.float32)
    y = xf * lax.rsqrt(jnp.mean(xf * xf, axis=-1, keepdims=True) + EPS)
    return (y * g.astype(jnp.float32)).astype(x.dtype)


def rope(x, pos):
    half = HEAD_DIM // 2
    inv = ROPE_THETA ** (-jnp.arange(half, dtype=jnp.float32) / half)
    ang = pos.astype(jnp.float32)[:, None] * inv[None, :]
    cos = jnp.cos(ang)[None, :, None, :]
    sin = jnp.sin(ang)[None, :, None, :]
    xf = x.astype(jnp.float32)
    x1, x2 = xf[..., :half], xf[..., half:]
    return jnp.concatenate([x1 * cos - x2 * sin, x2 * cos + x1 * sin], axis=-1).astype(x.dtype)


def windowed_gqa_sink(q, k, v, sink):
    B, S = q.shape[0], q.shape[1]
    nb = S // BLOCK
    qb = q.reshape(B, nb, BLOCK, N_KV_HEADS, Q_PER_KV, HEAD_DIM)

    def band(t):
        tp = jnp.pad(t, ((0, 0), (BLOCK, BLOCK), (0, 0), (0, 0)))
        tp = tp.reshape(B, nb + 2, BLOCK, N_KV_HEADS, HEAD_DIM)
        return jnp.concatenate([tp[:, :-2], tp[:, 1:-1], tp[:, 2:]], axis=2)

    kb, vb = band(k), band(v)
    s = jnp.einsum('bnqgrd,bnkgd->bngrqk', qb, kb,
                   preferred_element_type=jnp.float32) * (HEAD_DIM ** -0.5)
    qpos = jnp.arange(nb)[:, None] * BLOCK + jnp.arange(BLOCK)[None, :]
    kpos = (jnp.arange(nb)[:, None] - 1) * BLOCK + jnp.arange(3 * BLOCK)[None, :]
    rel = kpos[:, None, :] - qpos[:, :, None]
    valid = (jnp.abs(rel) <= WINDOW) & (kpos[:, None, :] >= 0) & (kpos[:, None, :] < S)
    s = jnp.where(valid[None, :, None, None], s, NEG)
    sk = sink.astype(jnp.float32).reshape(1, 1, N_KV_HEADS, Q_PER_KV, 1, 1)
    m = jnp.maximum(jnp.max(s, axis=-1, keepdims=True), sk)
    p = jnp.exp(s - m)
    p = p / (jnp.sum(p, axis=-1, keepdims=True) + jnp.exp(sk - m))
    o = jnp.einsum('bngrqk,bnkgd->bnqgrd', p.astype(v.dtype), vb)
    return o.reshape(B, S, Q_COLS)


def centred_depthwise_conv(x, w, b):
    C = x.shape[-1]
    y = lax.conv_general_dilated(x, w.reshape(CONV_WIDTH, 1, C), window_strides=(1,),
                                 padding=[(CONV_PAD_LEFT, CONV_PAD_RIGHT)],
                                 dimension_numbers=('NWC', 'WIO', 'NWC'),
                                 feature_group_count=C)
    return y + b


def rg_lru(x, w_a, b_a, w_x, b_x, lam, reverse):
    B, S, _ = x.shape
    xb = x.reshape(B, S, N_LRU_BLOCKS, LRU_BLOCK)
    r = jax.nn.sigmoid(jnp.einsum('bshi,hij->bshj', xb, w_a).reshape(B, S, LRU_WIDTH) + b_a)
    i = jax.nn.sigmoid(jnp.einsum('bshi,hij->bshj', xb, w_x).reshape(B, S, LRU_WIDTH) + b_x)
    log_a = (-LRU_C * r.astype(jnp.float32)) * jax.nn.softplus(-lam.astype(jnp.float32))
    a = jnp.exp(log_a)
    u = jnp.sqrt(-jnp.expm1(2.0 * log_a)) * (i * x).astype(jnp.float32)

    def combine(left, right):
        a1, b1 = left
        a2, b2 = right
        return a1 * a2, a2 * b1 + b2

    _, h = lax.associative_scan(combine, (a, u), axis=1, reverse=reverse)
    return h.astype(x.dtype)


def peer(x, w_pq, sub_k1, sub_k2, u_emb, v_emb):
    B, S, D = x.shape
    T = B * S
    xt = x.reshape(T, D)
    q = (xt @ w_pq).reshape(T, PEER_HEADS, 2, PEER_HALF)
    s1 = jnp.einsum('thd,hkd->thk', q[:, :, 0], sub_k1, preferred_element_type=jnp.float32)
    s2 = jnp.einsum('thd,hkd->thk', q[:, :, 1], sub_k2, preferred_element_type=jnp.float32)
    v1, i1 = lax.top_k(s1, PEER_TOPK)
    v2, i2 = lax.top_k(s2, PEER_TOPK)
    cand = (v1[..., :, None] + v2[..., None, :]).reshape(T, PEER_HEADS, PEER_TOPK * PEER_TOPK)
    cidx = (i1[..., :, None] * N_KEYS + i2[..., None, :]).reshape(T, PEER_HEADS, PEER_TOPK * PEER_TOPK)
    sc, sel = lax.top_k(cand, PEER_TOPK)
    eidx = jnp.take_along_axis(cidx, sel, axis=-1)
    gates = jax.nn.softmax(sc, axis=-1)
    nc = T // PEER_CHUNK
    E = PEER_HEADS * PEER_TOPK

    def expert_chunk(args):
        xc, ic, gc = args
        uc = u_emb[ic]
        act = jax.nn.gelu(jnp.einsum('cd,ced->ce', xc, uc))
        vc = v_emb[ic]
        return jnp.einsum('ce,ced->cd', (gc * act).astype(vc.dtype), vc)

    out = lax.map(expert_chunk, (xt.reshape(nc, PEER_CHUNK, D),
                                 eidx.reshape(nc, PEER_CHUNK, E),
                                 gates.reshape(nc, PEER_CHUNK, E)))
    return out.reshape(B, S, D).astype(x.dtype)


def setup_inputs(seed: int = 0) -> dict:
    key = jax.random.key(seed)
    ks = jax.random.split(key, 32)
    f32 = jnp.float32
    L = DEPTH

    def nrm(k, shape, scale):
        return jax.random.normal(k, shape, f32) * scale

    def gain(k, n):
        return 1.0 + 0.02 * jax.random.normal(k, (L, n), f32)

    def lam_init(k):
        a_c = jax.random.uniform(k, (L, LRU_WIDTH), f32, 0.9, 0.999)
        a = a_c ** (1.0 / LRU_C)
        return jnp.log(a) - jnp.log1p(-a)

    gs = LRU_BLOCK ** -0.5
    return {
        "x": jax.random.normal(ks[0], (BATCH, SEQ, D_MODEL), f32),
        "g_mix": gain(ks[1], D_MODEL),
        "w_in": nrm(ks[2], (L, D_MODEL, IN_COLS), D_MODEL ** -0.5),
        "sink": nrm(ks[3], (L, N_Q_HEADS), 0.5),
        "conv_w": nrm(ks[4], (L, CONV_WIDTH, LRU_WIDTH), CONV_WIDTH ** -0.5),
        "conv_b": nrm(ks[5], (L, LRU_WIDTH), 0.01),
        "fwd_wa": nrm(ks[6], (L, N_LRU_BLOCKS, LRU_BLOCK, LRU_BLOCK), gs),
        "fwd_ba": nrm(ks[7], (L, LRU_WIDTH), 0.01),
        "fwd_wx": nrm(ks[8], (L, N_LRU_BLOCKS, LRU_BLOCK, LRU_BLOCK), gs),
        "fwd_bx": nrm(ks[9], (L, LRU_WIDTH), 0.01),
        "fwd_lam": lam_init(ks[10]),
        "bwd_wa": nrm(ks[11], (L, N_LRU_BLOCKS, LRU_BLOCK, LRU_BLOCK), gs),
        "bwd_ba": nrm(ks[12], (L, LRU_WIDTH), 0.01),
        "bwd_wx": nrm(ks[13], (L, N_LRU_BLOCKS, LRU_BLOCK, LRU_BLOCK), gs),
        "bwd_bx": nrm(ks[14], (L, LRU_WIDTH), 0.01),
        "bwd_lam": lam_init(ks[15]),
        "g_attn_out": gain(ks[16], ATTN_WIDTH),
        "g_lru_out": gain(ks[17], LRU_WIDTH),
        "w_out": nrm(ks[18], (L, MIX_WIDTH, D_MODEL), MIX_WIDTH ** -0.5),
        "g_ffn": gain(ks[19], D_MODEL),
        "w_pq": nrm(ks[20], (L, D_MODEL, PEER_HEADS * PEER_QDIM), D_MODEL ** -0.5),
        "sub_k1": nrm(ks[21], (L, PEER_HEADS, N_KEYS, PEER_HALF), PEER_HALF ** -0.5),
        "sub_k2": nrm(ks[22], (L, PEER_HEADS, N_KEYS, PEER_HALF), PEER_HALF ** -0.5),
        "u_emb": nrm(ks[23], (L, N_EXPERTS, D_MODEL), D_MODEL ** -0.5),
        "v_emb": nrm(ks[24], (L, N_EXPERTS, D_MODEL), (PEER_HEADS * PEER_TOPK) ** -0.5),
        "g_final": 1.0 + 0.02 * jax.random.normal(ks[25], (D_MODEL,), f32),
    }


def reference(x, g_mix, w_in, sink, conv_w, conv_b,
              fwd_wa, fwd_ba, fwd_wx, fwd_bx, fwd_lam,
              bwd_wa, bwd_ba, bwd_wx, bwd_bx, bwd_lam,
              g_attn_out, g_lru_out, w_out, g_ffn,
              w_pq, sub_k1, sub_k2, u_emb, v_emb, g_final):
    B, S, D = x.shape
    pos = jnp.arange(S)
    splits = [Q_COLS, Q_COLS + KV_COLS, Q_COLS + 2 * KV_COLS, Q_COLS + 2 * KV_COLS + LRU_WIDTH]
    for l in range(DEPTH):
        h = rms_norm(x, g_mix[l])
        proj = h @ w_in[l]
        q, k, v, x_gate, x_rec = jnp.split(proj, splits, axis=-1)
        q = rope(q.reshape(B, S, N_Q_HEADS, HEAD_DIM), pos)
        k = rope(k.reshape(B, S, N_KV_HEADS, HEAD_DIM), pos)
        v = v.reshape(B, S, N_KV_HEADS, HEAD_DIM)
        y_attn = windowed_gqa_sink(q, k, v, sink[l])

        c = centred_depthwise_conv(x_rec, conv_w[l], conv_b[l])
        h_rec = (rg_lru(c, fwd_wa[l], fwd_ba[l], fwd_wx[l], fwd_bx[l], fwd_lam[l], False)
                 + rg_lru(c, bwd_wa[l], bwd_ba[l], bwd_wx[l], bwd_bx[l], bwd_lam[l], True))
        y_lru = jax.nn.gelu(x_gate) * h_rec

        y = jnp.concatenate([rms_norm(y_attn, g_attn_out[l]), rms_norm(y_lru, g_lru_out[l])], axis=-1)
        x = x + y @ w_out[l]
        x = x + peer(rms_norm(x, g_ffn[l]), w_pq[l], sub_k1[l], sub_k2[l], u_emb[l], v_emb[l])
    return rms_norm(x, g_final)
```

```python
import functools

import jax
import jax.numpy as jnp
from jax import lax
from jax.experimental import pallas as pl
from jax.experimental.pallas import tpu as pltpu

F32 = jnp.float32
BF16 = jnp.bfloat16

HEAD_DIM = 64
N_KV_HEADS = 4
Q_PER_KV = 4
N_Q_HEADS = N_KV_HEADS * Q_PER_KV
WINDOW = 128
ATTN_BLOCK = 128
ROPE_THETA = 10000.0
LRU_BLOCK = 64
CONV_WIDTH = 4
CONV_PAD_LEFT = 2
LRU_C = 8.0
PEER_HEADS = 8
PEER_HALF = 128
N_KEYS = 128
PEER_TOPK = 16
EPS = 1e-6
NEG = -1e30

LANES = 128
SUBLANES = 8
VMEM_LIMIT = 56 * 1024 * 1024

INPROJ_TM = 512
INPROJ_TN = 512
LRU_CHANNELS = 256
LRU_CHUNK = 256
OUTPROJ_TM = 256
PQ_TM = 256
TOPK_TL = 512
EXPERT_BLOCK = 4096
PEER_TM = 128


def _cparams(sem):
    return pltpu.CompilerParams(dimension_semantics=sem, vmem_limit_bytes=VMEM_LIMIT)


def _rms(x, g):
    return x * lax.rsqrt(jnp.mean(x * x, axis=-1, keepdims=True) + EPS) * g


def _inproj_kernel(x_ref, g_ref, w_ref, cos_ref, sin_ref, qkv_ref, gr_ref, h_ref, *, n_rope_tiles, kv_cols):
    j = pl.program_id(1)

    @pl.when(j == 0)
    def _():
        h_ref[...] = _rms(x_ref[...], g_ref[...]).astype(BF16)

    acc = jnp.dot(h_ref[...], w_ref[...], preferred_element_type=F32)
    tm, tn = acc.shape

    @pl.when(j < n_rope_tiles)
    def _():
        reps = tn // LANES
        cos = jnp.concatenate([cos_ref[...]] * reps, axis=1)
        sin = jnp.concatenate([sin_ref[...]] * reps, axis=1)
        lane = lax.broadcasted_iota(jnp.int32, (tm, tn), 1)
        first = (lane % HEAD_DIM) < (HEAD_DIM // 2)
        partner = jnp.where(first, pltpu.roll(acc, tn - HEAD_DIM // 2, 1), pltpu.roll(acc, HEAD_DIM // 2, 1))
        roped = acc * cos + partner * sin
        is_rope = jnp.logical_or(j < n_rope_tiles - 1, lane < kv_cols)
        qkv_ref[...] = jnp.where(is_rope, roped, acc).astype(BF16)

    @pl.when(j >= n_rope_tiles)
    def _():
        gr_ref[...] = acc


def _inproj(xt, g_mix, w_in_bf, cos_t, sin_t, seq):
    T, D = xt.shape
    n_cols = w_in_bf.shape[1]
    q_cols = N_Q_HEADS * HEAD_DIM
    kv_cols = N_KV_HEADS * HEAD_DIM
    qkv_cols = q_cols + 2 * kv_cols
    tm, tn = INPROJ_TM, INPROJ_TN
    n_rope_tiles = qkv_cols // tn
    assert qkv_cols % tn == 0 and (q_cols % tn == 0) and seq % tm == 0 and T % tm == 0
    pos_blocks = seq // tm
    kern = functools.partial(_inproj_kernel, n_rope_tiles=n_rope_tiles, kv_cols=kv_cols)
    return pl.pallas_call(
        kern,
        grid=(T // tm, n_cols // tn),
        in_specs=[
            pl.BlockSpec((tm, D), lambda i, j: (i, 0)),
            pl.BlockSpec((1, D), lambda i, j: (0, 0)),
            pl.BlockSpec((D, tn), lambda i, j: (0, j)),
            pl.BlockSpec((tm, LANES), lambda i, j: (i % pos_blocks, 0)),
            pl.BlockSpec((tm, LANES), lambda i, j: (i % pos_blocks, 0)),
        ],
        out_specs=[
            pl.BlockSpec((tm, tn), lambda i, j: (i, jnp.minimum(j, n_rope_tiles - 1))),
            pl.BlockSpec((tm, tn), lambda i, j: (i, jnp.maximum(j - n_rope_tiles, 0))),
        ],
        out_shape=[
            jax.ShapeDtypeStruct((T, qkv_cols), BF16),
            jax.ShapeDtypeStruct((T, n_cols - qkv_cols), F32),
        ],
        scratch_shapes=[pltpu.VMEM((tm, D), BF16)],
        compiler_params=_cparams(("parallel", "arbitrary")),
        name="inproj",
    )(xt, g_mix, w_in_bf, cos_t, sin_t)


def _attn_kernel(sink_ref, q_ref, kp_ref, kc_ref, kn_ref, vp_ref, vc_ref, vn_ref, o_ref, *, seq):
    n = pl.program_id(1)
    q = q_ref[...]
    k = jnp.concatenate([kp_ref[...], kc_ref[...], kn_ref[...]], axis=0)
    v = jnp.concatenate([vp_ref[...], vc_ref[...], vn_ref[...]], axis=0)
    nq, nk = ATTN_BLOCK, 3 * ATTN_BLOCK
    qpos = n * ATTN_BLOCK + lax.broadcasted_iota(jnp.int32, (nq, nk), 0)
    kpos = (n - 1) * ATTN_BLOCK + lax.broadcasted_iota(jnp.int32, (nq, nk), 1)
    valid = (jnp.abs(kpos - qpos) <= WINDOW) & (kpos >= 0) & (kpos < seq)
    scale = HEAD_DIM ** -0.5
    outs = []
    for g in range(N_KV_HEADS):
        kg = k[:, g * HEAD_DIM:(g + 1) * HEAD_DIM]
        vg = v[:, g * HEAD_DIM:(g + 1) * HEAD_DIM]
        for r in range(Q_PER_KV):
            h = g * Q_PER_KV + r
            qh = q[:, h * HEAD_DIM:(h + 1) * HEAD_DIM]
            s = lax.dot_general(qh, kg, (((1,), (1,)), ((), ())), preferred_element_type=F32) * scale
            s = jnp.where(valid, s, NEG)
            sk = sink_ref[h]
            m = jnp.maximum(jnp.max(s, axis=-1, keepdims=True), sk)
            p = jnp.exp(s - m)
            den = jnp.sum(p, axis=-1, keepdims=True) + jnp.exp(sk - m)
            o = jnp.dot(p.astype(BF16), vg, preferred_element_type=F32) / den
            outs.append(o)
    o_ref[...] = jnp.concatenate(outs, axis=1)


def _attention(qkv, sink, batch, seq):
    T = qkv.shape[0]
    nb = seq // ATTN_BLOCK
    q_cols = N_Q_HEADS * HEAD_DIM
    kv_cols = N_KV_HEADS * HEAD_DIM
    k_blk = q_cols // kv_cols
    v_blk = k_blk + 1
    blk = ATTN_BLOCK

    def row(b, n):
        return b * nb + n

    def kv_spec(col_blk, shift):
        return pl.BlockSpec((blk, kv_cols), lambda b, n: (row(b, jnp.clip(n + shift, 0, nb - 1)), col_blk))

    return pl.pallas_call(
        functools.partial(_attn_kernel, seq=seq),
        grid=(batch, nb),
        in_specs=[
            pl.BlockSpec(memory_space=pltpu.SMEM),
            pl.BlockSpec((blk, q_cols), lambda b, n: (row(b, n), 0)),
            kv_spec(k_blk, -1), kv_spec(k_blk, 0), kv_spec(k_blk, 1),
            kv_spec(v_blk, -1), kv_spec(v_blk, 0), kv_spec(v_blk, 1),
        ],
        out_specs=pl.BlockSpec((blk, q_cols), lambda b, n: (row(b, n), 0)),
        out_shape=jax.ShapeDtypeStruct((T, q_cols), F32),
        compiler_params=_cparams(("parallel", "parallel")),
        name="attn",
    )(sink, qkv, qkv, qkv, qkv, qkv, qkv, qkv)


def _scan_chunk(a, u, reverse):
    L, C = a.shape
    row = lax.broadcasted_iota(jnp.int32, (L, C), 0)
    A, H = a, u
    d = 1
    while d < L:
        if d < SUBLANES:
            if reverse:
                keep = row < L - d
                As = jnp.where(keep, pltpu.roll(A, L - d, 0), 1.0)
                Hs = jnp.where(keep, pltpu.roll(H, L - d, 0), 0.0)
            else:
                keep = row >= d
                As = jnp.where(keep, pltpu.roll(A, d, 0), 1.0)
                Hs = jnp.where(keep, pltpu.roll(H, d, 0), 0.0)
        else:
            one = jnp.ones((d, C), F32)
            zero = jnp.zeros((d, C), F32)
            if reverse:
                As = jnp.concatenate([A[d:], one], axis=0)
                Hs = jnp.concatenate([H[d:], zero], axis=0)
            else:
                As = jnp.concatenate([one, A[:L - d]], axis=0)
                Hs = jnp.concatenate([zero, H[:L - d]], axis=0)
        H = A * Hs + H
        A = A * As
        d *= 2
    return A, H


def _lru_kernel(xg_ref, xr_ref, cw_ref, cb_ref, wg_ref, bg_ref, lam_ref, y_ref, xp_ref, hf_ref):
    S, C = xr_ref.shape
    L = LRU_CHUNK
    n_chunks = S // L
    halo = SUBLANES

    xp_ref[0:halo, :] = jnp.zeros((halo, C), F32)
    xp_ref[S + halo:S + 2 * halo, :] = jnp.zeros((halo, C), F32)

    def copy_body(ci, carry):
        t0 = pl.multiple_of(ci * L, L)
        xp_ref[pl.ds(t0 + halo, L), :] = xr_ref[pl.ds(t0, L), :]
        return carry

    lax.fori_loop(0, n_chunks, copy_body, 0)

    cw = cw_ref[...]
    cb = cb_ref[...]
    bias = bg_ref[0]
    neg_c_softplus = -LRU_C * jax.nn.softplus(-lam_ref[0])

    def conv_chunk(t0):
        win = xp_ref[pl.ds(t0, L + 2 * halo), :]
        acc = cb
        for j in range(CONV_WIDTH):
            off = halo - CONV_PAD_LEFT + j
            acc = acc + cw[j:j + 1, :] * win[off:off + L, :]
        return acc

    def gate_au(c, direction):
        w = wg_ref[0, :, direction * 2 * C:(direction + 1) * 2 * C]
        z = jnp.dot(c.astype(BF16), w, preferred_element_type=F32) + bias[:, direction * 2 * C:(direction + 1) * 2 * C]
        r = jax.nn.sigmoid(z[:, :C])
        i = jax.nn.sigmoid(z[:, C:])
        log_a = r * neg_c_softplus[direction:direction + 1, :]
        a = jnp.exp(log_a)
        u = jnp.sqrt(1.0 - a * a) * (i * c)
        return a, u

    def fwd_body(ci, carry):
        t0 = pl.multiple_of(ci * L, L)
        c = conv_chunk(t0)
        a, u = gate_au(c, 0)
        A, H = _scan_chunk(a, u, reverse=False)
        h = H + A * carry
        hf_ref[pl.ds(t0, L), :] = h
        return h[L - 1:L, :]

    lax.fori_loop(0, n_chunks, fwd_body, jnp.zeros((1, C), F32))

    def bwd_body(k, carry):
        t0 = pl.multiple_of((n_chunks - 1 - k) * L, L)
        c = conv_chunk(t0)
        a, u = gate_au(c, 1)
        A, H = _scan_chunk(a, u, reverse=True)
        h = H + A * carry
        y_ref[pl.ds(t0, L), :] = jax.nn.gelu(xg_ref[pl.ds(t0, L), :]) * (hf_ref[pl.ds(t0, L), :] + h)
        return h[0:1, :]

    lax.fori_loop(0, n_chunks, bwd_body, jnp.zeros((1, C), F32))


def _block_diag_chunks(w, per_chunk):
    nblk, b, _ = w.shape
    w4 = w.reshape(nblk // per_chunk, per_chunk, b, b)
    eye = jnp.eye(per_chunk, dtype=w.dtype)
    m = w4[:, :, :, None, :] * eye[None, :, None, :, None]
    return m.reshape(nblk // per_chunk, per_chunk * b, per_chunk * b)


def _lru(gr, conv_w, conv_b, gate_ws, gate_bs, lams, batch, seq):
    T, two_w = gr.shape
    W = two_w // 2
    C = LRU_CHANNELS
    n_ch = W // C
    per_chunk = C // LRU_BLOCK
    wg = jnp.concatenate([_block_diag_chunks(w, per_chunk) for w in gate_ws], axis=-1).astype(BF16)
    bg = jnp.concatenate([b.reshape(n_ch, 1, C) for b in gate_bs], axis=-1)
    lam = jnp.stack([l.reshape(n_ch, C) for l in lams], axis=1)
    return pl.pallas_call(
        _lru_kernel,
        grid=(batch, n_ch),
        in_specs=[
            pl.BlockSpec((seq, C), lambda b, c: (b, c)),
            pl.BlockSpec((seq, C), lambda b, c: (b, n_ch + c)),
            pl.BlockSpec((CONV_WIDTH, C), lambda b, c: (0, c)),
            pl.BlockSpec((1, C), lambda b, c: (0, c)),
            pl.BlockSpec((1, C, 4 * C), lambda b, c: (c, 0, 0)),
            pl.BlockSpec((1, 1, 4 * C), lambda b, c: (c, 0, 0)),
            pl.BlockSpec((1, 2, C), lambda b, c: (c, 0, 0)),
        ],
        out_specs=pl.BlockSpec((seq, C), lambda b, c: (b, c)),
        out_shape=jax.ShapeDtypeStruct((T, W), F32),
        scratch_shapes=[pltpu.VMEM((seq + 2 * SUBLANES, C), F32), pltpu.VMEM((seq, C), F32)],
        compiler_params=_cparams(("parallel", "parallel")),
        name="lru",
    )(gr, gr, conv_w, conv_b.reshape(1, W), wg, bg, lam)


def _outproj_kernel(ya_ref, yl_ref, x_ref, ga_ref, gl_ref, w_ref, gf_ref, x1_ref, xn_ref):
    y = jnp.concatenate([_rms(ya_ref[...], ga_ref[...]), _rms(yl_ref[...], gl_ref[...])], axis=1).astype(BF16)
    x1 = x_ref[...] + jnp.dot(y, w_ref[...], preferred_element_type=F32)
    x1_ref[...] = x1
    xn_ref[...] = _rms(x1, gf_ref[...])


def _outproj(y_attn, y_lru, xt, g_attn, g_lru, w_out_bf, g_ffn):
    T, D = xt.shape
    wa, wl = y_attn.shape[1], y_lru.shape[1]
    tm = OUTPROJ_TM
    return pl.pallas_call(
        _outproj_kernel,
        grid=(T // tm,),
        in_specs=[
            pl.BlockSpec((tm, wa), lambda i: (i, 0)),
            pl.BlockSpec((tm, wl), lambda i: (i, 0)),
            pl.BlockSpec((tm, D), lambda i: (i, 0)),
            pl.BlockSpec((1, wa), lambda i: (0, 0)),
            pl.BlockSpec((1, wl), lambda i: (0, 0)),
            pl.BlockSpec((wa + wl, D), lambda i: (0, 0)),
            pl.BlockSpec((1, D), lambda i: (0, 0)),
        ],
        out_specs=[pl.BlockSpec((tm, D), lambda i: (i, 0)), pl.BlockSpec((tm, D), lambda i: (i, 0))],
        out_shape=[jax.ShapeDtypeStruct((T, D), F32), jax.ShapeDtypeStruct((T, D), F32)],
        compiler_params=_cparams(("parallel",)),
        name="outproj",
    )(y_attn, y_lru, xt, g_attn, g_lru, w_out_bf, g_ffn)


def _pq_kernel(xn_ref, w_ref, k_ref, s_ref):
    q = jnp.dot(xn_ref[...].astype(BF16), w_ref[...], preferred_element_type=F32).astype(BF16)
    for hh in range(2 * PEER_HEADS):
        half, head = divmod(hh, PEER_HEADS)
        col = (head * 2 + half) * PEER_HALF
        qh = q[:, col:col + PEER_HALF]
        s_ref[hh] = lax.dot_general(k_ref[hh], qh, (((1,), (1,)), ((), ())), preferred_element_type=F32)


def _peer_scores(xn, w_pq_bf, keys_bf):
    T, D = xn.shape
    tm = PQ_TM
    nh = keys_bf.shape[0]
    return pl.pallas_call(
        _pq_kernel,
        grid=(T // tm,),
        in_specs=[
            pl.BlockSpec((tm, D), lambda i: (i, 0)),
            pl.BlockSpec(w_pq_bf.shape, lambda i: (0, 0)),
            pl.BlockSpec(keys_bf.shape, lambda i: (0, 0, 0)),
        ],
        out_specs=pl.BlockSpec((nh, N_KEYS, tm), lambda i: (0, 0, i)),
        out_shape=jax.ShapeDtypeStruct((nh, N_KEYS, T), F32),
        compiler_params=_cparams(("parallel",)),
        name="pq",
    )(xn, w_pq_bf, keys_bf)


def _top16_rows(s, payload=None):
    n, tl = s.shape
    row = lax.broadcasted_iota(jnp.int32, (n, tl), 0)
    vals, idxs, pays = [], [], []
    for _ in range(PEER_TOPK):
        m = jnp.max(s, axis=0, keepdims=True)
        idx = jnp.min(jnp.where(s == m, row, n), axis=0, keepdims=True)
        hit = row == idx
        vals.append(m)
        idxs.append(idx)
        if payload is not None:
            pays.append(jnp.max(jnp.where(hit, payload, -1), axis=0, keepdims=True))
        s = jnp.where(hit, -jnp.inf, s)
    cat = lambda xs: jnp.concatenate(xs, axis=0)
    return cat(vals), cat(idxs), (cat(pays) if payload is not None else None)


def _topk_kernel(s1_ref, s2_ref, g_ref, e_ref):
    v1, i1, _ = _top16_rows(s1_ref[0])
    v2, i2, _ = _top16_rows(s2_ref[0])
    cand = jnp.concatenate([v1[a:a + 1, :] + v2 for a in range(PEER_TOPK)], axis=0)
    cidx = jnp.concatenate([i1[a:a + 1, :] * N_KEYS + i2 for a in range(PEER_TOPK)], axis=0)
    sc, _, eidx = _top16_rows(cand, cidx)
    ex = jnp.exp(sc - sc[0:1, :])
    g_ref[0] = ex / jnp.sum(ex, axis=0, keepdims=True)
    e_ref[0] = eidx


def _peer_topk(scores):
    nh2, nk, T = scores.shape
    nh = nh2 // 2
    tl = TOPK_TL
    return pl.pallas_call(
        _topk_kernel,
        grid=(nh, T // tl),
        in_specs=[
            pl.BlockSpec((1, nk, tl), lambda h, i: (h, 0, i)),
            pl.BlockSpec((1, nk, tl), lambda h, i: (nh + h, 0, i)),
        ],
        out_specs=[
            pl.BlockSpec((1, PEER_TOPK, tl), lambda h, i: (h, 0, i)),
            pl.BlockSpec((1, PEER_TOPK, tl), lambda h, i: (h, 0, i)),
        ],
        out_shape=[
            jax.ShapeDtypeStruct((nh, PEER_TOPK, T), F32),
            jax.ShapeDtypeStruct((nh, PEER_TOPK, T), jnp.int32),
        ],
        compiler_params=_cparams(("parallel", "parallel")),
        name="topk",
    )(scores, scores)


ROW_SUBLANES = 16


def _sublane_fold(parts):
    sub = lax.broadcasted_iota(jnp.int32, (SUBLANES, LANES), 0)
    step = 1
    while len(parts) > 1:
        low = (sub & step) == 0
        parts = [jnp.where(low, a, b) + pltpu.roll(jnp.where(low, b, a), step, 0)
                 for a, b in zip(parts[0::2], parts[1::2])]
        step *= 2
    return parts[0]


def _pdot_kernel(idx_ref, x_ref, u_ref, d_ref, *, n_pairs):
    tm = x_ref.shape[0]
    n_chunks = n_pairs // SUBLANES

    def body(tl, carry):
        x = x_ref[tl]
        base = tl * n_pairs
        folded = []
        for c in range(n_chunks):
            parts = []
            for k in range(SUBLANES):
                e = idx_ref[base + c * SUBLANES + k]
                p = u_ref[e].astype(F32) * x
                parts.append(p[:SUBLANES] + p[SUBLANES:])
            folded.append(_sublane_fold(parts))
        r = jnp.concatenate(folded, axis=0)
        d_ref[tl] = jnp.sum(r.T, axis=0, keepdims=True)
        return carry

    lax.fori_loop(0, tm, body, 0)


def _peer_dots(idx_flat, x3, u_rows, n_pairs):
    T = x3.shape[0]
    n_exp = u_rows.shape[0]
    nb = n_exp // EXPERT_BLOCK
    tm = PEER_TM
    return pl.pallas_call(
        functools.partial(_pdot_kernel, n_pairs=n_pairs),
        grid=(nb, T // tm),
        in_specs=[
            pl.BlockSpec((tm * n_pairs,), lambda b, i: (i,), memory_space=pltpu.SMEM),
            pl.BlockSpec((tm, ROW_SUBLANES, LANES), lambda b, i: (i, 0, 0)),
            pl.BlockSpec((EXPERT_BLOCK, ROW_SUBLANES, LANES), lambda b, i: (b, 0, 0)),
        ],
        out_specs=pl.BlockSpec((None, tm, 1, n_pairs), lambda b, i: (b, i, 0, 0)),
        out_shape=jax.ShapeDtypeStruct((nb, T, 1, n_pairs), F32),
        compiler_params=_cparams(("arbitrary", "arbitrary")),
        name="pdot",
    )(idx_flat, x3, u_rows)


def _act_kernel(d_ref, blk_ref, g_ref, w_ref):
    nb = d_ref.shape[0]
    blk = blk_ref[...]
    dot = jnp.zeros(blk.shape, F32)
    for b in range(nb):
        dot = jnp.where(blk == b, d_ref[b], dot)
    w = g_ref[...] * jax.nn.gelu(dot)
    for b in range(nb):
        w_ref[b] = jnp.where(blk == b, w, 0.0)


def _peer_act(dots, blk, gates):
    nb, T, n_pairs = dots.shape
    tm = min(1024, T)
    return pl.pallas_call(
        _act_kernel,
        grid=(T // tm,),
        in_specs=[
            pl.BlockSpec((nb, tm, n_pairs), lambda i: (0, i, 0)),
            pl.BlockSpec((tm, n_pairs), lambda i: (i, 0)),
            pl.BlockSpec((tm, n_pairs), lambda i: (i, 0)),
        ],
        out_specs=pl.BlockSpec((nb, tm, n_pairs), lambda i: (0, i, 0)),
        out_shape=jax.ShapeDtypeStruct((nb, T, n_pairs), F32),
        compiler_params=_cparams(("parallel",)),
        name="act",
    )(dots, blk, gates)


def _pacc_kernel(idx_ref, w_ref, v_ref, o_ref, *, n_pairs):
    tm = o_ref.shape[0]

    def body(tl, carry):
        base = tl * n_pairs
        acc = jnp.zeros((ROW_SUBLANES, LANES), F32)
        for j in range(n_pairs):
            acc = acc + w_ref[base + j] * v_ref[idx_ref[base + j]].astype(F32)
        o_ref[tl] = acc
        return carry

    lax.fori_loop(0, tm, body, 0)


def _peer_accumulate(idx_flat, w_flat, v_rows, T, n_pairs):
    n_exp = v_rows.shape[0]
    nb = n_exp // EXPERT_BLOCK
    tm = PEER_TM
    n_tiles = T // tm
    return pl.pallas_call(
        functools.partial(_pacc_kernel, n_pairs=n_pairs),
        grid=(nb, n_tiles),
        in_specs=[
            pl.BlockSpec((tm * n_pairs,), lambda b, i: (i,), memory_space=pltpu.SMEM),
            pl.BlockSpec((tm * n_pairs,), lambda b, i: (b * n_tiles + i,), memory_space=pltpu.SMEM),
            pl.BlockSpec((EXPERT_BLOCK, ROW_SUBLANES, LANES), lambda b, i: (b, 0, 0)),
        ],
        out_specs=pl.BlockSpec((None, tm, ROW_SUBLANES, LANES), lambda b, i: (b, i, 0, 0)),
        out_shape=jax.ShapeDtypeStruct((nb, T, ROW_SUBLANES, LANES), F32),
        compiler_params=_cparams(("arbitrary", "arbitrary")),
        name="pacc",
    )(idx_flat, w_flat, v_rows)


def _final_kernel(x1_ref, p_ref, g_ref, o_ref):
    acc = x1_ref[...]
    for b in range(p_ref.shape[0]):
        acc = acc + p_ref[b]
    o_ref[...] = _rms(acc, g_ref[...])


def _final(x1, peer_parts, g_final):
    T, D = x1.shape
    nb = peer_parts.shape[0]
    tm = 256
    return pl.pallas_call(
        _final_kernel,
        grid=(T // tm,),
        in_specs=[
            pl.BlockSpec((tm, D), lambda i: (i, 0)),
            pl.BlockSpec((nb, tm, D), lambda i: (0, i, 0)),
            pl.BlockSpec((1, D), lambda i: (0, 0)),
        ],
        out_specs=pl.BlockSpec((tm, D), lambda i: (i, 0)),
        out_shape=jax.ShapeDtypeStruct((T, D), F32),
        compiler_params=_cparams(("parallel",)),
        name="final",
    )(x1, peer_parts, g_final)


def _rope_tables(seq):
    half = HEAD_DIM // 2
    inv = ROPE_THETA ** (-jnp.arange(half, dtype=F32) / half)
    ang = jnp.arange(seq).astype(F32)[:, None] * inv[None, :]
    cos, sin = jnp.cos(ang), jnp.sin(ang)
    reps = LANES // HEAD_DIM
    cos_t = jnp.concatenate([cos, cos] * reps, axis=1)
    sin_t = jnp.concatenate([-sin, sin] * reps, axis=1)
    return cos_t, sin_t


def _layer(xt, batch, seq, g_mix, w_in, sink, conv_w, conv_b, fwd_wa, fwd_ba, fwd_wx, fwd_bx, fwd_lam,
           bwd_wa, bwd_ba, bwd_wx, bwd_bx, bwd_lam, g_attn_out, g_lru_out, w_out, g_ffn,
           w_pq, sub_k1, sub_k2, u_emb, v_emb):
    T, D = xt.shape
    cos_t, sin_t = _rope_tables(seq)
    qkv, gr = _inproj(xt, g_mix.reshape(1, D), w_in.astype(BF16), cos_t, sin_t, seq)
    y_attn = _attention(qkv, sink, batch, seq)
    y_lru = _lru(gr, conv_w, conv_b, (fwd_wa, fwd_wx, bwd_wa, bwd_wx), (fwd_ba, fwd_bx, bwd_ba, bwd_bx),
                 (fwd_lam, bwd_lam), batch, seq)
    x1, xn = _outproj(y_attn, y_lru, xt, g_attn_out.reshape(1, -1), g_lru_out.reshape(1, -1),
                      w_out.astype(BF16), g_ffn.reshape(1, D))

    keys = jnp.concatenate([sub_k1, sub_k2], axis=0).astype(BF16)
    scores = _peer_scores(xn, w_pq.astype(BF16), keys)
    gates_t, eidx_t = _peer_topk(scores)
    n_pairs = PEER_HEADS * PEER_TOPK
    gates = gates_t.transpose(2, 0, 1).reshape(T, n_pairs)
    eidx = eidx_t.transpose(2, 0, 1).reshape(T, n_pairs)
    blk = eidx // EXPERT_BLOCK
    loc = (eidx % EXPERT_BLOCK).reshape(T * n_pairs)

    n_exp = u_emb.shape[0]
    x3 = xn.reshape(T, ROW_SUBLANES, LANES)
    u_rows = u_emb.astype(BF16).reshape(n_exp, ROW_SUBLANES, LANES)
    v_rows = v_emb.astype(BF16).reshape(n_exp, ROW_SUBLANES, LANES)
    dots = _peer_dots(loc, x3, u_rows, n_pairs)
    nb = dots.shape[0]
    w_blk = _peer_act(dots.reshape(nb, T, n_pairs), blk, gates)
    parts = _peer_accumulate(loc, w_blk.reshape(nb * T * n_pairs), v_rows, T, n_pairs)
    return x1, parts.reshape(nb, T, D)


def kernel(x, g_mix, w_in, sink, conv_w, conv_b, fwd_wa, fwd_ba, fwd_wx, fwd_bx, fwd_lam, bwd_wa, bwd_ba, bwd_wx, bwd_bx, bwd_lam, g_attn_out, g_lru_out, w_out, g_ffn, w_pq, sub_k1, sub_k2, u_emb, v_emb, g_final):
    B, S, D = x.shape
    assert g_mix.shape[0] == 1, "single-layer trunk"
    xt = x.reshape(B * S, D)
    x1, parts = _layer(xt, B, S, g_mix[0], w_in[0], sink[0], conv_w[0], conv_b[0],
                       fwd_wa[0], fwd_ba[0], fwd_wx[0], fwd_bx[0], fwd_lam[0],
                       bwd_wa[0], bwd_ba[0], bwd_wx[0], bwd_bx[0], bwd_lam[0],
                       g_attn_out[0], g_lru_out[0], w_out[0], g_ffn[0],
                       w_pq[0], sub_k1[0], sub_k2[0], u_emb[0], v_emb[0])
    return _final(x1, parts, g_final.reshape(1, D)).reshape(B, S, D)
```

```python
import functools

import jax
import jax.numpy as jnp
from jax import lax
from jax.experimental import pallas as pl
from jax.experimental.pallas import tpu as pltpu

F32 = jnp.float32
BF16 = jnp.bfloat16

HEAD_DIM = 64
N_KV_HEADS = 4
Q_PER_KV = 4
N_Q_HEADS = N_KV_HEADS * Q_PER_KV
WINDOW = 128
ATTN_BLOCK = 128
ROPE_THETA = 10000.0
LRU_BLOCK = 64
CONV_WIDTH = 4
CONV_PAD_LEFT = 2
LRU_C = 8.0
PEER_HEADS = 8
PEER_HALF = 128
N_KEYS = 128
PEER_TOPK = 16
EPS = 1e-6
NEG = -1e30

LANES = 128
SUBLANES = 8
VMEM_LIMIT = 56 * 1024 * 1024

INPROJ_TM = 512
INPROJ_TN = 512
LRU_CHANNELS = 256
LRU_CHUNK = 256
OUTPROJ_TM = 256
PQ_TM = 256
TOPK_TL = 512
EXPERT_BLOCK = 8192
PEER_TM = 128
CHUNK = 16
CS_STRIDE = 8


def _cparams(sem):
    return pltpu.CompilerParams(dimension_semantics=sem, vmem_limit_bytes=VMEM_LIMIT)


def _rms(x, g):
    return x * lax.rsqrt(jnp.mean(x * x, axis=-1, keepdims=True) + EPS) * g


def _inproj_kernel(x_ref, g_ref, w_ref, cos_ref, sin_ref, qkv_ref, gr_ref, h_ref, *, n_rope_tiles, kv_cols):
    j = pl.program_id(1)

    @pl.when(j == 0)
    def _():
        h_ref[...] = _rms(x_ref[...], g_ref[...]).astype(BF16)

    acc = jnp.dot(h_ref[...], w_ref[...], preferred_element_type=F32)
    tm, tn = acc.shape

    @pl.when(j < n_rope_tiles)
    def _():
        reps = tn // LANES
        cos = jnp.concatenate([cos_ref[...]] * reps, axis=1)
        sin = jnp.concatenate([sin_ref[...]] * reps, axis=1)
        lane = lax.broadcasted_iota(jnp.int32, (tm, tn), 1)
        first = (lane % HEAD_DIM) < (HEAD_DIM // 2)
        partner = jnp.where(first, pltpu.roll(acc, tn - HEAD_DIM // 2, 1), pltpu.roll(acc, HEAD_DIM // 2, 1))
        roped = acc * cos + partner * sin
        is_rope = jnp.logical_or(j < n_rope_tiles - 1, lane < kv_cols)
        qkv_ref[...] = jnp.where(is_rope, roped, acc).astype(BF16)

    @pl.when(j >= n_rope_tiles)
    def _():
        gr_ref[...] = acc


def _inproj(xt, g_mix, w_in_bf, cos_t, sin_t, seq):
    T, D = xt.shape
    n_cols = w_in_bf.shape[1]
    q_cols = N_Q_HEADS * HEAD_DIM
    kv_cols = N_KV_HEADS * HEAD_DIM
    qkv_cols = q_cols + 2 * kv_cols
    tm, tn = INPROJ_TM, INPROJ_TN
    n_rope_tiles = qkv_cols // tn
    assert qkv_cols % tn == 0 and (q_cols % tn == 0) and seq % tm == 0 and T % tm == 0
    pos_blocks = seq // tm
    kern = functools.partial(_inproj_kernel, n_rope_tiles=n_rope_tiles, kv_cols=kv_cols)
    return pl.pallas_call(
        kern,
        grid=(T // tm, n_cols // tn),
        in_specs=[
            pl.BlockSpec((tm, D), lambda i, j: (i, 0)),
            pl.BlockSpec((1, D), lambda i, j: (0, 0)),
            pl.BlockSpec((D, tn), lambda i, j: (0, j)),
            pl.BlockSpec((tm, LANES), lambda i, j: (i % pos_blocks, 0)),
            pl.BlockSpec((tm, LANES), lambda i, j: (i % pos_blocks, 0)),
        ],
        out_specs=[
            pl.BlockSpec((tm, tn), lambda i, j: (i, jnp.minimum(j, n_rope_tiles - 1))),
            pl.BlockSpec((tm, tn), lambda i, j: (i, jnp.maximum(j - n_rope_tiles, 0))),
        ],
        out_shape=[
            jax.ShapeDtypeStruct((T, qkv_cols), BF16),
            jax.ShapeDtypeStruct((T, n_cols - qkv_cols), F32),
        ],
        scratch_shapes=[pltpu.VMEM((tm, D), BF16)],
        compiler_params=_cparams(("parallel", "arbitrary")),
        name="inproj",
    )(xt, g_mix, w_in_bf, cos_t, sin_t)


def _attn_kernel(sink_ref, q_ref, kp_ref, kc_ref, kn_ref, vp_ref, vc_ref, vn_ref, o_ref, *, seq):
    n = pl.program_id(1)
    q = q_ref[...]
    k = jnp.concatenate([kp_ref[...], kc_ref[...], kn_ref[...]], axis=0)
    v = jnp.concatenate([vp_ref[...], vc_ref[...], vn_ref[...]], axis=0)
    nq, nk = ATTN_BLOCK, 3 * ATTN_BLOCK
    qpos = n * ATTN_BLOCK + lax.broadcasted_iota(jnp.int32, (nq, nk), 0)
    kpos = (n - 1) * ATTN_BLOCK + lax.broadcasted_iota(jnp.int32, (nq, nk), 1)
    valid = (jnp.abs(kpos - qpos) <= WINDOW) & (kpos >= 0) & (kpos < seq)
    scale = HEAD_DIM ** -0.5
    outs = []
    for g in range(N_KV_HEADS):
        kg = k[:, g * HEAD_DIM:(g + 1) * HEAD_DIM]
        vg = v[:, g * HEAD_DIM:(g + 1) * HEAD_DIM]
        for r in range(Q_PER_KV):
            h = g * Q_PER_KV + r
            qh = q[:, h * HEAD_DIM:(h + 1) * HEAD_DIM]
            s = lax.dot_general(qh, kg, (((1,), (1,)), ((), ())), preferred_element_type=F32) * scale
            s = jnp.where(valid, s, NEG)
            sk = sink_ref[h]
            m = jnp.maximum(jnp.max(s, axis=-1, keepdims=True), sk)
            p = jnp.exp(s - m)
            den = jnp.sum(p, axis=-1, keepdims=True) + jnp.exp(sk - m)
            o = jnp.dot(p.astype(BF16), vg, preferred_element_type=F32) / den
            outs.append(o)
    o_ref[...] = jnp.concatenate(outs, axis=1)


def _attention(qkv, sink, batch, seq):
    T = qkv.shape[0]
    nb = seq // ATTN_BLOCK
    q_cols = N_Q_HEADS * HEAD_DIM
    kv_cols = N_KV_HEADS * HEAD_DIM
    k_blk = q_cols // kv_cols
    v_blk = k_blk + 1
    blk = ATTN_BLOCK

    def row(b, n):
        return b * nb + n

    def kv_spec(col_blk, shift):
        return pl.BlockSpec((blk, kv_cols), lambda b, n: (row(b, jnp.clip(n + shift, 0, nb - 1)), col_blk))

    return pl.pallas_call(
        functools.partial(_attn_kernel, seq=seq),
        grid=(batch, nb),
        in_specs=[
            pl.BlockSpec(memory_space=pltpu.SMEM),
            pl.BlockSpec((blk, q_cols), lambda b, n: (row(b, n), 0)),
            kv_spec(k_blk, -1), kv_spec(k_blk, 0), kv_spec(k_blk, 1),
            kv_spec(v_blk, -1), kv_spec(v_blk, 0), kv_spec(v_blk, 1),
        ],
        out_specs=pl.BlockSpec((blk, q_cols), lambda b, n: (row(b, n), 0)),
        out_shape=jax.ShapeDtypeStruct((T, q_cols), F32),
        compiler_params=_cparams(("parallel", "parallel")),
        name="attn",
    )(sink, qkv, qkv, qkv, qkv, qkv, qkv, qkv)


def _scan_chunk(a, u, reverse):
    L, C = a.shape
    row = lax.broadcasted_iota(jnp.int32, (L, C), 0)
    A, H = a, u
    d = 1
    while d < L:
        if d < SUBLANES:
            if reverse:
                keep = row < L - d
                As = jnp.where(keep, pltpu.roll(A, L - d, 0), 1.0)
                Hs = jnp.where(keep, pltpu.roll(H, L - d, 0), 0.0)
            else:
                keep = row >= d
                As = jnp.where(keep, pltpu.roll(A, d, 0), 1.0)
                Hs = jnp.where(keep, pltpu.roll(H, d, 0), 0.0)
        else:
            one = jnp.ones((d, C), F32)
            zero = jnp.zeros((d, C), F32)
            if reverse:
                As = jnp.concatenate([A[d:], one], axis=0)
                Hs = jnp.concatenate([H[d:], zero], axis=0)
            else:
                As = jnp.concatenate([one, A[:L - d]], axis=0)
                Hs = jnp.concatenate([zero, H[:L - d]], axis=0)
        H = A * Hs + H
        A = A * As
        d *= 2
    return A, H


def _lru_kernel(xg_ref, xr_ref, cw_ref, cb_ref, wg_ref, bg_ref, lam_ref, y_ref, xp_ref, hf_ref):
    S, C = xr_ref.shape
    L = LRU_CHUNK
    n_chunks = S // L
    halo = SUBLANES

    xp_ref[0:halo, :] = jnp.zeros((halo, C), F32)
    xp_ref[S + halo:S + 2 * halo, :] = jnp.zeros((halo, C), F32)

    def copy_body(ci, carry):
        t0 = pl.multiple_of(ci * L, L)
        xp_ref[pl.ds(t0 + halo, L), :] = xr_ref[pl.ds(t0, L), :]
        return carry

    lax.fori_loop(0, n_chunks, copy_body, 0)

    cw = cw_ref[...]
    cb = cb_ref[...]
    bias = bg_ref[0]
    neg_c_softplus = -LRU_C * jax.nn.softplus(-lam_ref[0])

    def conv_chunk(t0):
        win = xp_ref[pl.ds(t0, L + 2 * halo), :]
        acc = cb
        for j in range(CONV_WIDTH):
            off = halo - CONV_PAD_LEFT + j
            acc = acc + cw[j:j + 1, :] * win[off:off + L, :]
        return acc

    def gate_au(c, direction):
        w = wg_ref[0, :, direction * 2 * C:(direction + 1) * 2 * C]
        z = jnp.dot(c.astype(BF16), w, preferred_element_type=F32) + bias[:, direction * 2 * C:(direction + 1) * 2 * C]
        r = jax.nn.sigmoid(z[:, :C])
        i = jax.nn.sigmoid(z[:, C:])
        log_a = r * neg_c_softplus[direction:direction + 1, :]
        a = jnp.exp(log_a)
        u = jnp.sqrt(1.0 - a * a) * (i * c)
        return a, u

    def fwd_body(ci, carry):
        t0 = pl.multiple_of(ci * L, L)
        c = conv_chunk(t0)
        a, u = gate_au(c, 0)
        A, H = _scan_chunk(a, u, reverse=False)
        h = H + A * carry
        hf_ref[pl.ds(t0, L), :] = h
        return h[L - 1:L, :]

    lax.fori_loop(0, n_chunks, fwd_body, jnp.zeros((1, C), F32))

    def bwd_body(k, carry):
        t0 = pl.multiple_of((n_chunks - 1 - k) * L, L)
        c = conv_chunk(t0)
        a, u = gate_au(c, 1)
        A, H = _scan_chunk(a, u, reverse=True)
        h = H + A * carry
        y_ref[pl.ds(t0, L), :] = jax.nn.gelu(xg_ref[pl.ds(t0, L), :]) * (hf_ref[pl.ds(t0, L), :] + h)
        return h[0:1, :]

    lax.fori_loop(0, n_chunks, bwd_body, jnp.zeros((1, C), F32))


def _block_diag_chunks(w, per_chunk):
    nblk, b, _ = w.shape
    w4 = w.reshape(nblk // per_chunk, per_chunk, b, b)
    eye = jnp.eye(per_chunk, dtype=w.dtype)
    m = w4[:, :, :, None, :] * eye[None, :, None, :, None]
    return m.reshape(nblk // per_chunk, per_chunk * b, per_chunk * b)


def _lru(gr, conv_w, conv_b, gate_ws, gate_bs, lams, batch, seq):
    T, two_w = gr.shape
    W = two_w // 2
    C = LRU_CHANNELS
    n_ch = W // C
    per_chunk = C // LRU_BLOCK
    wg = jnp.concatenate([_block_diag_chunks(w, per_chunk) for w in gate_ws], axis=-1).astype(BF16)
    bg = jnp.concatenate([b.reshape(n_ch, 1, C) for b in gate_bs], axis=-1)
    lam = jnp.stack([l.reshape(n_ch, C) for l in lams], axis=1)
    return pl.pallas_call(
        _lru_kernel,
        grid=(batch, n_ch),
        in_specs=[
            pl.BlockSpec((seq, C), lambda b, c: (b, c)),
            pl.BlockSpec((seq, C), lambda b, c: (b, n_ch + c)),
            pl.BlockSpec((CONV_WIDTH, C), lambda b, c: (0, c)),
            pl.BlockSpec((1, C), lambda b, c: (0, c)),
            pl.BlockSpec((1, C, 4 * C), lambda b, c: (c, 0, 0)),
            pl.BlockSpec((1, 1, 4 * C), lambda b, c: (c, 0, 0)),
            pl.BlockSpec((1, 2, C), lambda b, c: (c, 0, 0)),
        ],
        out_specs=pl.BlockSpec((seq, C), lambda b, c: (b, c)),
        out_shape=jax.ShapeDtypeStruct((T, W), F32),
        scratch_shapes=[pltpu.VMEM((seq + 2 * SUBLANES, C), F32), pltpu.VMEM((seq, C), F32)],
        compiler_params=_cparams(("parallel", "parallel")),
        name="lru",
    )(gr, gr, conv_w, conv_b.reshape(1, W), wg, bg, lam)


def _outproj_kernel(ya_ref, yl_ref, x_ref, ga_ref, gl_ref, w_ref, gf_ref, x1_ref, xn_ref):
    y = jnp.concatenate([_rms(ya_ref[...], ga_ref[...]), _rms(yl_ref[...], gl_ref[...])], axis=1).astype(BF16)
    x1 = x_ref[...] + jnp.dot(y, w_ref[...], preferred_element_type=F32)
    x1_ref[...] = x1
    xn_ref[...] = _rms(x1, gf_ref[...])


def _outproj(y_attn, y_lru, xt, g_attn, g_lru, w_out_bf, g_ffn):
    T, D = xt.shape
    wa, wl = y_attn.shape[1], y_lru.shape[1]
    tm = OUTPROJ_TM
    return pl.pallas_call(
        _outproj_kernel,
        grid=(T // tm,),
        in_specs=[
            pl.BlockSpec((tm, wa), lambda i: (i, 0)),
            pl.BlockSpec((tm, wl), lambda i: (i, 0)),
            pl.BlockSpec((tm, D), lambda i: (i, 0)),
            pl.BlockSpec((1, wa), lambda i: (0, 0)),
            pl.BlockSpec((1, wl), lambda i: (0, 0)),
            pl.BlockSpec((wa + wl, D), lambda i: (0, 0)),
            pl.BlockSpec((1, D), lambda i: (0, 0)),
        ],
        out_specs=[pl.BlockSpec((tm, D), lambda i: (i, 0)), pl.BlockSpec((tm, D), lambda i: (i, 0))],
        out_shape=[jax.ShapeDtypeStruct((T, D), F32), jax.ShapeDtypeStruct((T, D), F32)],
        compiler_params=_cparams(("parallel",)),
        name="outproj",
    )(y_attn, y_lru, xt, g_attn, g_lru, w_out_bf, g_ffn)


def _pq_kernel(xn_ref, w_ref, k_ref, s_ref):
    q = jnp.dot(xn_ref[...].astype(BF16), w_ref[...], preferred_element_type=F32).astype(BF16)
    for hh in range(2 * PEER_HEADS):
        half, head = divmod(hh, PEER_HEADS)
        col = (head * 2 + half) * PEER_HALF
        qh = q[:, col:col + PEER_HALF]
        s_ref[hh] = lax.dot_general(k_ref[hh], qh, (((1,), (1,)), ((), ())), preferred_element_type=F32)


def _peer_scores(xn, w_pq_bf, keys_bf):
    T, D = xn.shape
    tm = PQ_TM
    nh = keys_bf.shape[0]
    return pl.pallas_call(
        _pq_kernel,
        grid=(T // tm,),
        in_specs=[
            pl.BlockSpec((tm, D), lambda i: (i, 0)),
            pl.BlockSpec(w_pq_bf.shape, lambda i: (0, 0)),
            pl.BlockSpec(keys_bf.shape, lambda i: (0, 0, 0)),
        ],
        out_specs=pl.BlockSpec((nh, N_KEYS, tm), lambda i: (0, 0, i)),
        out_shape=jax.ShapeDtypeStruct((nh, N_KEYS, T), F32),
        compiler_params=_cparams(("parallel",)),
        name="pq",
    )(xn, w_pq_bf, keys_bf)


def _top16_rows(s, payload=None):
    n, tl = s.shape
    row = lax.broadcasted_iota(jnp.int32, (n, tl), 0)
    vals, idxs, pays = [], [], []
    for _ in range(PEER_TOPK):
        m = jnp.max(s, axis=0, keepdims=True)
        idx = jnp.min(jnp.where(s == m, row, n), axis=0, keepdims=True)
        hit = row == idx
        vals.append(m)
        idxs.append(idx)
        if payload is not None:
            pays.append(jnp.max(jnp.where(hit, payload, -1), axis=0, keepdims=True))
        s = jnp.where(hit, -jnp.inf, s)
    cat = lambda xs: jnp.concatenate(xs, axis=0)
    return cat(vals), cat(idxs), (cat(pays) if payload is not None else None)


def _topk_kernel(s1_ref, s2_ref, g_ref, e_ref):
    v1, i1, _ = _top16_rows(s1_ref[0])
    v2, i2, _ = _top16_rows(s2_ref[0])
    cand = jnp.concatenate([v1[a:a + 1, :] + v2 for a in range(PEER_TOPK)], axis=0)
    cidx = jnp.concatenate([i1[a:a + 1, :] * N_KEYS + i2 for a in range(PEER_TOPK)], axis=0)
    sc, _, eidx = _top16_rows(cand, cidx)
    ex = jnp.exp(sc - sc[0:1, :])
    g_ref[0] = ex / jnp.sum(ex, axis=0, keepdims=True)
    e_ref[0] = eidx


def _peer_topk(scores):
    nh2, nk, T = scores.shape
    nh = nh2 // 2
    tl = TOPK_TL
    return pl.pallas_call(
        _topk_kernel,
        grid=(nh, T // tl),
        in_specs=[
            pl.BlockSpec((1, nk, tl), lambda h, i: (h, 0, i)),
            pl.BlockSpec((1, nk, tl), lambda h, i: (nh + h, 0, i)),
        ],
        out_specs=[
            pl.BlockSpec((1, PEER_TOPK, tl), lambda h, i: (h, 0, i)),
            pl.BlockSpec((1, PEER_TOPK, tl), lambda h, i: (h, 0, i)),
        ],
        out_shape=[
            jax.ShapeDtypeStruct((nh, PEER_TOPK, T), F32),
            jax.ShapeDtypeStruct((nh, PEER_TOPK, T), jnp.int32),
        ],
        compiler_params=_cparams(("parallel", "parallel")),
        name="topk",
    )(scores, scores)


ROW_SUBLANES = 16


def _sublane_fold(parts):
    sub = lax.broadcasted_iota(jnp.int32, (SUBLANES, LANES), 0)
    step = 1
    while len(parts) > 1:
        low = (sub & step) == 0
        parts = [jnp.where(low, a, b) + pltpu.roll(jnp.where(low, b, a), step, 0)
                 for a, b in zip(parts[0::2], parts[1::2])]
        step *= 2
    return parts[0]


def _route(eidx, gates, n_blocks):
    T, P = eidx.shape
    blocks = jnp.arange(n_blocks, dtype=jnp.int32)
    blk = eidx // EXPERT_BLOCK
    cnt = jnp.sum((blk[:, :, None] == blocks[None, None, :]).astype(jnp.int32), axis=1)
    pcnt = (cnt + CHUNK - 1) // CHUNK * CHUNK
    cand = jnp.arange(CHUNK, dtype=jnp.int32)
    active = cand[None, None, :] < (pcnt - cnt)[:, :, None]
    pad_key = jnp.where(active, (((blocks + 1) * EXPERT_BLOCK - 1) * 2 + 1)[None, :, None], jnp.iinfo(jnp.int32).max)
    keys = jnp.concatenate([eidx * 2, pad_key.reshape(T, n_blocks * CHUNK)], axis=1)
    zeros_i = jnp.zeros((T, n_blocks * CHUNK), jnp.int32)
    loc = jnp.concatenate([eidx % EXPERT_BLOCK, zeros_i], axis=1)
    gts = jnp.concatenate([gates, zeros_i.astype(F32)], axis=1)
    _, loc_s, gts_s = lax.sort((keys, loc, gts), dimension=1, num_keys=1)
    ends = jnp.cumsum(pcnt, axis=1) // CHUNK
    cs = jnp.concatenate([jnp.zeros((T, 1), jnp.int32), ends,
                          jnp.zeros((T, CS_STRIDE - n_blocks - 1), jnp.int32)], axis=1)
    return loc_s.reshape(-1), gts_s.reshape(-1), cs.reshape(-1)


def _pdot_kernel(cs_ref, idx_ref, x_ref, u_ref, d_ref, rbuf_ref, *, slots):
    b = pl.program_id(0)
    tm = x_ref.shape[0]
    zero_rows = jnp.zeros((SUBLANES, LANES), F32)

    def token_body(tl, q):
        c0 = cs_ref[tl * CS_STRIDE + b]
        c1 = cs_ref[tl * CS_STRIDE + b + 1]
        x = x_ref[tl]
        base = tl * slots

        def chunk_body(c, q):
            idx_c = idx_ref.at[pl.ds(base + c * CHUNK, CHUNK)]
            for h in range(CHUNK // SUBLANES):
                parts = []
                for k in range(SUBLANES):
                    p = u_ref[idx_c[h * SUBLANES + k]].astype(F32) * x
                    parts.append(p[:SUBLANES] + p[SUBLANES:])
                row0 = pl.multiple_of(q * CHUNK + h * SUBLANES, SUBLANES)
                rbuf_ref[pl.ds(row0, SUBLANES), :] = _sublane_fold(parts)
            return q + 1

        return lax.fori_loop(c0, c1, chunk_body, q)

    n_chunks = lax.fori_loop(0, tm, token_body, 0)

    per_step = SUBLANES * LANES // CHUNK
    n_steps = (n_chunks + per_step - 1) // per_step
    folds_per_chunk = CHUNK // SUBLANES

    def zero_body(f, carry):
        rbuf_ref[pl.ds(pl.multiple_of(f * SUBLANES, SUBLANES), SUBLANES), :] = zero_rows
        return carry

    lax.fori_loop(n_chunks * folds_per_chunk, n_steps * per_step * folds_per_chunk, zero_body, 0)
    d_ref[...] = jnp.zeros(d_ref.shape, F32)

    def reduce_body(s, carry):
        rows = []
        for j in range(SUBLANES):
            r = rbuf_ref[pl.ds(pl.multiple_of((s * SUBLANES + j) * LANES, LANES), LANES), :]
            rows.append(jnp.sum(r.T, axis=0, keepdims=True))
        d_ref[pl.ds(pl.multiple_of(s * SUBLANES, SUBLANES), SUBLANES), :] = jnp.concatenate(rows, axis=0)
        return carry

    lax.fori_loop(0, n_steps, reduce_body, 0)


def _expert_block_spec():
    return pl.BlockSpec((EXPERT_BLOCK, ROW_SUBLANES, LANES), lambda b, i: (b, 0, 0), pipeline_mode=pl.Buffered(1))


def _peer_dots(cs, idx, x3, u_rows, slots):
    T = x3.shape[0]
    nb = u_rows.shape[0] // EXPERT_BLOCK
    tm = PEER_TM
    n_tiles = T // tm
    max_chunks = tm * (slots // CHUNK)
    groups = max_chunks * CHUNK // LANES
    return pl.pallas_call(
        functools.partial(_pdot_kernel, slots=slots),
        grid=(nb, n_tiles),
        in_specs=[
            pl.BlockSpec((tm * CS_STRIDE,), lambda b, i: (i,), memory_space=pltpu.SMEM),
            pl.BlockSpec((tm * slots,), lambda b, i: (i,), memory_space=pltpu.SMEM),
            pl.BlockSpec((tm, ROW_SUBLANES, LANES), lambda b, i: (i, 0, 0)),
            _expert_block_spec(),
        ],
        out_specs=pl.BlockSpec((None, None, groups, LANES), lambda b, i: (b, i, 0, 0)),
        out_shape=jax.ShapeDtypeStruct((nb, n_tiles, groups, LANES), F32),
        scratch_shapes=[pltpu.VMEM((max_chunks * CHUNK, LANES), F32)],
        compiler_params=_cparams(("arbitrary", "arbitrary")),
        name="pdot",
    )(cs, idx, x3, u_rows)


def _act_kernel(d_ref, a_ref):
    a_ref[...] = jax.nn.gelu(d_ref[...])


def _peer_act(dots):
    rows, lanes = dots.shape
    tm = min(4096, rows)
    return pl.pallas_call(
        _act_kernel,
        grid=(rows // tm,),
        in_specs=[pl.BlockSpec((tm, lanes), lambda i: (i, 0))],
        out_specs=pl.BlockSpec((tm, lanes), lambda i: (i, 0)),
        out_shape=jax.ShapeDtypeStruct((rows, lanes), F32),
        compiler_params=_cparams(("parallel",)),
        name="act",
    )(dots)


def _pacc_kernel(cs_ref, idx_ref, gate_ref, act_ref, v_ref, o_ref, *, slots):
    b = pl.program_id(0)
    tm = o_ref.shape[0]

    def token_body(tl, q):
        c0 = cs_ref[tl * CS_STRIDE + b]
        c1 = cs_ref[tl * CS_STRIDE + b + 1]
        base = tl * slots

        def chunk_body(c, carry):
            q, acc = carry
            idx_c = idx_ref.at[pl.ds(base + c * CHUNK, CHUNK)]
            gate_c = gate_ref.at[pl.ds(base + c * CHUNK, CHUNK)]
            act_c = act_ref.at[pl.ds(q * CHUNK, CHUNK)]
            for k in range(CHUNK):
                acc = acc + (gate_c[k] * act_c[k]) * v_ref[idx_c[k]].astype(F32)
            return q + 1, acc

        q, acc = lax.fori_loop(c0, c1, chunk_body, (q, jnp.zeros((ROW_SUBLANES, LANES), F32)))
        o_ref[tl] = acc
        return q

    lax.fori_loop(0, tm, token_body, 0)


def _peer_accumulate(cs, idx, gate, act, v_rows, T, slots):
    nb = v_rows.shape[0] // EXPERT_BLOCK
    tm = PEER_TM
    n_tiles = T // tm
    per_step = act.shape[0] // (nb * n_tiles)
    return pl.pallas_call(
        functools.partial(_pacc_kernel, slots=slots),
        grid=(nb, n_tiles),
        in_specs=[
            pl.BlockSpec((tm * CS_STRIDE,), lambda b, i: (i,), memory_space=pltpu.SMEM),
            pl.BlockSpec((tm * slots,), lambda b, i: (i,), memory_space=pltpu.SMEM),
            pl.BlockSpec((tm * slots,), lambda b, i: (i,), memory_space=pltpu.SMEM),
            pl.BlockSpec((per_step,), lambda b, i: (b * n_tiles + i,), memory_space=pltpu.SMEM),
            _expert_block_spec(),
        ],
        out_specs=pl.BlockSpec((None, tm, ROW_SUBLANES, LANES), lambda b, i: (b, i, 0, 0)),
        out_shape=jax.ShapeDtypeStruct((nb, T, ROW_SUBLANES, LANES), F32),
        compiler_params=_cparams(("arbitrary", "arbitrary")),
        name="pacc",
    )(cs, idx, gate, act, v_rows)


def _final_kernel(x1_ref, p_ref, g_ref, o_ref):
    acc = x1_ref[...]
    for b in range(p_ref.shape[0]):
        acc = acc + p_ref[b]
    o_ref[...] = _rms(acc, g_ref[...])


def _final(x1, peer_parts, g_final):
    T, D = x1.shape
    nb = peer_parts.shape[0]
    tm = 256
    return pl.pallas_call(
        _final_kernel,
        grid=(T // tm,),
        in_specs=[
            pl.BlockSpec((tm, D), lambda i: (i, 0)),
            pl.BlockSpec((nb, tm, D), lambda i: (0, i, 0)),
            pl.BlockSpec((1, D), lambda i: (0, 0)),
        ],
        out_specs=pl.BlockSpec((tm, D), lambda i: (i, 0)),
        out_shape=jax.ShapeDtypeStruct((T, D), F32),
        compiler_params=_cparams(("parallel",)),
        name="final",
    )(x1, peer_parts, g_final)


def _rope_tables(seq):
    half = HEAD_DIM // 2
    inv = ROPE_THETA ** (-jnp.arange(half, dtype=F32) / half)
    ang = jnp.arange(seq).astype(F32)[:, None] * inv[None, :]
    cos, sin = jnp.cos(ang), jnp.sin(ang)
    reps = LANES // HEAD_DIM
    cos_t = jnp.concatenate([cos, cos] * reps, axis=1)
    sin_t = jnp.concatenate([-sin, sin] * reps, axis=1)
    return cos_t, sin_t


def _layer(xt, batch, seq, g_mix, w_in, sink, conv_w, conv_b, fwd_wa, fwd_ba, fwd_wx, fwd_bx, fwd_lam,
           bwd_wa, bwd_ba, bwd_wx, bwd_bx, bwd_lam, g_attn_out, g_lru_out, w_out, g_ffn,
           w_pq, sub_k1, sub_k2, u_emb, v_emb):
    T, D = xt.shape
    cos_t, sin_t = _rope_tables(seq)
    qkv, gr = _inproj(xt, g_mix.reshape(1, D), w_in.astype(BF16), cos_t, sin_t, seq)
    y_attn = _attention(qkv, sink, batch, seq)
    y_lru = _lru(gr, conv_w, conv_b, (fwd_wa, fwd_wx, bwd_wa, bwd_wx), (fwd_ba, fwd_bx, bwd_ba, bwd_bx),
                 (fwd_lam, bwd_lam), batch, seq)
    x1, xn = _outproj(y_attn, y_lru, xt, g_attn_out.reshape(1, -1), g_lru_out.reshape(1, -1),
                      w_out.astype(BF16), g_ffn.reshape(1, D))

    keys = jnp.concatenate([sub_k1, sub_k2], axis=0).astype(BF16)
    scores = _peer_scores(xn, w_pq.astype(BF16), keys)
    gates_t, eidx_t = _peer_topk(scores)
    n_pairs = PEER_HEADS * PEER_TOPK
    gates = gates_t.transpose(2, 0, 1).reshape(T, n_pairs)
    eidx = eidx_t.transpose(2, 0, 1).reshape(T, n_pairs)
    n_exp = u_emb.shape[0]
    nb = n_exp // EXPERT_BLOCK
    slots = n_pairs + nb * CHUNK
    idx, gate, cs = _route(eidx, gates, nb)

    x3 = xn.reshape(T, ROW_SUBLANES, LANES)
    u_rows = u_emb.astype(BF16).reshape(n_exp, ROW_SUBLANES, LANES)
    v_rows = v_emb.astype(BF16).reshape(n_exp, ROW_SUBLANES, LANES)
    dots = _peer_dots(cs, idx, x3, u_rows, slots)
    act = _peer_act(dots.reshape(-1, LANES)).reshape(-1)
    parts = _peer_accumulate(cs, idx, gate, act, v_rows, T, slots)
    return x1, parts.reshape(nb, T, D)


def kernel(x, g_mix, w_in, sink, conv_w, conv_b, fwd_wa, fwd_ba, fwd_wx, fwd_bx, fwd_lam, bwd_wa, bwd_ba, bwd_wx, bwd_bx, bwd_lam, g_attn_out, g_lru_out, w_out, g_ffn, w_pq, sub_k1, sub_k2, u_emb, v_emb, g_final):
    B, S, D = x.shape
    assert g_mix.shape[0] == 1, "single-layer trunk"
    xt = x.reshape(B * S, D)
    x1, parts = _layer(xt, B, S, g_mix[0], w_in[0], sink[0], conv_w[0], conv_b[0],
                       fwd_wa[0], fwd_ba[0], fwd_wx[0], fwd_bx[0], fwd_lam[0],
                       bwd_wa[0], bwd_ba[0], bwd_wx[0], bwd_bx[0], bwd_lam[0],
                       g_attn_out[0], g_lru_out[0], w_out[0], g_ffn[0],
                       w_pq[0], sub_k1[0], sub_k2[0], u_emb[0], v_emb[0])
    return _final(x1, parts, g_final.reshape(1, D)).reshape(B, S, D)
```

```python
import functools

import jax
import jax.numpy as jnp
from jax import lax
from jax.experimental import pallas as pl
from jax.experimental.pallas import tpu as pltpu

F32 = jnp.float32
BF16 = jnp.bfloat16

HEAD_DIM = 64
N_KV_HEADS = 4
Q_PER_KV = 4
N_Q_HEADS = N_KV_HEADS * Q_PER_KV
WINDOW = 128
ATTN_BLOCK = 128
ROPE_THETA = 10000.0
LRU_BLOCK = 64
CONV_WIDTH = 4
CONV_PAD_LEFT = 2
LRU_C = 8.0
PEER_HEADS = 8
PEER_HALF = 128
N_KEYS = 128
PEER_TOPK = 16
EPS = 1e-6
NEG = -1e30

LANES = 128
SUBLANES = 8
VMEM_LIMIT = 56 * 1024 * 1024

INPROJ_TM = 512
INPROJ_TN = 512
LRU_CHANNELS = 256
LRU_CHUNK = 256
OUTPROJ_TM = 256
PQ_TM = 256
TOPK_TL = 512
EXPERT_BLOCK = 8192
PEER_TM = 128
CHUNK = 16
CS_STRIDE = 8
GATE_BITS = 18
GATE_SHIFT = 31 - GATE_BITS


def _cparams(sem):
    return pltpu.CompilerParams(dimension_semantics=sem, vmem_limit_bytes=VMEM_LIMIT)


def _rms(x, g):
    return x * lax.rsqrt(jnp.mean(x * x, axis=-1, keepdims=True) + EPS) * g


def _inproj_kernel(x_ref, g_ref, w_ref, cos_ref, sin_ref, qkv_ref, gr_ref, h_ref, *, n_rope_tiles, kv_cols):
    j = pl.program_id(1)

    @pl.when(j == 0)
    def _():
        h_ref[...] = _rms(x_ref[...], g_ref[...]).astype(BF16)

    acc = jnp.dot(h_ref[...], w_ref[...], preferred_element_type=F32)
    tm, tn = acc.shape

    @pl.when(j < n_rope_tiles)
    def _():
        reps = tn // LANES
        cos = jnp.concatenate([cos_ref[...]] * reps, axis=1)
        sin = jnp.concatenate([sin_ref[...]] * reps, axis=1)
        lane = lax.broadcasted_iota(jnp.int32, (tm, tn), 1)
        first = (lane % HEAD_DIM) < (HEAD_DIM // 2)
        partner = jnp.where(first, pltpu.roll(acc, tn - HEAD_DIM // 2, 1), pltpu.roll(acc, HEAD_DIM // 2, 1))
        roped = acc * cos + partner * sin
        is_rope = jnp.logical_or(j < n_rope_tiles - 1, lane < kv_cols)
        qkv_ref[...] = jnp.where(is_rope, roped, acc).astype(BF16)

    @pl.when(j >= n_rope_tiles)
    def _():
        gr_ref[...] = acc


def _inproj(xt, g_mix, w_in_bf, cos_t, sin_t, seq):
    T, D = xt.shape
    n_cols = w_in_bf.shape[1]
    q_cols = N_Q_HEADS * HEAD_DIM
    kv_cols = N_KV_HEADS * HEAD_DIM
    qkv_cols = q_cols + 2 * kv_cols
    tm, tn = INPROJ_TM, INPROJ_TN
    n_rope_tiles = qkv_cols // tn
    assert qkv_cols % tn == 0 and (q_cols % tn == 0) and seq % tm == 0 and T % tm == 0
    pos_blocks = seq // tm
    kern = functools.partial(_inproj_kernel, n_rope_tiles=n_rope_tiles, kv_cols=kv_cols)
    return pl.pallas_call(
        kern,
        grid=(T // tm, n_cols // tn),
        in_specs=[
            pl.BlockSpec((tm, D), lambda i, j: (i, 0)),
            pl.BlockSpec((1, D), lambda i, j: (0, 0)),
            pl.BlockSpec((D, tn), lambda i, j: (0, j)),
            pl.BlockSpec((tm, LANES), lambda i, j: (i % pos_blocks, 0)),
            pl.BlockSpec((tm, LANES), lambda i, j: (i % pos_blocks, 0)),
        ],
        out_specs=[
            pl.BlockSpec((tm, tn), lambda i, j: (i, jnp.minimum(j, n_rope_tiles - 1))),
            pl.BlockSpec((tm, tn), lambda i, j: (i, jnp.maximum(j - n_rope_tiles, 0))),
        ],
        out_shape=[
            jax.ShapeDtypeStruct((T, qkv_cols), BF16),
            jax.ShapeDtypeStruct((T, n_cols - qkv_cols), F32),
        ],
        scratch_shapes=[pltpu.VMEM((tm, D), BF16)],
        compiler_params=_cparams(("parallel", "arbitrary")),
        name="inproj",
    )(xt, g_mix, w_in_bf, cos_t, sin_t)


def _attn_kernel(sink_ref, q_ref, kp_ref, kc_ref, kn_ref, vp_ref, vc_ref, vn_ref, o_ref, *, seq):
    n = pl.program_id(1)
    q = q_ref[...]
    k = jnp.concatenate([kp_ref[...], kc_ref[...], kn_ref[...]], axis=0)
    v = jnp.concatenate([vp_ref[...], vc_ref[...], vn_ref[...]], axis=0)
    nq, nk = ATTN_BLOCK, 3 * ATTN_BLOCK
    qpos = n * ATTN_BLOCK + lax.broadcasted_iota(jnp.int32, (nq, nk), 0)
    kpos = (n - 1) * ATTN_BLOCK + lax.broadcasted_iota(jnp.int32, (nq, nk), 1)
    valid = (jnp.abs(kpos - qpos) <= WINDOW) & (kpos >= 0) & (kpos < seq)
    scale = HEAD_DIM ** -0.5
    outs = []
    for g in range(N_KV_HEADS):
        kg = k[:, g * HEAD_DIM:(g + 1) * HEAD_DIM]
        vg = v[:, g * HEAD_DIM:(g + 1) * HEAD_DIM]
        for r in range(Q_PER_KV):
            h = g * Q_PER_KV + r
            qh = q[:, h * HEAD_DIM:(h + 1) * HEAD_DIM]
            s = lax.dot_general(qh, kg, (((1,), (1,)), ((), ())), preferred_element_type=F32) * scale
            s = jnp.where(valid, s, NEG)
            sk = sink_ref[h]
            m = jnp.maximum(jnp.max(s, axis=-1, keepdims=True), sk)
            p = jnp.exp(s - m)
            den = jnp.sum(p, axis=-1, keepdims=True) + jnp.exp(sk - m)
            o = jnp.dot(p.astype(BF16), vg, preferred_element_type=F32) / den
            outs.append(o)
    o_ref[...] = jnp.concatenate(outs, axis=1)


def _attention(qkv, sink, batch, seq):
    T = qkv.shape[0]
    nb = seq // ATTN_BLOCK
    q_cols = N_Q_HEADS * HEAD_DIM
    kv_cols = N_KV_HEADS * HEAD_DIM
    k_blk = q_cols // kv_cols
    v_blk = k_blk + 1
    blk = ATTN_BLOCK

    def row(b, n):
        return b * nb + n

    def kv_spec(col_blk, shift):
        return pl.BlockSpec((blk, kv_cols), lambda b, n: (row(b, jnp.clip(n + shift, 0, nb - 1)), col_blk))

    return pl.pallas_call(
        functools.partial(_attn_kernel, seq=seq),
        grid=(batch, nb),
        in_specs=[
            pl.BlockSpec(memory_space=pltpu.SMEM),
            pl.BlockSpec((blk, q_cols), lambda b, n: (row(b, n), 0)),
            kv_spec(k_blk, -1), kv_spec(k_blk, 0), kv_spec(k_blk, 1),
            kv_spec(v_blk, -1), kv_spec(v_blk, 0), kv_spec(v_blk, 1),
        ],
        out_specs=pl.BlockSpec((blk, q_cols), lambda b, n: (row(b, n), 0)),
        out_shape=jax.ShapeDtypeStruct((T, q_cols), F32),
        compiler_params=_cparams(("parallel", "parallel")),
        name="attn",
    )(sink, qkv, qkv, qkv, qkv, qkv, qkv, qkv)


def _scan_chunk(a, u, reverse):
    L, C = a.shape
    row = lax.broadcasted_iota(jnp.int32, (L, C), 0)
    A, H = a, u
    d = 1
    while d < L:
        if d < SUBLANES:
            if reverse:
                keep = row < L - d
                As = jnp.where(keep, pltpu.roll(A, L - d, 0), 1.0)
                Hs = jnp.where(keep, pltpu.roll(H, L - d, 0), 0.0)
            else:
                keep = row >= d
                As = jnp.where(keep, pltpu.roll(A, d, 0), 1.0)
                Hs = jnp.where(keep, pltpu.roll(H, d, 0), 0.0)
        else:
            one = jnp.ones((d, C), F32)
            zero = jnp.zeros((d, C), F32)
            if reverse:
                As = jnp.concatenate([A[d:], one], axis=0)
                Hs = jnp.concatenate([H[d:], zero], axis=0)
            else:
                As = jnp.concatenate([one, A[:L - d]], axis=0)
                Hs = jnp.concatenate([zero, H[:L - d]], axis=0)
        H = A * Hs + H
        A = A * As
        d *= 2
    return A, H


def _lru_kernel(xg_ref, xr_ref, cw_ref, cb_ref, wg_ref, bg_ref, lam_ref, y_ref, xp_ref, hf_ref):
    S, C = xr_ref.shape
    L = LRU_CHUNK
    n_chunks = S // L
    halo = SUBLANES

    xp_ref[0:halo, :] = jnp.zeros((halo, C), F32)
    xp_ref[S + halo:S + 2 * halo, :] = jnp.zeros((halo, C), F32)

    def copy_body(ci, carry):
        t0 = pl.multiple_of(ci * L, L)
        xp_ref[pl.ds(t0 + halo, L), :] = xr_ref[pl.ds(t0, L), :]
        return carry

    lax.fori_loop(0, n_chunks, copy_body, 0)

    cw = cw_ref[...]
    cb = cb_ref[...]
    bias = bg_ref[0]
    neg_c_softplus = -LRU_C * jax.nn.softplus(-lam_ref[0])

    def conv_chunk(t0):
        win = xp_ref[pl.ds(t0, L + 2 * halo), :]
        acc = cb
        for j in range(CONV_WIDTH):
            off = halo - CONV_PAD_LEFT + j
            acc = acc + cw[j:j + 1, :] * win[off:off + L, :]
        return acc

    def gate_au(c, direction):
        w = wg_ref[0, :, direction * 2 * C:(direction + 1) * 2 * C]
        z = jnp.dot(c.astype(BF16), w, preferred_element_type=F32) + bias[:, direction * 2 * C:(direction + 1) * 2 * C]
        r = jax.nn.sigmoid(z[:, :C])
        i = jax.nn.sigmoid(z[:, C:])
        log_a = r * neg_c_softplus[direction:direction + 1, :]
        a = jnp.exp(log_a)
        u = jnp.sqrt(1.0 - a * a) * (i * c)
        return a, u

    def fwd_body(ci, carry):
        t0 = pl.multiple_of(ci * L, L)
        c = conv_chunk(t0)
        a, u = gate_au(c, 0)
        A, H = _scan_chunk(a, u, reverse=False)
        h = H + A * carry
        hf_ref[pl.ds(t0, L), :] = h
        return h[L - 1:L, :]

    lax.fori_loop(0, n_chunks, fwd_body, jnp.zeros((1, C), F32))

    def bwd_body(k, carry):
        t0 = pl.multiple_of((n_chunks - 1 - k) * L, L)
        c = conv_chunk(t0)
        a, u = gate_au(c, 1)
        A, H = _scan_chunk(a, u, reverse=True)
        h = H + A * carry
        y_ref[pl.ds(t0, L), :] = jax.nn.gelu(xg_ref[pl.ds(t0, L), :]) * (hf_ref[pl.ds(t0, L), :] + h)
        return h[0:1, :]

    lax.fori_loop(0, n_chunks, bwd_body, jnp.zeros((1, C), F32))


def _block_diag_chunks(w, per_chunk):
    nblk, b, _ = w.shape
    w4 = w.reshape(nblk // per_chunk, per_chunk, b, b)
    eye = jnp.eye(per_chunk, dtype=w.dtype)
    m = w4[:, :, :, None, :] * eye[None, :, None, :, None]
    return m.reshape(nblk // per_chunk, per_chunk * b, per_chunk * b)


def _lru(gr, conv_w, conv_b, gate_ws, gate_bs, lams, batch, seq):
    T, two_w = gr.shape
    W = two_w // 2
    C = LRU_CHANNELS
    n_ch = W // C
    per_chunk = C // LRU_BLOCK
    wg = jnp.concatenate([_block_diag_chunks(w, per_chunk) for w in gate_ws], axis=-1).astype(BF16)
    bg = jnp.concatenate([b.reshape(n_ch, 1, C) for b in gate_bs], axis=-1)
    lam = jnp.stack([l.reshape(n_ch, C) for l in lams], axis=1)
    return pl.pallas_call(
        _lru_kernel,
        grid=(batch, n_ch),
        in_specs=[
            pl.BlockSpec((seq, C), lambda b, c: (b, c)),
            pl.BlockSpec((seq, C), lambda b, c: (b, n_ch + c)),
            pl.BlockSpec((CONV_WIDTH, C), lambda b, c: (0, c)),
            pl.BlockSpec((1, C), lambda b, c: (0, c)),
            pl.BlockSpec((1, C, 4 * C), lambda b, c: (c, 0, 0)),
            pl.BlockSpec((1, 1, 4 * C), lambda b, c: (c, 0, 0)),
            pl.BlockSpec((1, 2, C), lambda b, c: (c, 0, 0)),
        ],
        out_specs=pl.BlockSpec((seq, C), lambda b, c: (b, c)),
        out_shape=jax.ShapeDtypeStruct((T, W), F32),
        scratch_shapes=[pltpu.VMEM((seq + 2 * SUBLANES, C), F32), pltpu.VMEM((seq, C), F32)],
        compiler_params=_cparams(("parallel", "parallel")),
        name="lru",
    )(gr, gr, conv_w, conv_b.reshape(1, W), wg, bg, lam)


def _outproj_kernel(ya_ref, yl_ref, x_ref, ga_ref, gl_ref, w_ref, gf_ref, x1_ref, xn_ref):
    y = jnp.concatenate([_rms(ya_ref[...], ga_ref[...]), _rms(yl_ref[...], gl_ref[...])], axis=1).astype(BF16)
    x1 = x_ref[...] + jnp.dot(y, w_ref[...], preferred_element_type=F32)
    x1_ref[...] = x1
    xn_ref[...] = _rms(x1, gf_ref[...])


def _outproj(y_attn, y_lru, xt, g_attn, g_lru, w_out_bf, g_ffn):
    T, D = xt.shape
    wa, wl = y_attn.shape[1], y_lru.shape[1]
    tm = OUTPROJ_TM
    return pl.pallas_call(
        _outproj_kernel,
        grid=(T // tm,),
        in_specs=[
            pl.BlockSpec((tm, wa), lambda i: (i, 0)),
            pl.BlockSpec((tm, wl), lambda i: (i, 0)),
            pl.BlockSpec((tm, D), lambda i: (i, 0)),
            pl.BlockSpec((1, wa), lambda i: (0, 0)),
            pl.BlockSpec((1, wl), lambda i: (0, 0)),
            pl.BlockSpec((wa + wl, D), lambda i: (0, 0)),
            pl.BlockSpec((1, D), lambda i: (0, 0)),
        ],
        out_specs=[pl.BlockSpec((tm, D), lambda i: (i, 0)), pl.BlockSpec((tm, D), lambda i: (i, 0))],
        out_shape=[jax.ShapeDtypeStruct((T, D), F32), jax.ShapeDtypeStruct((T, D), F32)],
        compiler_params=_cparams(("parallel",)),
        name="outproj",
    )(y_attn, y_lru, xt, g_attn, g_lru, w_out_bf, g_ffn)


def _pq_kernel(xn_ref, w_ref, k_ref, s_ref):
    q = jnp.dot(xn_ref[...].astype(BF16), w_ref[...], preferred_element_type=F32).astype(BF16)
    for hh in range(2 * PEER_HEADS):
        half, head = divmod(hh, PEER_HEADS)
        col = (head * 2 + half) * PEER_HALF
        qh = q[:, col:col + PEER_HALF]
        s_ref[hh] = lax.dot_general(k_ref[hh], qh, (((1,), (1,)), ((), ())), preferred_element_type=F32)


def _peer_scores(xn, w_pq_bf, keys_bf):
    T, D = xn.shape
    tm = PQ_TM
    nh = keys_bf.shape[0]
    return pl.pallas_call(
        _pq_kernel,
        grid=(T // tm,),
        in_specs=[
            pl.BlockSpec((tm, D), lambda i: (i, 0)),
            pl.BlockSpec(w_pq_bf.shape, lambda i: (0, 0)),
            pl.BlockSpec(keys_bf.shape, lambda i: (0, 0, 0)),
        ],
        out_specs=pl.BlockSpec((nh, N_KEYS, tm), lambda i: (0, 0, i)),
        out_shape=jax.ShapeDtypeStruct((nh, N_KEYS, T), F32),
        compiler_params=_cparams(("parallel",)),
        name="pq",
    )(xn, w_pq_bf, keys_bf)


def _top16_rows(s, payload=None):
    n, tl = s.shape
    row = lax.broadcasted_iota(jnp.int32, (n, tl), 0)
    vals, idxs, pays = [], [], []
    for _ in range(PEER_TOPK):
        m = jnp.max(s, axis=0, keepdims=True)
        idx = jnp.min(jnp.where(s == m, row, n), axis=0, keepdims=True)
        hit = row == idx
        vals.append(m)
        idxs.append(idx)
        if payload is not None:
            pays.append(jnp.max(jnp.where(hit, payload, -1), axis=0, keepdims=True))
        s = jnp.where(hit, -jnp.inf, s)
    cat = lambda xs: jnp.concatenate(xs, axis=0)
    return cat(vals), cat(idxs), (cat(pays) if payload is not None else None)


def _topk_kernel(s1_ref, s2_ref, g_ref, e_ref):
    v1, i1, _ = _top16_rows(s1_ref[0])
    v2, i2, _ = _top16_rows(s2_ref[0])
    K = PEER_TOPK
    tl = v1.shape[1]
    sub = lax.broadcasted_iota(jnp.int32, (SUBLANES, tl), 0)
    cand, cidx = [], []
    for a in range(K // 2):
        n_b = K // (a + 1)
        for b0 in range(0, n_b, SUBLANES):
            keep = sub < (n_b - b0)
            cand.append(jnp.where(keep, v1[a:a + 1, :] + v2[b0:b0 + SUBLANES, :], -jnp.inf))
            cidx.append(i1[a:a + 1, :] * N_KEYS + i2[b0:b0 + SUBLANES, :])
    cand.append(v1[K // 2:, :] + v2[0:1, :])
    cidx.append(i1[K // 2:, :] * N_KEYS + i2[0:1, :])
    sc, _, eidx = _top16_rows(jnp.concatenate(cand, axis=0), jnp.concatenate(cidx, axis=0))
    ex = jnp.exp(sc - sc[0:1, :])
    g_ref[0] = ex / jnp.sum(ex, axis=0, keepdims=True)
    e_ref[0] = eidx


def _peer_topk(scores):
    nh2, nk, T = scores.shape
    nh = nh2 // 2
    tl = TOPK_TL
    return pl.pallas_call(
        _topk_kernel,
        grid=(nh, T // tl),
        in_specs=[
            pl.BlockSpec((1, nk, tl), lambda h, i: (h, 0, i)),
            pl.BlockSpec((1, nk, tl), lambda h, i: (nh + h, 0, i)),
        ],
        out_specs=[
            pl.BlockSpec((1, PEER_TOPK, tl), lambda h, i: (h, 0, i)),
            pl.BlockSpec((1, PEER_TOPK, tl), lambda h, i: (h, 0, i)),
        ],
        out_shape=[
            jax.ShapeDtypeStruct((nh, PEER_TOPK, T), F32),
            jax.ShapeDtypeStruct((nh, PEER_TOPK, T), jnp.int32),
        ],
        compiler_params=_cparams(("parallel", "parallel")),
        name="topk",
    )(scores, scores)


ROW_SUBLANES = 16


def _sublane_fold(parts):
    sub = lax.broadcasted_iota(jnp.int32, (SUBLANES, LANES), 0)
    step = 1
    while len(parts) > 1:
        low = (sub & step) == 0
        parts = [jnp.where(low, a, b) + pltpu.roll(jnp.where(low, b, a), step, 0)
                 for a, b in zip(parts[0::2], parts[1::2])]
        step *= 2
    return parts[0]


def _route(eidx, gates, n_blocks):
    T, P = eidx.shape
    assert EXPERT_BLOCK <= 1 << GATE_SHIFT
    u32 = jnp.uint32
    blocks = jnp.arange(n_blocks, dtype=jnp.int32)
    blk = eidx // EXPERT_BLOCK
    cnt = jnp.sum((blk[:, :, None] == blocks[None, None, :]).astype(jnp.int32), axis=1)
    pcnt = (cnt + CHUNK - 1) // CHUNK * CHUNK
    cand = jnp.arange(CHUNK, dtype=jnp.int32)
    active = cand[None, None, :] < (pcnt - cnt)[:, :, None]
    gate_bits = (lax.bitcast_convert_type(gates, u32) + u32(1 << (GATE_SHIFT - 1))) >> GATE_SHIFT
    pad_mark = u32((1 << GATE_BITS) - 1)
    word = (eidx.astype(u32) << GATE_BITS) | gate_bits
    last_row = ((blocks + 1) * EXPERT_BLOCK - 1).astype(u32)
    pad_word = jnp.where(active, ((last_row << GATE_BITS) | pad_mark)[None, :, None], u32(0xFFFFFFFF))
    srt = lax.sort(jnp.concatenate([word, pad_word.reshape(T, n_blocks * CHUNK)], axis=1), dimension=1)
    is_pad = (srt & pad_mark) == pad_mark
    loc = jnp.where(is_pad, u32(0), (srt >> GATE_BITS) & u32(EXPERT_BLOCK - 1))
    packed = jnp.where(is_pad, u32(0), (srt & pad_mark) << GATE_SHIFT) | loc
    ends = jnp.cumsum(pcnt, axis=1) // CHUNK
    cs = jnp.concatenate([jnp.zeros((T, 1), jnp.int32), ends,
                          jnp.zeros((T, CS_STRIDE - n_blocks - 1), jnp.int32)], axis=1)
    return (loc.astype(jnp.int32).reshape(-1), lax.bitcast_convert_type(packed, jnp.int32).reshape(-1),
            cs.reshape(-1))


def _pdot_kernel(cs_ref, idx_ref, x_ref, u_ref, d_ref, rbuf_ref, *, slots):
    b = pl.program_id(0)
    tm = x_ref.shape[0]
    zero_rows = jnp.zeros((SUBLANES, LANES), F32)

    def token_body(tl, q):
        c0 = cs_ref[tl * CS_STRIDE + b]
        c1 = cs_ref[tl * CS_STRIDE + b + 1]
        x = x_ref[tl]
        base = tl * slots

        def chunk(c, q):
            idx_c = idx_ref.at[pl.ds(base + c * CHUNK, CHUNK)]
            for h in range(CHUNK // SUBLANES):
                parts = []
                for k in range(SUBLANES):
                    p = u_ref[idx_c[h * SUBLANES + k]].astype(F32) * x
                    parts.append(p[:SUBLANES] + p[SUBLANES:])
                row0 = pl.multiple_of(q * CHUNK + h * SUBLANES, SUBLANES)
                rbuf_ref[pl.ds(row0, SUBLANES), :] = _sublane_fold(parts)
            return q + 1

        q = lax.fori_loop(0, (c1 - c0) // 2, lambda i, q: chunk(c0 + 2 * i + 1, chunk(c0 + 2 * i, q)), q)
        return lax.cond((c1 - c0) % 2 == 1, lambda q: chunk(c1 - 1, q), lambda q: q, q)

    n_chunks = lax.fori_loop(0, tm, token_body, 0)

    per_step = SUBLANES * LANES // CHUNK
    n_steps = (n_chunks + per_step - 1) // per_step
    folds_per_chunk = CHUNK // SUBLANES

    def zero_body(f, carry):
        rbuf_ref[pl.ds(pl.multiple_of(f * SUBLANES, SUBLANES), SUBLANES), :] = zero_rows
        return carry

    lax.fori_loop(n_chunks * folds_per_chunk, n_steps * per_step * folds_per_chunk, zero_body, 0)
    d_ref[...] = jnp.zeros(d_ref.shape, F32)

    def reduce_body(s, carry):
        rows = []
        for j in range(SUBLANES):
            r = rbuf_ref[pl.ds(pl.multiple_of((s * SUBLANES + j) * LANES, LANES), LANES), :]
            rows.append(jnp.sum(r.T, axis=0, keepdims=True))
        d_ref[pl.ds(pl.multiple_of(s * SUBLANES, SUBLANES), SUBLANES), :] = jnp.concatenate(rows, axis=0)
        return carry

    lax.fori_loop(0, n_steps, reduce_body, 0)


def _expert_block_spec():
    return pl.BlockSpec((EXPERT_BLOCK, ROW_SUBLANES, LANES), lambda b, i: (b, 0, 0), pipeline_mode=pl.Buffered(1))


def _peer_dots(cs, idx, x3, u_rows, slots):
    T = x3.shape[0]
    nb = u_rows.shape[0] // EXPERT_BLOCK
    tm = PEER_TM
    n_tiles = T // tm
    max_chunks = tm * (slots // CHUNK)
    groups = max_chunks * CHUNK // LANES
    return pl.pallas_call(
        functools.partial(_pdot_kernel, slots=slots),
        grid=(nb, n_tiles),
        in_specs=[
            pl.BlockSpec((tm * CS_STRIDE,), lambda b, i: (i,), memory_space=pltpu.SMEM),
            pl.BlockSpec((tm * slots,), lambda b, i: (i,), memory_space=pltpu.SMEM),
            pl.BlockSpec((tm, ROW_SUBLANES, LANES), lambda b, i: (i, 0, 0)),
            _expert_block_spec(),
        ],
        out_specs=pl.BlockSpec((None, None, groups, LANES), lambda b, i: (b, i, 0, 0)),
        out_shape=jax.ShapeDtypeStruct((nb, n_tiles, groups, LANES), F32),
        scratch_shapes=[pltpu.VMEM((max_chunks * CHUNK, LANES), F32)],
        compiler_params=_cparams(("arbitrary", "arbitrary")),
        name="pdot",
    )(cs, idx, x3, u_rows)


def _act_kernel(d_ref, a_ref):
    a_ref[...] = jax.nn.gelu(d_ref[...])


def _peer_act(dots):
    rows, lanes = dots.shape
    tm = min(4096, rows)
    return pl.pallas_call(
        _act_kernel,
        grid=(rows // tm,),
        in_specs=[pl.BlockSpec((tm, lanes), lambda i: (i, 0))],
        out_specs=pl.BlockSpec((tm, lanes), lambda i: (i, 0)),
        out_shape=jax.ShapeDtypeStruct((rows, lanes), F32),
        compiler_params=_cparams(("parallel",)),
        name="act",
    )(dots)


def _pacc_kernel(cs_ref, pk_ref, act_ref, v_ref, o_ref, *, slots):
    b = pl.program_id(0)
    tm = o_ref.shape[0]
    row_mask = EXPERT_BLOCK - 1

    def token_body(tl, q):
        c0 = cs_ref[tl * CS_STRIDE + b]
        c1 = cs_ref[tl * CS_STRIDE + b + 1]
        base = tl * slots

        def chunk(c, carry):
            q, accs = carry
            accs = list(accs)
            pk_c = pk_ref.at[pl.ds(base + c * CHUNK, CHUNK)]
            act_c = act_ref.at[pl.ds(q * CHUNK, CHUNK)]
            for k in range(CHUNK):
                word = pk_c[k]
                gate = lax.bitcast_convert_type(word & ~row_mask, F32)
                accs[k % 2] = accs[k % 2] + (gate * act_c[k]) * v_ref[word & row_mask].astype(F32)
            return q + 1, tuple(accs)

        zero = jnp.zeros((ROW_SUBLANES, LANES), F32)
        carry = lax.fori_loop(0, (c1 - c0) // 2, lambda i, cr: chunk(c0 + 2 * i + 1, chunk(c0 + 2 * i, cr)),
                              (q, (zero, zero)))
        q, accs = lax.cond((c1 - c0) % 2 == 1, lambda cr: chunk(c1 - 1, cr), lambda cr: cr, carry)
        o_ref[tl] = accs[0] + accs[1]
        return q

    lax.fori_loop(0, tm, token_body, 0)


def _peer_accumulate(cs, packed, act, v_rows, T, slots):
    nb = v_rows.shape[0] // EXPERT_BLOCK
    tm = PEER_TM
    n_tiles = T // tm
    per_step = act.shape[0] // (nb * n_tiles)
    return pl.pallas_call(
        functools.partial(_pacc_kernel, slots=slots),
        grid=(nb, n_tiles),
        in_specs=[
            pl.BlockSpec((tm * CS_STRIDE,), lambda b, i: (i,), memory_space=pltpu.SMEM),
            pl.BlockSpec((tm * slots,), lambda b, i: (i,), memory_space=pltpu.SMEM),
            pl.BlockSpec((per_step,), lambda b, i: (b * n_tiles + i,), memory_space=pltpu.SMEM),
            _expert_block_spec(),
        ],
        out_specs=pl.BlockSpec((None, tm, ROW_SUBLANES, LANES), lambda b, i: (b, i, 0, 0)),
        out_shape=jax.ShapeDtypeStruct((nb, T, ROW_SUBLANES, LANES), F32),
        compiler_params=_cparams(("arbitrary", "arbitrary")),
        name="pacc",
    )(cs, packed, act, v_rows)


def _final_kernel(x1_ref, p_ref, g_ref, o_ref):
    acc = x1_ref[...]
    for b in range(p_ref.shape[0]):
        acc = acc + p_ref[b]
    o_ref[...] = _rms(acc, g_ref[...])


def _final(x1, peer_parts, g_final):
    T, D = x1.shape
    nb = peer_parts.shape[0]
    tm = 256
    return pl.pallas_call(
        _final_kernel,
        grid=(T // tm,),
        in_specs=[
            pl.BlockSpec((tm, D), lambda i: (i, 0)),
            pl.BlockSpec((nb, tm, D), lambda i: (0, i, 0)),
            pl.BlockSpec((1, D), lambda i: (0, 0)),
        ],
        out_specs=pl.BlockSpec((tm, D), lambda i: (i, 0)),
        out_shape=jax.ShapeDtypeStruct((T, D), F32),
        compiler_params=_cparams(("parallel",)),
        name="final",
    )(x1, peer_parts, g_final)


def _rope_tables(seq):
    half = HEAD_DIM // 2
    inv = ROPE_THETA ** (-jnp.arange(half, dtype=F32) / half)
    ang = jnp.arange(seq).astype(F32)[:, None] * inv[None, :]
    cos, sin = jnp.cos(ang), jnp.sin(ang)
    reps = LANES // HEAD_DIM
    cos_t = jnp.concatenate([cos, cos] * reps, axis=1)
    sin_t = jnp.concatenate([-sin, sin] * reps, axis=1)
    return cos_t, sin_t


def _layer(xt, batch, seq, g_mix, w_in, sink, conv_w, conv_b, fwd_wa, fwd_ba, fwd_wx, fwd_bx, fwd_lam,
           bwd_wa, bwd_ba, bwd_wx, bwd_bx, bwd_lam, g_attn_out, g_lru_out, w_out, g_ffn,
           w_pq, sub_k1, sub_k2, u_emb, v_emb):
    T, D = xt.shape
    cos_t, sin_t = _rope_tables(seq)
    qkv, gr = _inproj(xt, g_mix.reshape(1, D), w_in.astype(BF16), cos_t, sin_t, seq)
    y_attn = _attention(qkv, sink, batch, seq)
    y_lru = _lru(gr, conv_w, conv_b, (fwd_wa, fwd_wx, bwd_wa, bwd_wx), (fwd_ba, fwd_bx, bwd_ba, bwd_bx),
                 (fwd_lam, bwd_lam), batch, seq)
    x1, xn = _outproj(y_attn, y_lru, xt, g_attn_out.reshape(1, -1), g_lru_out.reshape(1, -1),
                      w_out.astype(BF16), g_ffn.reshape(1, D))

    keys = jnp.concatenate([sub_k1, sub_k2], axis=0).astype(BF16)
    scores = _peer_scores(xn, w_pq.astype(BF16), keys)
    gates_t, eidx_t = _peer_topk(scores)
    n_pairs = PEER_HEADS * PEER_TOPK
    gates = gates_t.transpose(2, 0, 1).reshape(T, n_pairs)
    eidx = eidx_t.transpose(2, 0, 1).reshape(T, n_pairs)
    n_exp = u_emb.shape[0]
    nb = n_exp // EXPERT_BLOCK
    slots = n_pairs + nb * CHUNK
    idx, packed, cs = _route(eidx, gates, nb)

    x3 = xn.reshape(T, ROW_SUBLANES, LANES)
    u_rows = u_emb.astype(BF16).reshape(n_exp, ROW_SUBLANES, LANES)
    v_rows = v_emb.astype(BF16).reshape(n_exp, ROW_SUBLANES, LANES)
    dots = _peer_dots(cs, idx, x3, u_rows, slots)
    act = _peer_act(dots.reshape(-1, LANES)).reshape(-1)
    parts = _peer_accumulate(cs, packed, act, v_rows, T, slots)
    return x1, parts.reshape(nb, T, D)


def kernel(x, g_mix, w_in, sink, conv_w, conv_b, fwd_wa, fwd_ba, fwd_wx, fwd_bx, fwd_lam, bwd_wa, bwd_ba, bwd_wx, bwd_bx, bwd_lam, g_attn_out, g_lru_out, w_out, g_ffn, w_pq, sub_k1, sub_k2, u_emb, v_emb, g_final):
    B, S, D = x.shape
    assert g_mix.shape[0] == 1, "single-layer trunk"
    xt = x.reshape(B * S, D)
    x1, parts = _layer(xt, B, S, g_mix[0], w_in[0], sink[0], conv_w[0], conv_b[0],
                       fwd_wa[0], fwd_ba[0], fwd_wx[0], fwd_bx[0], fwd_lam[0],
                       bwd_wa[0], bwd_ba[0], bwd_wx[0], bwd_bx[0], bwd_lam[0],
                       g_attn_out[0], g_lru_out[0], w_out[0], g_ffn[0],
                       w_pq[0], sub_k1[0], sub_k2[0], u_emb[0], v_emb[0])
    return _final(x1, parts, g_final.reshape(1, D)).reshape(B, S, D)
```

```python
import functools

import jax
import jax.numpy as jnp
from jax import lax
from jax.experimental import pallas as pl
from jax.experimental.pallas import tpu as pltpu

F32 = jnp.float32
BF16 = jnp.bfloat16

HEAD_DIM = 64
N_KV_HEADS = 4
Q_PER_KV = 4
N_Q_HEADS = N_KV_HEADS * Q_PER_KV
WINDOW = 128
ATTN_BLOCK = 128
ROPE_THETA = 10000.0
LRU_BLOCK = 64
CONV_WIDTH = 4
CONV_PAD_LEFT = 2
LRU_C = 8.0
PEER_HEADS = 8
PEER_HALF = 128
N_KEYS = 128
PEER_TOPK = 16
EPS = 1e-6
NEG = -1e30

LANES = 128
SUBLANES = 8
VMEM_LIMIT = 56 * 1024 * 1024

INPROJ_TM = 512
INPROJ_TN = 512
LRU_CHANNELS = 256
LRU_CHUNK = 256
OUTPROJ_TM = 256
PQ_TM = 256
TOPK_TL = 512
EXPERT_BLOCK = 8192
PEER_TM = 64
CHUNK = 16
CS_STRIDE = 8
GATE_BITS = 18
GATE_SHIFT = 31 - GATE_BITS


def _cparams(sem):
    return pltpu.CompilerParams(dimension_semantics=sem, vmem_limit_bytes=VMEM_LIMIT)


def _rms(x, g):
    return x * lax.rsqrt(jnp.mean(x * x, axis=-1, keepdims=True) + EPS) * g


def _inproj_kernel(x_ref, g_ref, w_ref, cos_ref, sin_ref, qkv_ref, gr_ref, h_ref, *, n_rope_tiles, kv_cols):
    j = pl.program_id(1)

    @pl.when(j == 0)
    def _():
        h_ref[...] = _rms(x_ref[...], g_ref[...]).astype(BF16)

    acc = jnp.dot(h_ref[...], w_ref[...], preferred_element_type=F32)
    tm, tn = acc.shape

    @pl.when(j < n_rope_tiles)
    def _():
        reps = tn // LANES
        cos = jnp.concatenate([cos_ref[...]] * reps, axis=1)
        sin = jnp.concatenate([sin_ref[...]] * reps, axis=1)
        lane = lax.broadcasted_iota(jnp.int32, (tm, tn), 1)
        first = (lane % HEAD_DIM) < (HEAD_DIM // 2)
        partner = jnp.where(first, pltpu.roll(acc, tn - HEAD_DIM // 2, 1), pltpu.roll(acc, HEAD_DIM // 2, 1))
        roped = acc * cos + partner * sin
        is_rope = jnp.logical_or(j < n_rope_tiles - 1, lane < kv_cols)
        qkv_ref[...] = jnp.where(is_rope, roped, acc).astype(BF16)

    @pl.when(j >= n_rope_tiles)
    def _():
        gr_ref[...] = acc


def _inproj(xt, g_mix, w_in_bf, cos_t, sin_t, seq):
    T, D = xt.shape
    n_cols = w_in_bf.shape[1]
    q_cols = N_Q_HEADS * HEAD_DIM
    kv_cols = N_KV_HEADS * HEAD_DIM
    qkv_cols = q_cols + 2 * kv_cols
    tm, tn = INPROJ_TM, INPROJ_TN
    n_rope_tiles = qkv_cols // tn
    assert qkv_cols % tn == 0 and (q_cols % tn == 0) and seq % tm == 0 and T % tm == 0
    pos_blocks = seq // tm
    kern = functools.partial(_inproj_kernel, n_rope_tiles=n_rope_tiles, kv_cols=kv_cols)
    return pl.pallas_call(
        kern,
        grid=(T // tm, n_cols // tn),
        in_specs=[
            pl.BlockSpec((tm, D), lambda i, j: (i, 0)),
            pl.BlockSpec((1, D), lambda i, j: (0, 0)),
            pl.BlockSpec((D, tn), lambda i, j: (0, j)),
            pl.BlockSpec((tm, LANES), lambda i, j: (i % pos_blocks, 0)),
            pl.BlockSpec((tm, LANES), lambda i, j: (i % pos_blocks, 0)),
        ],
        out_specs=[
            pl.BlockSpec((tm, tn), lambda i, j: (i, jnp.minimum(j, n_rope_tiles - 1))),
            pl.BlockSpec((tm, tn), lambda i, j: (i, jnp.maximum(j - n_rope_tiles, 0))),
        ],
        out_shape=[
            jax.ShapeDtypeStruct((T, qkv_cols), BF16),
            jax.ShapeDtypeStruct((T, n_cols - qkv_cols), F32),
        ],
        scratch_shapes=[pltpu.VMEM((tm, D), BF16)],
        compiler_params=_cparams(("parallel", "arbitrary")),
        name="inproj",
    )(xt, g_mix, w_in_bf, cos_t, sin_t)


def _attn_kernel(sink_ref, q_ref, kp_ref, kc_ref, kn_ref, vp_ref, vc_ref, vn_ref, o_ref, *, seq):
    n = pl.program_id(1)
    q = q_ref[...]
    k = jnp.concatenate([kp_ref[...], kc_ref[...], kn_ref[...]], axis=0)
    v = jnp.concatenate([vp_ref[...], vc_ref[...], vn_ref[...]], axis=0)
    nq, nk = ATTN_BLOCK, 3 * ATTN_BLOCK
    qpos = n * ATTN_BLOCK + lax.broadcasted_iota(jnp.int32, (nq, nk), 0)
    kpos = (n - 1) * ATTN_BLOCK + lax.broadcasted_iota(jnp.int32, (nq, nk), 1)
    valid = (jnp.abs(kpos - qpos) <= WINDOW) & (kpos >= 0) & (kpos < seq)
    scale = HEAD_DIM ** -0.5
    outs = []
    for g in range(N_KV_HEADS):
        kg = k[:, g * HEAD_DIM:(g + 1) * HEAD_DIM]
        vg = v[:, g * HEAD_DIM:(g + 1) * HEAD_DIM]
        for r in range(Q_PER_KV):
            h = g * Q_PER_KV + r
            qh = q[:, h * HEAD_DIM:(h + 1) * HEAD_DIM]
            s = lax.dot_general(qh, kg, (((1,), (1,)), ((), ())), preferred_element_type=F32) * scale
            s = jnp.where(valid, s, NEG)
            sk = sink_ref[h]
            m = jnp.maximum(jnp.max(s, axis=-1, keepdims=True), sk)
            p = jnp.exp(s - m)
            den = jnp.sum(p, axis=-1, keepdims=True) + jnp.exp(sk - m)
            o = jnp.dot(p.astype(BF16), vg, preferred_element_type=F32) / den
            outs.append(o)
    o_ref[...] = jnp.concatenate(outs, axis=1)


def _attention(qkv, sink, batch, seq):
    T = qkv.shape[0]
    nb = seq // ATTN_BLOCK
    q_cols = N_Q_HEADS * HEAD_DIM
    kv_cols = N_KV_HEADS * HEAD_DIM
    k_blk = q_cols // kv_cols
    v_blk = k_blk + 1
    blk = ATTN_BLOCK

    def row(b, n):
        return b * nb + n

    def kv_spec(col_blk, shift):
        return pl.BlockSpec((blk, kv_cols), lambda b, n: (row(b, jnp.clip(n + shift, 0, nb - 1)), col_blk))

    return pl.pallas_call(
        functools.partial(_attn_kernel, seq=seq),
        grid=(batch, nb),
        in_specs=[
            pl.BlockSpec(memory_space=pltpu.SMEM),
            pl.BlockSpec((blk, q_cols), lambda b, n: (row(b, n), 0)),
            kv_spec(k_blk, -1), kv_spec(k_blk, 0), kv_spec(k_blk, 1),
            kv_spec(v_blk, -1), kv_spec(v_blk, 0), kv_spec(v_blk, 1),
        ],
        out_specs=pl.BlockSpec((blk, q_cols), lambda b, n: (row(b, n), 0)),
        out_shape=jax.ShapeDtypeStruct((T, q_cols), F32),
        compiler_params=_cparams(("parallel", "parallel")),
        name="attn",
    )(sink, qkv, qkv, qkv, qkv, qkv, qkv, qkv)


def _scan_chunk(a, u, reverse):
    L, C = a.shape
    row = lax.broadcasted_iota(jnp.int32, (L, C), 0)
    A, H = a, u
    d = 1
    while d < L:
        if d < SUBLANES:
            if reverse:
                keep = row < L - d
                As = jnp.where(keep, pltpu.roll(A, L - d, 0), 1.0)
                Hs = jnp.where(keep, pltpu.roll(H, L - d, 0), 0.0)
            else:
                keep = row >= d
                As = jnp.where(keep, pltpu.roll(A, d, 0), 1.0)
                Hs = jnp.where(keep, pltpu.roll(H, d, 0), 0.0)
        else:
            one = jnp.ones((d, C), F32)
            zero = jnp.zeros((d, C), F32)
            if reverse:
                As = jnp.concatenate([A[d:], one], axis=0)
                Hs = jnp.concatenate([H[d:], zero], axis=0)
            else:
                As = jnp.concatenate([one, A[:L - d]], axis=0)
                Hs = jnp.concatenate([zero, H[:L - d]], axis=0)
        H = A * Hs + H
        A = A * As
        d *= 2
    return A, H


def _lru_kernel(xg_ref, xr_ref, cw_ref, cb_ref, wg_ref, bg_ref, lam_ref, y_ref, xp_ref, hf_ref):
    S, C = xr_ref.shape
    L = LRU_CHUNK
    n_chunks = S // L
    halo = SUBLANES

    xp_ref[0:halo, :] = jnp.zeros((halo, C), F32)
    xp_ref[S + halo:S + 2 * halo, :] = jnp.zeros((halo, C), F32)

    def copy_body(ci, carry):
        t0 = pl.multiple_of(ci * L, L)
        xp_ref[pl.ds(t0 + halo, L), :] = xr_ref[pl.ds(t0, L), :]
        return carry

    lax.fori_loop(0, n_chunks, copy_body, 0)

    cw = cw_ref[...]
    cb = cb_ref[...]
    bias = bg_ref[0]
    neg_c_softplus = -LRU_C * jax.nn.softplus(-lam_ref[0])

    def conv_chunk(t0):
        win = xp_ref[pl.ds(t0, L + 2 * halo), :]
        acc = cb
        for j in range(CONV_WIDTH):
            off = halo - CONV_PAD_LEFT + j
            acc = acc + cw[j:j + 1, :] * win[off:off + L, :]
        return acc

    def gate_au(c, direction):
        w = wg_ref[0, :, direction * 2 * C:(direction + 1) * 2 * C]
        z = jnp.dot(c.astype(BF16), w, preferred_element_type=F32) + bias[:, direction * 2 * C:(direction + 1) * 2 * C]
        r = jax.nn.sigmoid(z[:, :C])
        i = jax.nn.sigmoid(z[:, C:])
        log_a = r * neg_c_softplus[direction:direction + 1, :]
        a = jnp.exp(log_a)
        u = jnp.sqrt(1.0 - a * a) * (i * c)
        return a, u

    def fwd_body(ci, carry):
        t0 = pl.multiple_of(ci * L, L)
        c = conv_chunk(t0)
        a, u = gate_au(c, 0)
        A, H = _scan_chunk(a, u, reverse=False)
        h = H + A * carry
        hf_ref[pl.ds(t0, L), :] = h
        return h[L - 1:L, :]

    lax.fori_loop(0, n_chunks, fwd_body, jnp.zeros((1, C), F32))

    def bwd_body(k, carry):
        t0 = pl.multiple_of((n_chunks - 1 - k) * L, L)
        c = conv_chunk(t0)
        a, u = gate_au(c, 1)
        A, H = _scan_chunk(a, u, reverse=True)
        h = H + A * carry
        y_ref[pl.ds(t0, L), :] = jax.nn.gelu(xg_ref[pl.ds(t0, L), :]) * (hf_ref[pl.ds(t0, L), :] + h)
        return h[0:1, :]

    lax.fori_loop(0, n_chunks, bwd_body, jnp.zeros((1, C), F32))


def _block_diag_chunks(w, per_chunk):
    nblk, b, _ = w.shape
    w4 = w.reshape(nblk // per_chunk, per_chunk, b, b)
    eye = jnp.eye(per_chunk, dtype=w.dtype)
    m = w4[:, :, :, None, :] * eye[None, :, None, :, None]
    return m.reshape(nblk // per_chunk, per_chunk * b, per_chunk * b)


def _lru(gr, conv_w, conv_b, gate_ws, gate_bs, lams, batch, seq):
    T, two_w = gr.shape
    W = two_w // 2
    C = LRU_CHANNELS
    n_ch = W // C
    per_chunk = C // LRU_BLOCK
    wg = jnp.concatenate([_block_diag_chunks(w, per_chunk) for w in gate_ws], axis=-1).astype(BF16)
    bg = jnp.concatenate([b.reshape(n_ch, 1, C) for b in gate_bs], axis=-1)
    lam = jnp.stack([l.reshape(n_ch, C) for l in lams], axis=1)
    return pl.pallas_call(
        _lru_kernel,
        grid=(batch, n_ch),
        in_specs=[
            pl.BlockSpec((seq, C), lambda b, c: (b, c)),
            pl.BlockSpec((seq, C), lambda b, c: (b, n_ch + c)),
            pl.BlockSpec((CONV_WIDTH, C), lambda b, c: (0, c)),
            pl.BlockSpec((1, C), lambda b, c: (0, c)),
            pl.BlockSpec((1, C, 4 * C), lambda b, c: (c, 0, 0)),
            pl.BlockSpec((1, 1, 4 * C), lambda b, c: (c, 0, 0)),
            pl.BlockSpec((1, 2, C), lambda b, c: (c, 0, 0)),
        ],
        out_specs=pl.BlockSpec((seq, C), lambda b, c: (b, c)),
        out_shape=jax.ShapeDtypeStruct((T, W), F32),
        scratch_shapes=[pltpu.VMEM((seq + 2 * SUBLANES, C), F32), pltpu.VMEM((seq, C), F32)],
        compiler_params=_cparams(("parallel", "parallel")),
        name="lru",
    )(gr, gr, conv_w, conv_b.reshape(1, W), wg, bg, lam)


def _outproj_kernel(ya_ref, yl_ref, x_ref, ga_ref, gl_ref, w_ref, gf_ref, x1_ref, xn_ref):
    y = jnp.concatenate([_rms(ya_ref[...], ga_ref[...]), _rms(yl_ref[...], gl_ref[...])], axis=1).astype(BF16)
    x1 = x_ref[...] + jnp.dot(y, w_ref[...], preferred_element_type=F32)
    x1_ref[...] = x1
    xn_ref[...] = _rms(x1, gf_ref[...])


def _outproj(y_attn, y_lru, xt, g_attn, g_lru, w_out_bf, g_ffn):
    T, D = xt.shape
    wa, wl = y_attn.shape[1], y_lru.shape[1]
    tm = OUTPROJ_TM
    return pl.pallas_call(
        _outproj_kernel,
        grid=(T // tm,),
        in_specs=[
            pl.BlockSpec((tm, wa), lambda i: (i, 0)),
            pl.BlockSpec((tm, wl), lambda i: (i, 0)),
            pl.BlockSpec((tm, D), lambda i: (i, 0)),
            pl.BlockSpec((1, wa), lambda i: (0, 0)),
            pl.BlockSpec((1, wl), lambda i: (0, 0)),
            pl.BlockSpec((wa + wl, D), lambda i: (0, 0)),
            pl.BlockSpec((1, D), lambda i: (0, 0)),
        ],
        out_specs=[pl.BlockSpec((tm, D), lambda i: (i, 0)), pl.BlockSpec((tm, D), lambda i: (i, 0))],
        out_shape=[jax.ShapeDtypeStruct((T, D), F32), jax.ShapeDtypeStruct((T, D), F32)],
        compiler_params=_cparams(("parallel",)),
        name="outproj",
    )(y_attn, y_lru, xt, g_attn, g_lru, w_out_bf, g_ffn)


def _pq_kernel(xn_ref, w_ref, k_ref, s_ref):
    q = jnp.dot(xn_ref[...].astype(BF16), w_ref[...], preferred_element_type=F32).astype(BF16)
    for hh in range(2 * PEER_HEADS):
        half, head = divmod(hh, PEER_HEADS)
        col = (head * 2 + half) * PEER_HALF
        qh = q[:, col:col + PEER_HALF]
        s_ref[hh] = lax.dot_general(k_ref[hh], qh, (((1,), (1,)), ((), ())), preferred_element_type=F32)


def _peer_scores(xn, w_pq_bf, keys_bf):
    T, D = xn.shape
    tm = PQ_TM
    nh = keys_bf.shape[0]
    return pl.pallas_call(
        _pq_kernel,
        grid=(T // tm,),
        in_specs=[
            pl.BlockSpec((tm, D), lambda i: (i, 0)),
            pl.BlockSpec(w_pq_bf.shape, lambda i: (0, 0)),
            pl.BlockSpec(keys_bf.shape, lambda i: (0, 0, 0)),
        ],
        out_specs=pl.BlockSpec((nh, N_KEYS, tm), lambda i: (0, 0, i)),
        out_shape=jax.ShapeDtypeStruct((nh, N_KEYS, T), F32),
        compiler_params=_cparams(("parallel",)),
        name="pq",
    )(xn, w_pq_bf, keys_bf)


def _top16_rows(s, payload=None):
    n, tl = s.shape
    row = lax.broadcasted_iota(jnp.int32, (n, tl), 0)
    vals, idxs, pays = [], [], []
    for _ in range(PEER_TOPK):
        m = jnp.max(s, axis=0, keepdims=True)
        idx = jnp.min(jnp.where(s == m, row, n), axis=0, keepdims=True)
        hit = row == idx
        vals.append(m)
        idxs.append(idx)
        if payload is not None:
            pays.append(jnp.max(jnp.where(hit, payload, -1), axis=0, keepdims=True))
        s = jnp.where(hit, -jnp.inf, s)
    cat = lambda xs: jnp.concatenate(xs, axis=0)
    return cat(vals), cat(idxs), (cat(pays) if payload is not None else None)


def _topk_kernel(s1_ref, s2_ref, g_ref, e_ref):
    v1, i1, _ = _top16_rows(s1_ref[0])
    v2, i2, _ = _top16_rows(s2_ref[0])
    K = PEER_TOPK
    tl = v1.shape[1]
    sub = lax.broadcasted_iota(jnp.int32, (SUBLANES, tl), 0)
    cand, cidx = [], []
    for a in range(K // 2):
        n_b = K // (a + 1)
        for b0 in range(0, n_b, SUBLANES):
            keep = sub < (n_b - b0)
            cand.append(jnp.where(keep, v1[a:a + 1, :] + v2[b0:b0 + SUBLANES, :], -jnp.inf))
            cidx.append(i1[a:a + 1, :] * N_KEYS + i2[b0:b0 + SUBLANES, :])
    cand.append(v1[K // 2:, :] + v2[0:1, :])
    cidx.append(i1[K // 2:, :] * N_KEYS + i2[0:1, :])
    sc, _, eidx = _top16_rows(jnp.concatenate(cand, axis=0), jnp.concatenate(cidx, axis=0))
    ex = jnp.exp(sc - sc[0:1, :])
    g_ref[0] = ex / jnp.sum(ex, axis=0, keepdims=True)
    e_ref[0] = eidx


def _peer_topk(scores):
    nh2, nk, T = scores.shape
    nh = nh2 // 2
    tl = TOPK_TL
    return pl.pallas_call(
        _topk_kernel,
        grid=(nh, T // tl),
        in_specs=[
            pl.BlockSpec((1, nk, tl), lambda h, i: (h, 0, i)),
            pl.BlockSpec((1, nk, tl), lambda h, i: (nh + h, 0, i)),
        ],
        out_specs=[
            pl.BlockSpec((1, PEER_TOPK, tl), lambda h, i: (h, 0, i)),
            pl.BlockSpec((1, PEER_TOPK, tl), lambda h, i: (h, 0, i)),
        ],
        out_shape=[
            jax.ShapeDtypeStruct((nh, PEER_TOPK, T), F32),
            jax.ShapeDtypeStruct((nh, PEER_TOPK, T), jnp.int32),
        ],
        compiler_params=_cparams(("parallel", "parallel")),
        name="topk",
    )(scores, scores)


ROW_SUBLANES = 16


def _pack_rows(emb):
    n, d = emb.shape
    half = d // 2
    assert half == SUBLANES * LANES
    bits = lax.bitcast_convert_type(emb.astype(BF16), jnp.uint16).astype(jnp.uint32)
    return (bits[:, :half] | (bits[:, half:] << 16)).reshape(n * SUBLANES, LANES)


def _unpack_rows(word):
    return pltpu.bitcast(word << 16, F32), pltpu.bitcast(word & jnp.uint32(0xFFFF0000), F32)


def _sublane_fold(parts):
    sub = lax.broadcasted_iota(jnp.int32, (SUBLANES, LANES), 0)
    step = 1
    while len(parts) > 1:
        low = (sub & step) == 0
        parts = [jnp.where(low, a, b) + pltpu.roll(jnp.where(low, b, a), step, 0)
                 for a, b in zip(parts[0::2], parts[1::2])]
        step *= 2
    return parts[0]


def _route(eidx, gates, n_blocks):
    T, P = eidx.shape
    u32 = jnp.uint32
    blocks = jnp.arange(n_blocks, dtype=jnp.int32)
    blk = eidx // EXPERT_BLOCK
    cnt = jnp.sum((blk[:, :, None] == blocks[None, None, :]).astype(jnp.int32), axis=1)
    pcnt = (cnt + CHUNK - 1) // CHUNK * CHUNK
    cand = jnp.arange(CHUNK, dtype=jnp.int32)
    active = cand[None, None, :] < (pcnt - cnt)[:, :, None]
    gate_bits = (lax.bitcast_convert_type(gates, u32) + u32(1 << (GATE_SHIFT - 1))) >> GATE_SHIFT
    pad_mark = u32((1 << GATE_BITS) - 1)
    word = (eidx.astype(u32) << GATE_BITS) | gate_bits
    last_row = ((blocks + 1) * EXPERT_BLOCK - 1).astype(u32)
    pad_word = jnp.where(active, ((last_row << GATE_BITS) | pad_mark)[None, :, None], u32(0xFFFFFFFF))
    srt = lax.sort(jnp.concatenate([word, pad_word.reshape(T, n_blocks * CHUNK)], axis=1), dimension=1)
    is_pad = (srt & pad_mark) == pad_mark
    loc = jnp.where(is_pad, u32(0), ((srt >> GATE_BITS) & u32(EXPERT_BLOCK - 1)) * SUBLANES)
    gate = lax.bitcast_convert_type(jnp.where(is_pad, u32(0), (srt & pad_mark) << GATE_SHIFT), F32)
    ends = jnp.cumsum(pcnt, axis=1) // CHUNK
    cs = jnp.concatenate([jnp.zeros((T, 1), jnp.int32), ends,
                          jnp.zeros((T, CS_STRIDE - n_blocks - 1), jnp.int32)], axis=1)
    n_chunks = jnp.sum((ends - cs[:, :n_blocks]).reshape(T // PEER_TM, PEER_TM, n_blocks), axis=1)
    return loc.astype(jnp.int32).reshape(-1), gate.reshape(-1), cs.reshape(-1), n_chunks.T.reshape(-1)


def _pdot_kernel(cs_ref, idx_ref, gate_ref, x_ref, u_ref, d_ref, g_ref, rbuf_ref, gbuf_ref, *, slots):
    b = pl.program_id(0)
    tm = x_ref.shape[0]
    zero_rows = jnp.zeros((SUBLANES, LANES), F32)
    sub = lax.broadcasted_iota(jnp.int32, (SUBLANES, LANES), 0)

    def token_body(tl, q):
        c0 = cs_ref[tl * CS_STRIDE + b]
        c1 = cs_ref[tl * CS_STRIDE + b + 1]
        x = x_ref[tl]
        x_lo, x_hi = x[:SUBLANES], x[SUBLANES:]
        base = tl * slots

        def chunk(c, q):
            idx_c = idx_ref.at[pl.ds(base + c * CHUNK, CHUNK)]
            gate_c = gate_ref.at[pl.ds(base + c * CHUNK, CHUNK)]
            for h in range(CHUNK // SUBLANES):
                parts = []
                gates = zero_rows
                for k in range(SUBLANES):
                    row = pl.multiple_of(idx_c[h * SUBLANES + k], SUBLANES)
                    lo, hi = _unpack_rows(u_ref[pl.ds(row, SUBLANES), :])
                    parts.append(lo * x_lo + hi * x_hi)
                    gates = jnp.where(sub == k, gate_c[h * SUBLANES + k], gates)
                row0 = pl.multiple_of(q * CHUNK + h * SUBLANES, SUBLANES)
                rbuf_ref[pl.ds(row0, SUBLANES), :] = _sublane_fold(parts)
                gbuf_ref[pl.ds(row0, SUBLANES), :] = gates
            return q + 1

        q = lax.fori_loop(0, (c1 - c0) // 2, lambda i, q: chunk(c0 + 2 * i + 1, chunk(c0 + 2 * i, q)), q)
        return lax.cond((c1 - c0) % 2 == 1, lambda q: chunk(c1 - 1, q), lambda q: q, q)

    n_chunks = lax.fori_loop(0, tm, token_body, 0)

    per_step = SUBLANES * LANES // CHUNK
    n_steps = (n_chunks + per_step - 1) // per_step
    folds_per_chunk = CHUNK // SUBLANES

    def zero_body(f, carry):
        rbuf_ref[pl.ds(pl.multiple_of(f * SUBLANES, SUBLANES), SUBLANES), :] = zero_rows
        gbuf_ref[pl.ds(pl.multiple_of(f * SUBLANES, SUBLANES), SUBLANES), :] = zero_rows
        return carry

    lax.fori_loop(n_chunks * folds_per_chunk, n_steps * per_step * folds_per_chunk, zero_body, 0)
    d_ref[...] = jnp.zeros(d_ref.shape, F32)
    g_ref[...] = jnp.zeros(g_ref.shape, F32)

    diag = (lax.broadcasted_iota(jnp.int32, (LANES, LANES), 0) == lax.broadcasted_iota(jnp.int32, (LANES, LANES), 1))

    def reduce_body(s, carry):
        dots, gates = [], []
        for j in range(SUBLANES):
            rows = pl.ds(pl.multiple_of((s * SUBLANES + j) * LANES, LANES), LANES)
            dots.append(jnp.sum(rbuf_ref[rows, :].T, axis=0, keepdims=True))
            gates.append(jnp.sum(jnp.where(diag, gbuf_ref[rows, :], 0.0), axis=0, keepdims=True))
        out_rows = pl.ds(pl.multiple_of(s * SUBLANES, SUBLANES), SUBLANES)
        d_ref[out_rows, :] = jnp.concatenate(dots, axis=0)
        g_ref[out_rows, :] = jnp.concatenate(gates, axis=0)
        return carry

    lax.fori_loop(0, n_steps, reduce_body, 0)


def _expert_block_spec():
    return pl.BlockSpec((EXPERT_BLOCK * SUBLANES, LANES), lambda b, i: (b, 0), pipeline_mode=pl.Buffered(1))


def _peer_dots(cs, idx, gate, x3, u_rows, slots):
    T = x3.shape[0]
    nb = u_rows.shape[0] // (EXPERT_BLOCK * SUBLANES)
    tm = PEER_TM
    n_tiles = T // tm
    max_pairs = tm * slots
    groups = max_pairs // LANES
    smem_tile = lambda n: pl.BlockSpec((tm * n,), lambda b, i: (i,), memory_space=pltpu.SMEM)
    out = jax.ShapeDtypeStruct((nb, n_tiles, groups, LANES), F32)
    out_spec = pl.BlockSpec((None, None, groups, LANES), lambda b, i: (b, i, 0, 0))
    return pl.pallas_call(
        functools.partial(_pdot_kernel, slots=slots),
        grid=(nb, n_tiles),
        in_specs=[
            smem_tile(CS_STRIDE), smem_tile(slots), smem_tile(slots),
            pl.BlockSpec((tm, ROW_SUBLANES, LANES), lambda b, i: (i, 0, 0)),
            _expert_block_spec(),
        ],
        out_specs=[out_spec, out_spec],
        out_shape=[out, out],
        scratch_shapes=[pltpu.VMEM((max_pairs, LANES), F32), pltpu.VMEM((max_pairs, LANES), F32)],
        compiler_params=_cparams(("arbitrary", "arbitrary")),
        name="pdot",
    )(cs, idx, gate, x3, u_rows)


def _act_kernel(d_ref, g_ref, w_ref):
    w_ref[...] = g_ref[...] * jax.nn.gelu(d_ref[...])


def _peer_act(dots, gates):
    rows, lanes = dots.shape
    tm = min(4096, rows)
    spec = pl.BlockSpec((tm, lanes), lambda i: (i, 0))
    return pl.pallas_call(
        _act_kernel,
        grid=(rows // tm,),
        in_specs=[spec, spec],
        out_specs=spec,
        out_shape=jax.ShapeDtypeStruct((rows, lanes), F32),
        compiler_params=_cparams(("parallel",)),
        name="act",
    )(dots, gates)


def _pacc_kernel(nch_ref, cs_ref, idx_ref, w_ref, v_ref, o_ref, wrep_ref, *, slots):
    b = pl.program_id(0)
    i = pl.program_id(1)
    tm = o_ref.shape[0]

    n_groups = (nch_ref[b * pl.num_programs(1) + i] * CHUNK + LANES - 1) // LANES
    unroll = 4

    def expand_body(g4, carry):
        for j in range(unroll):
            g = g4 * unroll + j
            rows = jnp.broadcast_to(w_ref[pl.ds(g, 1), :], (LANES, LANES))
            wrep_ref[pl.ds(pl.multiple_of(g * LANES, LANES), LANES), :] = rows.T
        return carry

    assert w_ref.shape[0] % unroll == 0
    lax.fori_loop(0, (n_groups + unroll - 1) // unroll, expand_body, 0)

    def token_body(tl, q):
        c0 = cs_ref[tl * CS_STRIDE + b]
        c1 = cs_ref[tl * CS_STRIDE + b + 1]
        base = tl * slots

        def chunk(c, carry):
            q, accs = carry
            accs = list(accs)
            idx_c = idx_ref.at[pl.ds(base + c * CHUNK, CHUNK)]
            for k in range(CHUNK):
                w = jnp.broadcast_to(wrep_ref[pl.ds(q * CHUNK + k, 1), :], (SUBLANES, LANES))
                lo, hi = _unpack_rows(v_ref[pl.ds(pl.multiple_of(idx_c[k], SUBLANES), SUBLANES), :])
                j = 2 * (k % 2)
                accs[j] = accs[j] + w * lo
                accs[j + 1] = accs[j + 1] + w * hi
            return q + 1, tuple(accs)

        zero = jnp.zeros((SUBLANES, LANES), F32)
        q, accs = lax.fori_loop(c0, c1, chunk, (q, (zero, zero, zero, zero)))
        o_ref[tl] = jnp.concatenate([accs[0] + accs[2], accs[1] + accs[3]], axis=0)
        return q

    lax.fori_loop(0, tm, token_body, 0)


def _peer_accumulate(n_chunks, cs, idx, w, v_rows, T, slots):
    nb = v_rows.shape[0] // (EXPERT_BLOCK * SUBLANES)
    tm = PEER_TM
    n_tiles = T // tm
    groups = tm * slots // LANES
    smem_tile = lambda n: pl.BlockSpec((tm * n,), lambda b, i: (i,), memory_space=pltpu.SMEM)
    return pl.pallas_call(
        functools.partial(_pacc_kernel, slots=slots),
        grid=(nb, n_tiles),
        in_specs=[
            pl.BlockSpec(memory_space=pltpu.SMEM),
            smem_tile(CS_STRIDE), smem_tile(slots),
            pl.BlockSpec((None, None, groups, LANES), lambda b, i: (b, i, 0, 0)),
            _expert_block_spec(),
        ],
        out_specs=pl.BlockSpec((None, tm, ROW_SUBLANES, LANES), lambda b, i: (b, i, 0, 0)),
        out_shape=jax.ShapeDtypeStruct((nb, T, ROW_SUBLANES, LANES), F32),
        scratch_shapes=[pltpu.VMEM((tm * slots, LANES), F32)],
        compiler_params=_cparams(("arbitrary", "arbitrary")),
        name="pacc",
    )(n_chunks, cs, idx, w, v_rows)


def _final_kernel(x1_ref, p_ref, g_ref, o_ref):
    acc = x1_ref[...]
    for b in range(p_ref.shape[0]):
        acc = acc + p_ref[b]
    o_ref[...] = _rms(acc, g_ref[...])


def _final(x1, peer_parts, g_final):
    T, D = x1.shape
    nb = peer_parts.shape[0]
    tm = 256
    return pl.pallas_call(
        _final_kernel,
        grid=(T // tm,),
        in_specs=[
            pl.BlockSpec((tm, D), lambda i: (i, 0)),
            pl.BlockSpec((nb, tm, D), lambda i: (0, i, 0)),
            pl.BlockSpec((1, D), lambda i: (0, 0)),
        ],
        out_specs=pl.BlockSpec((tm, D), lambda i: (i, 0)),
        out_shape=jax.ShapeDtypeStruct((T, D), F32),
        compiler_params=_cparams(("parallel",)),
        name="final",
    )(x1, peer_parts, g_final)


def _rope_tables(seq):
    half = HEAD_DIM // 2
    inv = ROPE_THETA ** (-jnp.arange(half, dtype=F32) / half)
    ang = jnp.arange(seq).astype(F32)[:, None] * inv[None, :]
    cos, sin = jnp.cos(ang), jnp.sin(ang)
    reps = LANES // HEAD_DIM
    cos_t = jnp.concatenate([cos, cos] * reps, axis=1)
    sin_t = jnp.concatenate([-sin, sin] * reps, axis=1)
    return cos_t, sin_t


def _layer(xt, batch, seq, g_mix, w_in, sink, conv_w, conv_b, fwd_wa, fwd_ba, fwd_wx, fwd_bx, fwd_lam,
           bwd_wa, bwd_ba, bwd_wx, bwd_bx, bwd_lam, g_attn_out, g_lru_out, w_out, g_ffn,
           w_pq, sub_k1, sub_k2, u_emb, v_emb):
    T, D = xt.shape
    cos_t, sin_t = _rope_tables(seq)
    qkv, gr = _inproj(xt, g_mix.reshape(1, D), w_in.astype(BF16), cos_t, sin_t, seq)
    y_attn = _attention(qkv, sink, batch, seq)
    y_lru = _lru(gr, conv_w, conv_b, (fwd_wa, fwd_wx, bwd_wa, bwd_wx), (fwd_ba, fwd_bx, bwd_ba, bwd_bx),
                 (fwd_lam, bwd_lam), batch, seq)
    x1, xn = _outproj(y_attn, y_lru, xt, g_attn_out.reshape(1, -1), g_lru_out.reshape(1, -1),
                      w_out.astype(BF16), g_ffn.reshape(1, D))

    keys = jnp.concatenate([sub_k1, sub_k2], axis=0).astype(BF16)
    scores = _peer_scores(xn, w_pq.astype(BF16), keys)
    gates_t, eidx_t = _peer_topk(scores)
    n_pairs = PEER_HEADS * PEER_TOPK
    gates = gates_t.transpose(2, 0, 1).reshape(T, n_pairs)
    eidx = eidx_t.transpose(2, 0, 1).reshape(T, n_pairs)
    n_exp = u_emb.shape[0]
    nb = n_exp // EXPERT_BLOCK
    slots = n_pairs + nb * CHUNK
    idx, gate, cs, n_chunks = _route(eidx, gates, nb)

    x3 = xn.reshape(T, ROW_SUBLANES, LANES)
    u_rows = _pack_rows(u_emb)
    v_rows = _pack_rows(v_emb)
    dots, gates_c = _peer_dots(cs, idx, gate, x3, u_rows, slots)
    w = _peer_act(dots.reshape(-1, LANES), gates_c.reshape(-1, LANES)).reshape(dots.shape)
    parts = _peer_accumulate(n_chunks, cs, idx, w, v_rows, T, slots)
    return x1, parts.reshape(nb, T, D)


def kernel(x, g_mix, w_in, sink, conv_w, conv_b, fwd_wa, fwd_ba, fwd_wx, fwd_bx, fwd_lam, bwd_wa, bwd_ba, bwd_wx, bwd_bx, bwd_lam, g_attn_out, g_lru_out, w_out, g_ffn, w_pq, sub_k1, sub_k2, u_emb, v_emb, g_final):
    B, S, D = x.shape
    assert g_mix.shape[0] == 1, "single-layer trunk"
    xt = x.reshape(B * S, D)
    x1, parts = _layer(xt, B, S, g_mix[0], w_in[0], sink[0], conv_w[0], conv_b[0],
                       fwd_wa[0], fwd_ba[0], fwd_wx[0], fwd_bx[0], fwd_lam[0],
                       bwd_wa[0], bwd_ba[0], bwd_wx[0], bwd_bx[0], bwd_lam[0],
                       g_attn_out[0], g_lru_out[0], w_out[0], g_ffn[0],
                       w_pq[0], sub_k1[0], sub_k2[0], u_emb[0], v_emb[0])
    return _final(x1, parts, g_final.reshape(1, D)).reshape(B, S, D)
```

```python
import functools

import jax
import jax.numpy as jnp
from jax import lax
from jax.experimental import pallas as pl
from jax.experimental.pallas import tpu as pltpu

F32 = jnp.float32
BF16 = jnp.bfloat16

HEAD_DIM = 64
N_KV_HEADS = 4
Q_PER_KV = 4
N_Q_HEADS = N_KV_HEADS * Q_PER_KV
WINDOW = 128
ATTN_BLOCK = 128
ROPE_THETA = 10000.0
LRU_BLOCK = 64
CONV_WIDTH = 4
CONV_PAD_LEFT = 2
LRU_C = 8.0
PEER_HEADS = 8
PEER_HALF = 128
N_KEYS = 128
PEER_TOPK = 16
EPS = 1e-6
NEG = -1e30

LANES = 128
SUBLANES = 8
VMEM_LIMIT = 56 * 1024 * 1024

INPROJ_TM = 512
INPROJ_TN = 512
LRU_CHANNELS = 256
LRU_CHUNK = 256
OUTPROJ_TM = 256
PQ_TM = 256
TOPK_TL = 512
EXPERT_BLOCK = 8192
PEER_TM = 64
CHUNK = 16
CS_STRIDE = 8
GATE_BITS = 18
GATE_SHIFT = 31 - GATE_BITS


def _cparams(sem):
    return pltpu.CompilerParams(dimension_semantics=sem, vmem_limit_bytes=VMEM_LIMIT)


def _rms(x, g):
    return x * lax.rsqrt(jnp.mean(x * x, axis=-1, keepdims=True) + EPS) * g


def _inproj_kernel(x_ref, g_ref, w_ref, cos_ref, sin_ref, qkv_ref, gr_ref, h_ref, *, n_rope_tiles, kv_cols):
    j = pl.program_id(1)

    @pl.when(j == 0)
    def _():
        h_ref[...] = _rms(x_ref[...], g_ref[...]).astype(BF16)

    acc = jnp.dot(h_ref[...], w_ref[...], preferred_element_type=F32)
    tm, tn = acc.shape

    @pl.when(j < n_rope_tiles)
    def _():
        reps = tn // LANES
        cos = jnp.concatenate([cos_ref[...]] * reps, axis=1)
        sin = jnp.concatenate([sin_ref[...]] * reps, axis=1)
        lane = lax.broadcasted_iota(jnp.int32, (tm, tn), 1)
        first = (lane % HEAD_DIM) < (HEAD_DIM // 2)
        partner = jnp.where(first, pltpu.roll(acc, tn - HEAD_DIM // 2, 1), pltpu.roll(acc, HEAD_DIM // 2, 1))
        roped = acc * cos + partner * sin
        is_rope = jnp.logical_or(j < n_rope_tiles - 1, lane < kv_cols)
        qkv_ref[...] = jnp.where(is_rope, roped, acc).astype(BF16)

    @pl.when(j >= n_rope_tiles)
    def _():
        gr_ref[...] = acc


def _inproj(xt, g_mix, w_in_bf, cos_t, sin_t, seq):
    T, D = xt.shape
    n_cols = w_in_bf.shape[1]
    q_cols = N_Q_HEADS * HEAD_DIM
    kv_cols = N_KV_HEADS * HEAD_DIM
    qkv_cols = q_cols + 2 * kv_cols
    tm, tn = INPROJ_TM, INPROJ_TN
    n_rope_tiles = qkv_cols // tn
    assert qkv_cols % tn == 0 and (q_cols % tn == 0) and seq % tm == 0 and T % tm == 0
    pos_blocks = seq // tm
    kern = functools.partial(_inproj_kernel, n_rope_tiles=n_rope_tiles, kv_cols=kv_cols)
    return pl.pallas_call(
        kern,
        grid=(T // tm, n_cols // tn),
        in_specs=[
            pl.BlockSpec((tm, D), lambda i, j: (i, 0)),
            pl.BlockSpec((1, D), lambda i, j: (0, 0)),
            pl.BlockSpec((D, tn), lambda i, j: (0, j)),
            pl.BlockSpec((tm, LANES), lambda i, j: (i % pos_blocks, 0)),
            pl.BlockSpec((tm, LANES), lambda i, j: (i % pos_blocks, 0)),
        ],
        out_specs=[
            pl.BlockSpec((tm, tn), lambda i, j: (i, jnp.minimum(j, n_rope_tiles - 1))),
            pl.BlockSpec((tm, tn), lambda i, j: (i, jnp.maximum(j - n_rope_tiles, 0))),
        ],
        out_shape=[
            jax.ShapeDtypeStruct((T, qkv_cols), BF16),
            jax.ShapeDtypeStruct((T, n_cols - qkv_cols), F32),
        ],
        scratch_shapes=[pltpu.VMEM((tm, D), BF16)],
        compiler_params=_cparams(("parallel", "arbitrary")),
        name="inproj",
    )(xt, g_mix, w_in_bf, cos_t, sin_t)


def _attn_kernel(sink_ref, q_ref, kp_ref, kc_ref, kn_ref, vp_ref, vc_ref, vn_ref, o_ref, *, seq):
    n = pl.program_id(1)
    q = q_ref[...]
    k = jnp.concatenate([kp_ref[...], kc_ref[...], kn_ref[...]], axis=0)
    v = jnp.concatenate([vp_ref[...], vc_ref[...], vn_ref[...]], axis=0)
    nq, nk = ATTN_BLOCK, 3 * ATTN_BLOCK
    qpos = n * ATTN_BLOCK + lax.broadcasted_iota(jnp.int32, (nq, nk), 0)
    kpos = (n - 1) * ATTN_BLOCK + lax.broadcasted_iota(jnp.int32, (nq, nk), 1)
    valid = (jnp.abs(kpos - qpos) <= WINDOW) & (kpos >= 0) & (kpos < seq)
    scale = HEAD_DIM ** -0.5
    outs = []
    for g in range(N_KV_HEADS):
        kg = k[:, g * HEAD_DIM:(g + 1) * HEAD_DIM]
        vg = v[:, g * HEAD_DIM:(g + 1) * HEAD_DIM]
        for r in range(Q_PER_KV):
            h = g * Q_PER_KV + r
            qh = q[:, h * HEAD_DIM:(h + 1) * HEAD_DIM]
            s = lax.dot_general(qh, kg, (((1,), (1,)), ((), ())), preferred_element_type=F32) * scale
            s = jnp.where(valid, s, NEG)
            sk = sink_ref[h]
            m = jnp.maximum(jnp.max(s, axis=-1, keepdims=True), sk)
            p = jnp.exp(s - m)
            den = jnp.sum(p, axis=-1, keepdims=True) + jnp.exp(sk - m)
            o = jnp.dot(p.astype(BF16), vg, preferred_element_type=F32) / den
            outs.append(o)
    o_ref[...] = jnp.concatenate(outs, axis=1)


def _attention(qkv, sink, batch, seq):
    T = qkv.shape[0]
    nb = seq // ATTN_BLOCK
    q_cols = N_Q_HEADS * HEAD_DIM
    kv_cols = N_KV_HEADS * HEAD_DIM
    k_blk = q_cols // kv_cols
    v_blk = k_blk + 1
    blk = ATTN_BLOCK

    def row(b, n):
        return b * nb + n

    def kv_spec(col_blk, shift):
        return pl.BlockSpec((blk, kv_cols), lambda b, n: (row(b, jnp.clip(n + shift, 0, nb - 1)), col_blk))

    return pl.pallas_call(
        functools.partial(_attn_kernel, seq=seq),
        grid=(batch, nb),
        in_specs=[
            pl.BlockSpec(memory_space=pltpu.SMEM),
            pl.BlockSpec((blk, q_cols), lambda b, n: (row(b, n), 0)),
            kv_spec(k_blk, -1), kv_spec(k_blk, 0), kv_spec(k_blk, 1),
            kv_spec(v_blk, -1), kv_spec(v_blk, 0), kv_spec(v_blk, 1),
        ],
        out_specs=pl.BlockSpec((blk, q_cols), lambda b, n: (row(b, n), 0)),
        out_shape=jax.ShapeDtypeStruct((T, q_cols), F32),
        compiler_params=_cparams(("parallel", "parallel")),
        name="attn",
    )(sink, qkv, qkv, qkv, qkv, qkv, qkv, qkv)


def _scan_chunk(a, u, reverse):
    L, C = a.shape
    row = lax.broadcasted_iota(jnp.int32, (L, C), 0)
    A, H = a, u
    d = 1
    while d < L:
        if d < SUBLANES:
            if reverse:
                keep = row < L - d
                As = jnp.where(keep, pltpu.roll(A, L - d, 0), 1.0)
                Hs = jnp.where(keep, pltpu.roll(H, L - d, 0), 0.0)
            else:
                keep = row >= d
                As = jnp.where(keep, pltpu.roll(A, d, 0), 1.0)
                Hs = jnp.where(keep, pltpu.roll(H, d, 0), 0.0)
        else:
            one = jnp.ones((d, C), F32)
            zero = jnp.zeros((d, C), F32)
            if reverse:
                As = jnp.concatenate([A[d:], one], axis=0)
                Hs = jnp.concatenate([H[d:], zero], axis=0)
            else:
                As = jnp.concatenate([one, A[:L - d]], axis=0)
                Hs = jnp.concatenate([zero, H[:L - d]], axis=0)
        H = A * Hs + H
        A = A * As
        d *= 2
    return A, H


def _lru_kernel(xg_ref, xr_ref, cw_ref, cb_ref, wg_ref, bg_ref, lam_ref, y_ref, xp_ref, hf_ref):
    S, C = xr_ref.shape
    L = LRU_CHUNK
    n_chunks = S // L
    halo = SUBLANES

    xp_ref[0:halo, :] = jnp.zeros((halo, C), F32)
    xp_ref[S + halo:S + 2 * halo, :] = jnp.zeros((halo, C), F32)

    def copy_body(ci, carry):
        t0 = pl.multiple_of(ci * L, L)
        xp_ref[pl.ds(t0 + halo, L), :] = xr_ref[pl.ds(t0, L), :]
        return carry

    lax.fori_loop(0, n_chunks, copy_body, 0)

    cw = cw_ref[...]
    cb = cb_ref[...]
    bias = bg_ref[0]
    neg_c_softplus = -LRU_C * jax.nn.softplus(-lam_ref[0])

    def conv_chunk(t0):
        win = xp_ref[pl.ds(t0, L + 2 * halo), :]
        acc = cb
        for j in range(CONV_WIDTH):
            off = halo - CONV_PAD_LEFT + j
            acc = acc + cw[j:j + 1, :] * win[off:off + L, :]
        return acc

    def gate_au(c, direction):
        w = wg_ref[0, :, direction * 2 * C:(direction + 1) * 2 * C]
        z = jnp.dot(c.astype(BF16), w, preferred_element_type=F32) + bias[:, direction * 2 * C:(direction + 1) * 2 * C]
        r = jax.nn.sigmoid(z[:, :C])
        i = jax.nn.sigmoid(z[:, C:])
        log_a = r * neg_c_softplus[direction:direction + 1, :]
        a = jnp.exp(log_a)
        u = jnp.sqrt(1.0 - a * a) * (i * c)
        return a, u

    def fwd_body(ci, carry):
        t0 = pl.multiple_of(ci * L, L)
        c = conv_chunk(t0)
        a, u = gate_au(c, 0)
        A, H = _scan_chunk(a, u, reverse=False)
        h = H + A * carry
        hf_ref[pl.ds(t0, L), :] = h
        return h[L - 1:L, :]

    lax.fori_loop(0, n_chunks, fwd_body, jnp.zeros((1, C), F32))

    def bwd_body(k, carry):
        t0 = pl.multiple_of((n_chunks - 1 - k) * L, L)
        c = conv_chunk(t0)
        a, u = gate_au(c, 1)
        A, H = _scan_chunk(a, u, reverse=True)
        h = H + A * carry
        y_ref[pl.ds(t0, L), :] = jax.nn.gelu(xg_ref[pl.ds(t0, L), :]) * (hf_ref[pl.ds(t0, L), :] + h)
        return h[0:1, :]

    lax.fori_loop(0, n_chunks, bwd_body, jnp.zeros((1, C), F32))


def _block_diag_chunks(w, per_chunk):
    nblk, b, _ = w.shape
    w4 = w.reshape(nblk // per_chunk, per_chunk, b, b)
    eye = jnp.eye(per_chunk, dtype=w.dtype)
    m = w4[:, :, :, None, :] * eye[None, :, None, :, None]
    return m.reshape(nblk // per_chunk, per_chunk * b, per_chunk * b)


def _lru(gr, conv_w, conv_b, gate_ws, gate_bs, lams, batch, seq):
    T, two_w = gr.shape
    W = two_w // 2
    C = LRU_CHANNELS
    n_ch = W // C
    per_chunk = C // LRU_BLOCK
    wg = jnp.concatenate([_block_diag_chunks(w, per_chunk) for w in gate_ws], axis=-1).astype(BF16)
    bg = jnp.concatenate([b.reshape(n_ch, 1, C) for b in gate_bs], axis=-1)
    lam = jnp.stack([l.reshape(n_ch, C) for l in lams], axis=1)
    return pl.pallas_call(
        _lru_kernel,
        grid=(batch, n_ch),
        in_specs=[
            pl.BlockSpec((seq, C), lambda b, c: (b, c)),
            pl.BlockSpec((seq, C), lambda b, c: (b, n_ch + c)),
            pl.BlockSpec((CONV_WIDTH, C), lambda b, c: (0, c)),
            pl.BlockSpec((1, C), lambda b, c: (0, c)),
            pl.BlockSpec((1, C, 4 * C), lambda b, c: (c, 0, 0)),
            pl.BlockSpec((1, 1, 4 * C), lambda b, c: (c, 0, 0)),
            pl.BlockSpec((1, 2, C), lambda b, c: (c, 0, 0)),
        ],
        out_specs=pl.BlockSpec((seq, C), lambda b, c: (b, c)),
        out_shape=jax.ShapeDtypeStruct((T, W), F32),
        scratch_shapes=[pltpu.VMEM((seq + 2 * SUBLANES, C), F32), pltpu.VMEM((seq, C), F32)],
        compiler_params=_cparams(("parallel", "parallel")),
        name="lru",
    )(gr, gr, conv_w, conv_b.reshape(1, W), wg, bg, lam)


def _outproj_kernel(ya_ref, yl_ref, x_ref, ga_ref, gl_ref, w_ref, gf_ref, x1_ref, xn_ref):
    y = jnp.concatenate([_rms(ya_ref[...], ga_ref[...]), _rms(yl_ref[...], gl_ref[...])], axis=1).astype(BF16)
    x1 = x_ref[...] + jnp.dot(y, w_ref[...], preferred_element_type=F32)
    x1_ref[...] = x1
    xn_ref[...] = _rms(x1, gf_ref[...])


def _outproj(y_attn, y_lru, xt, g_attn, g_lru, w_out_bf, g_ffn):
    T, D = xt.shape
    wa, wl = y_attn.shape[1], y_lru.shape[1]
    tm = OUTPROJ_TM
    return pl.pallas_call(
        _outproj_kernel,
        grid=(T // tm,),
        in_specs=[
            pl.BlockSpec((tm, wa), lambda i: (i, 0)),
            pl.BlockSpec((tm, wl), lambda i: (i, 0)),
            pl.BlockSpec((tm, D), lambda i: (i, 0)),
            pl.BlockSpec((1, wa), lambda i: (0, 0)),
            pl.BlockSpec((1, wl), lambda i: (0, 0)),
            pl.BlockSpec((wa + wl, D), lambda i: (0, 0)),
            pl.BlockSpec((1, D), lambda i: (0, 0)),
        ],
        out_specs=[pl.BlockSpec((tm, D), lambda i: (i, 0)), pl.BlockSpec((tm, D), lambda i: (i, 0))],
        out_shape=[jax.ShapeDtypeStruct((T, D), F32), jax.ShapeDtypeStruct((T, D), F32)],
        compiler_params=_cparams(("parallel",)),
        name="outproj",
    )(y_attn, y_lru, xt, g_attn, g_lru, w_out_bf, g_ffn)


def _pq_kernel(xn_ref, w_ref, k_ref, s_ref):
    q = jnp.dot(xn_ref[...].astype(BF16), w_ref[...], preferred_element_type=F32).astype(BF16)
    for hh in range(2 * PEER_HEADS):
        half, head = divmod(hh, PEER_HEADS)
        col = (head * 2 + half) * PEER_HALF
        qh = q[:, col:col + PEER_HALF]
        s_ref[hh] = lax.dot_general(k_ref[hh], qh, (((1,), (1,)), ((), ())), preferred_element_type=F32)


def _peer_scores(xn, w_pq_bf, keys_bf):
    T, D = xn.shape
    tm = PQ_TM
    nh = keys_bf.shape[0]
    return pl.pallas_call(
        _pq_kernel,
        grid=(T // tm,),
        in_specs=[
            pl.BlockSpec((tm, D), lambda i: (i, 0)),
            pl.BlockSpec(w_pq_bf.shape, lambda i: (0, 0)),
            pl.BlockSpec(keys_bf.shape, lambda i: (0, 0, 0)),
        ],
        out_specs=pl.BlockSpec((nh, N_KEYS, tm), lambda i: (0, 0, i)),
        out_shape=jax.ShapeDtypeStruct((nh, N_KEYS, T), F32),
        compiler_params=_cparams(("parallel",)),
        name="pq",
    )(xn, w_pq_bf, keys_bf)


def _top16_rows(s, payload=None):
    n, tl = s.shape
    row = lax.broadcasted_iota(jnp.int32, (n, tl), 0)
    vals, idxs, pays = [], [], []
    for _ in range(PEER_TOPK):
        m = jnp.max(s, axis=0, keepdims=True)
        idx = jnp.min(jnp.where(s == m, row, n), axis=0, keepdims=True)
        hit = row == idx
        vals.append(m)
        idxs.append(idx)
        if payload is not None:
            pays.append(jnp.max(jnp.where(hit, payload, -1), axis=0, keepdims=True))
        s = jnp.where(hit, -jnp.inf, s)
    cat = lambda xs: jnp.concatenate(xs, axis=0)
    return cat(vals), cat(idxs), (cat(pays) if payload is not None else None)


def _topk_kernel(s1_ref, s2_ref, g_ref, e_ref):
    v1, i1, _ = _top16_rows(s1_ref[0])
    v2, i2, _ = _top16_rows(s2_ref[0])
    K = PEER_TOPK
    tl = v1.shape[1]
    sub = lax.broadcasted_iota(jnp.int32, (SUBLANES, tl), 0)
    cand, cidx = [], []
    for a in range(K // 2):
        n_b = K // (a + 1)
        for b0 in range(0, n_b, SUBLANES):
            keep = sub < (n_b - b0)
            cand.append(jnp.where(keep, v1[a:a + 1, :] + v2[b0:b0 + SUBLANES, :], -jnp.inf))
            cidx.append(i1[a:a + 1, :] * N_KEYS + i2[b0:b0 + SUBLANES, :])
    cand.append(v1[K // 2:, :] + v2[0:1, :])
    cidx.append(i1[K // 2:, :] * N_KEYS + i2[0:1, :])
    sc, _, eidx = _top16_rows(jnp.concatenate(cand, axis=0), jnp.concatenate(cidx, axis=0))
    ex = jnp.exp(sc - sc[0:1, :])
    g_ref[0] = ex / jnp.sum(ex, axis=0, keepdims=True)
    e_ref[0] = eidx


def _peer_topk(scores):
    nh2, nk, T = scores.shape
    nh = nh2 // 2
    tl = TOPK_TL
    return pl.pallas_call(
        _topk_kernel,
        grid=(nh, T // tl),
        in_specs=[
            pl.BlockSpec((1, nk, tl), lambda h, i: (h, 0, i)),
            pl.BlockSpec((1, nk, tl), lambda h, i: (nh + h, 0, i)),
        ],
        out_specs=[
            pl.BlockSpec((1, PEER_TOPK, tl), lambda h, i: (h, 0, i)),
            pl.BlockSpec((1, PEER_TOPK, tl), lambda h, i: (h, 0, i)),
        ],
        out_shape=[
            jax.ShapeDtypeStruct((nh, PEER_TOPK, T), F32),
            jax.ShapeDtypeStruct((nh, PEER_TOPK, T), jnp.int32),
        ],
        compiler_params=_cparams(("parallel", "parallel")),
        name="topk",
    )(scores, scores)


ROW_SUBLANES = 16


def _pack_rows(emb):
    n, d = emb.shape
    half = d // 2
    assert half == SUBLANES * LANES
    bits = lax.bitcast_convert_type(emb.astype(BF16), jnp.uint16).astype(jnp.uint32)
    return (bits[:, :half] | (bits[:, half:] << 16)).reshape(n * SUBLANES, LANES)


def _unpack_rows(word):
    return pltpu.bitcast(word << 16, F32), pltpu.bitcast(word & jnp.uint32(0xFFFF0000), F32)


def _sublane_fold(parts):
    sub = lax.broadcasted_iota(jnp.int32, (SUBLANES, LANES), 0)
    step = 1
    while len(parts) > 1:
        low = (sub & step) == 0
        parts = [jnp.where(low, a, b) + pltpu.roll(jnp.where(low, b, a), step, 0)
                 for a, b in zip(parts[0::2], parts[1::2])]
        step *= 2
    return parts[0]


def _chunk_list_stride(chunks_per_token):
    return -(-PEER_TM * chunks_per_token // 1024) * 1024


def _route(eidx, gates, n_blocks):
    T, P = eidx.shape
    u32 = jnp.uint32
    blocks = jnp.arange(n_blocks, dtype=jnp.int32)
    blk = eidx // EXPERT_BLOCK
    cnt = jnp.sum((blk[:, :, None] == blocks[None, None, :]).astype(jnp.int32), axis=1)
    pcnt = (cnt + CHUNK - 1) // CHUNK * CHUNK
    cand = jnp.arange(CHUNK, dtype=jnp.int32)
    active = cand[None, None, :] < (pcnt - cnt)[:, :, None]
    gate_bits = (lax.bitcast_convert_type(gates, u32) + u32(1 << (GATE_SHIFT - 1))) >> GATE_SHIFT
    pad_mark = u32((1 << GATE_BITS) - 1)
    word = (eidx.astype(u32) << GATE_BITS) | gate_bits
    last_row = ((blocks + 1) * EXPERT_BLOCK - 1).astype(u32)
    pad_word = jnp.where(active, ((last_row << GATE_BITS) | pad_mark)[None, :, None], u32(0xFFFFFFFF))
    srt = lax.sort(jnp.concatenate([word, pad_word.reshape(T, n_blocks * CHUNK)], axis=1), dimension=1)
    is_pad = (srt & pad_mark) == pad_mark
    loc = jnp.where(is_pad, u32(0), ((srt >> GATE_BITS) & u32(EXPERT_BLOCK - 1)) * SUBLANES)
    gate = lax.bitcast_convert_type(jnp.where(is_pad, u32(0), (srt & pad_mark) << GATE_SHIFT), F32)
    ends = jnp.cumsum(pcnt, axis=1) // CHUNK
    cs = jnp.concatenate([jnp.zeros((T, 1), jnp.int32), ends,
                          jnp.zeros((T, CS_STRIDE - n_blocks - 1), jnp.int32)], axis=1)
    n_chunks = jnp.sum((ends - cs[:, :n_blocks]).reshape(T // PEER_TM, PEER_TM, n_blocks), axis=1)

    per_tok = (P + n_blocks * CHUNK) // CHUNK
    slot_j = jnp.arange(per_tok, dtype=jnp.int32)
    slot_blk = jnp.sum((slot_j[None, :, None] >= ends[:, None, :]).astype(jnp.int32), axis=2)
    tok = (jnp.arange(T, dtype=jnp.int32) % PEER_TM)[:, None]
    entry = (tok << 16) | (tok * (per_tok * CHUNK) + slot_j[None, :] * CHUNK)
    assert PEER_TM <= 1 << 8
    order = lax.sort(((slot_blk << 24) | entry).reshape(T // PEER_TM, PEER_TM * per_tok), dimension=1)
    clist = order & ((1 << 24) - 1)
    clist = jnp.pad(clist, ((0, 0), (0, _chunk_list_stride(per_tok) - PEER_TM * per_tok)))

    n_tiles, per_tile = T // PEER_TM, PEER_TM * per_tok
    run = (ends - cs[:, :n_blocks]).reshape(n_tiles, PEER_TM, n_blocks)
    before = (jnp.cumsum(run, axis=1) - run).reshape(T, n_blocks)
    gate_rows = gate.reshape(n_tiles, per_tile, CHUNK)
    rank_q = jnp.arange(per_tile, dtype=jnp.int32)
    gates_c = []
    for b in range(n_blocks):
        rank = jnp.where(slot_blk == b, before[:, b:b + 1] + slot_j[None, :] - cs[:, b:b + 1], -1)
        onehot = (rank.reshape(n_tiles, 1, per_tile) == rank_q[None, :, None]).astype(F32)
        gates_c.append(jnp.einsum('tqs,tsk->tqk', onehot, gate_rows, precision=lax.Precision.HIGHEST))
    gates_c = jnp.stack(gates_c).reshape(n_blocks, n_tiles, per_tile * CHUNK // LANES, LANES)
    return (loc.astype(jnp.int32).reshape(-1), gates_c, cs.reshape(-1), n_chunks.T.reshape(-1), clist.reshape(-1))


def _pdot_kernel(nch_ref, cl_ref, idx_ref, x_ref, u_ref, d_ref, rbuf_ref, *, n_blocks):
    b = pl.program_id(0)
    i = pl.program_id(1)
    n_tiles = pl.num_programs(1)
    n_chunks = nch_ref[b * n_tiles + i]
    first = jnp.int32(0)
    for bb in range(n_blocks - 1):
        first = first + jnp.where(bb < b, nch_ref[bb * n_tiles + i], 0)
    zero_rows = jnp.zeros((SUBLANES, LANES), F32)

    def chunk(q, carry):
        entry = cl_ref[first + q]
        base = entry & 0xFFFF
        x = x_ref[entry >> 16]
        x_lo, x_hi = x[:SUBLANES], x[SUBLANES:]
        for h in range(CHUNK // SUBLANES):
            parts = []
            for k in range(SUBLANES):
                row = pl.multiple_of(idx_ref[base + h * SUBLANES + k], SUBLANES)
                lo, hi = _unpack_rows(u_ref[pl.ds(row, SUBLANES), :])
                parts.append(lo * x_lo + hi * x_hi)
            row0 = pl.multiple_of(q * CHUNK + h * SUBLANES, SUBLANES)
            rbuf_ref[pl.ds(row0, SUBLANES), :] = _sublane_fold(parts)
        return carry

    lax.fori_loop(0, n_chunks // 2, lambda p, c: chunk(2 * p + 1, chunk(2 * p, c)), 0)

    @pl.when(n_chunks % 2 == 1)
    def _():
        chunk(n_chunks - 1, 0)

    per_step = SUBLANES * LANES // CHUNK
    n_steps = (n_chunks + per_step - 1) // per_step
    folds_per_chunk = CHUNK // SUBLANES

    def zero_body(f, carry):
        rbuf_ref[pl.ds(pl.multiple_of(f * SUBLANES, SUBLANES), SUBLANES), :] = zero_rows
        return carry

    lax.fori_loop(n_chunks * folds_per_chunk, n_steps * per_step * folds_per_chunk, zero_body, 0)
    d_ref[...] = jnp.zeros(d_ref.shape, F32)

    def reduce_body(s, carry):
        dots = []
        for j in range(SUBLANES):
            rows = pl.ds(pl.multiple_of((s * SUBLANES + j) * LANES, LANES), LANES)
            dots.append(jnp.sum(rbuf_ref[rows, :].T, axis=0, keepdims=True))
        d_ref[pl.ds(pl.multiple_of(s * SUBLANES, SUBLANES), SUBLANES), :] = jnp.concatenate(dots, axis=0)
        return carry

    lax.fori_loop(0, n_steps, reduce_body, 0)


def _expert_block_spec():
    return pl.BlockSpec((EXPERT_BLOCK * SUBLANES, LANES), lambda b, i: (b, 0), pipeline_mode=pl.Buffered(1))


def _peer_dots(n_chunks, clist, idx, x3, u_rows, slots):
    T = x3.shape[0]
    nb = u_rows.shape[0] // (EXPERT_BLOCK * SUBLANES)
    tm = PEER_TM
    n_tiles = T // tm
    max_pairs = tm * slots
    groups = max_pairs // LANES
    smem_tile = lambda n: pl.BlockSpec((tm * n,), lambda b, i: (i,), memory_space=pltpu.SMEM)
    out = jax.ShapeDtypeStruct((nb, n_tiles, groups, LANES), F32)
    out_spec = pl.BlockSpec((None, None, groups, LANES), lambda b, i: (b, i, 0, 0))
    return pl.pallas_call(
        functools.partial(_pdot_kernel, n_blocks=nb),
        grid=(nb, n_tiles),
        in_specs=[
            pl.BlockSpec(memory_space=pltpu.SMEM),
            pl.BlockSpec((_chunk_list_stride(slots // CHUNK),), lambda b, i: (i,), memory_space=pltpu.SMEM),
            smem_tile(slots),
            pl.BlockSpec((tm, ROW_SUBLANES, LANES), lambda b, i: (i, 0, 0)),
            _expert_block_spec(),
        ],
        out_specs=out_spec,
        out_shape=out,
        scratch_shapes=[pltpu.VMEM((max_pairs, LANES), F32)],
        compiler_params=_cparams(("arbitrary", "arbitrary")),
        name="pdot",
    )(n_chunks, clist, idx, x3, u_rows)


def _act_kernel(d_ref, g_ref, w_ref):
    w_ref[...] = g_ref[...] * jax.nn.gelu(d_ref[...])


def _peer_act(dots, gates):
    rows, lanes = dots.shape
    tm = min(4096, rows)
    spec = pl.BlockSpec((tm, lanes), lambda i: (i, 0))
    return pl.pallas_call(
        _act_kernel,
        grid=(rows // tm,),
        in_specs=[spec, spec],
        out_specs=spec,
        out_shape=jax.ShapeDtypeStruct((rows, lanes), F32),
        compiler_params=_cparams(("parallel",)),
        name="act",
    )(dots, gates)


def _pacc_kernel(nch_ref, cs_ref, idx_ref, w_ref, v_ref, o_ref, wrep_ref, *, slots):
    b = pl.program_id(0)
    i = pl.program_id(1)
    tm = o_ref.shape[0]

    n_groups = (nch_ref[b * pl.num_programs(1) + i] * CHUNK + LANES - 1) // LANES
    unroll = 4

    def expand_body(g4, carry):
        for j in range(unroll):
            g = g4 * unroll + j
            rows = jnp.broadcast_to(w_ref[pl.ds(g, 1), :], (LANES, LANES))
            wrep_ref[pl.ds(pl.multiple_of(g * LANES, LANES), LANES), :] = rows.T
        return carry

    assert w_ref.shape[0] % unroll == 0
    lax.fori_loop(0, (n_groups + unroll - 1) // unroll, expand_body, 0)

    def token_body(tl, q):
        c0 = cs_ref[tl * CS_STRIDE + b]
        c1 = cs_ref[tl * CS_STRIDE + b + 1]
        base = tl * slots

        def chunk(c, carry):
            q, accs = carry
            accs = list(accs)
            idx_c = idx_ref.at[pl.ds(base + c * CHUNK, CHUNK)]
            for k in range(CHUNK):
                w = jnp.broadcast_to(wrep_ref[pl.ds(q * CHUNK + k, 1), :], (SUBLANES, LANES))
                lo, hi = _unpack_rows(v_ref[pl.ds(pl.multiple_of(idx_c[k], SUBLANES), SUBLANES), :])
                j = 2 * (k % 2)
                accs[j] = accs[j] + w * lo
                accs[j + 1] = accs[j + 1] + w * hi
            return q + 1, tuple(accs)

        zero = jnp.zeros((SUBLANES, LANES), F32)
        q, accs = lax.fori_loop(c0, c1, chunk, (q, (zero, zero, zero, zero)))
        o_ref[tl] = jnp.concatenate([accs[0] + accs[2], accs[1] + accs[3]], axis=0)
        return q

    lax.fori_loop(0, tm, token_body, 0)


def _peer_accumulate(n_chunks, cs, idx, w, v_rows, T, slots):
    nb = v_rows.shape[0] // (EXPERT_BLOCK * SUBLANES)
    tm = PEER_TM
    n_tiles = T // tm
    groups = tm * slots // LANES
    smem_tile = lambda n: pl.BlockSpec((tm * n,), lambda b, i: (i,), memory_space=pltpu.SMEM)
    return pl.pallas_call(
        functools.partial(_pacc_kernel, slots=slots),
        grid=(nb, n_tiles),
        in_specs=[
            pl.BlockSpec(memory_space=pltpu.SMEM),
            smem_tile(CS_STRIDE), smem_tile(slots),
            pl.BlockSpec((None, None, groups, LANES), lambda b, i: (b, i, 0, 0)),
            _expert_block_spec(),
        ],
        out_specs=pl.BlockSpec((None, tm, ROW_SUBLANES, LANES), lambda b, i: (b, i, 0, 0)),
        out_shape=jax.ShapeDtypeStruct((nb, T, ROW_SUBLANES, LANES), F32),
        scratch_shapes=[pltpu.VMEM((tm * slots, LANES), F32)],
        compiler_params=_cparams(("arbitrary", "arbitrary")),
        name="pacc",
    )(n_chunks, cs, idx, w, v_rows)


def _final_kernel(x1_ref, p_ref, g_ref, o_ref):
    acc = x1_ref[...]
    for b in range(p_ref.shape[0]):
        acc = acc + p_ref[b]
    o_ref[...] = _rms(acc, g_ref[...])


def _final(x1, peer_parts, g_final):
    T, D = x1.shape
    nb = peer_parts.shape[0]
    tm = 256
    return pl.pallas_call(
        _final_kernel,
        grid=(T // tm,),
        in_specs=[
            pl.BlockSpec((tm, D), lambda i: (i, 0)),
            pl.BlockSpec((nb, tm, D), lambda i: (0, i, 0)),
            pl.BlockSpec((1, D), lambda i: (0, 0)),
        ],
        out_specs=pl.BlockSpec((tm, D), lambda i: (i, 0)),
        out_shape=jax.ShapeDtypeStruct((T, D), F32),
        compiler_params=_cparams(("parallel",)),
        name="final",
    )(x1, peer_parts, g_final)


def _rope_tables(seq):
    half = HEAD_DIM // 2
    inv = ROPE_THETA ** (-jnp.arange(half, dtype=F32) / half)
    ang = jnp.arange(seq).astype(F32)[:, None] * inv[None, :]
    cos, sin = jnp.cos(ang), jnp.sin(ang)
    reps = LANES // HEAD_DIM
    cos_t = jnp.concatenate([cos, cos] * reps, axis=1)
    sin_t = jnp.concatenate([-sin, sin] * reps, axis=1)
    return cos_t, sin_t


def _layer(xt, batch, seq, g_mix, w_in, sink, conv_w, conv_b, fwd_wa, fwd_ba, fwd_wx, fwd_bx, fwd_lam,
           bwd_wa, bwd_ba, bwd_wx, bwd_bx, bwd_lam, g_attn_out, g_lru_out, w_out, g_ffn,
           w_pq, sub_k1, sub_k2, u_emb, v_emb):
    T, D = xt.shape
    cos_t, sin_t = _rope_tables(seq)
    qkv, gr = _inproj(xt, g_mix.reshape(1, D), w_in.astype(BF16), cos_t, sin_t, seq)
    y_attn = _attention(qkv, sink, batch, seq)
    y_lru = _lru(gr, conv_w, conv_b, (fwd_wa, fwd_wx, bwd_wa, bwd_wx), (fwd_ba, fwd_bx, bwd_ba, bwd_bx),
                 (fwd_lam, bwd_lam), batch, seq)
    x1, xn = _outproj(y_attn, y_lru, xt, g_attn_out.reshape(1, -1), g_lru_out.reshape(1, -1),
                      w_out.astype(BF16), g_ffn.reshape(1, D))

    keys = jnp.concatenate([sub_k1, sub_k2], axis=0).astype(BF16)
    scores = _peer_scores(xn, w_pq.astype(BF16), keys)
    gates_t, eidx_t = _peer_topk(scores)
    n_pairs = PEER_HEADS * PEER_TOPK
    gates = gates_t.transpose(2, 0, 1).reshape(T, n_pairs)
    eidx = eidx_t.transpose(2, 0, 1).reshape(T, n_pairs)
    n_exp = u_emb.shape[0]
    nb = n_exp // EXPERT_BLOCK
    slots = n_pairs + nb * CHUNK
    idx, gates_c, cs, n_chunks, clist = _route(eidx, gates, nb)

    x3 = xn.reshape(T, ROW_SUBLANES, LANES)
    u_rows = _pack_rows(u_emb)
    v_rows = _pack_rows(v_emb)
    dots = _peer_dots(n_chunks, clist, idx, x3, u_rows, slots)
    w = _peer_act(dots.reshape(-1, LANES), gates_c.reshape(-1, LANES)).reshape(dots.shape)
    parts = _peer_accumulate(n_chunks, cs, idx, w, v_rows, T, slots)
    return x1, parts.reshape(nb, T, D)


def kernel(x, g_mix, w_in, sink, conv_w, conv_b, fwd_wa, fwd_ba, fwd_wx, fwd_bx, fwd_lam, bwd_wa, bwd_ba, bwd_wx, bwd_bx, bwd_lam, g_attn_out, g_lru_out, w_out, g_ffn, w_pq, sub_k1, sub_k2, u_emb, v_emb, g_final):
    B, S, D = x.shape
    assert g_mix.shape[0] == 1, "single-layer trunk"
    xt = x.reshape(B * S, D)
    x1, parts = _layer(xt, B, S, g_mix[0], w_in[0], sink[0], conv_w[0], conv_b[0],
                       fwd_wa[0], fwd_ba[0], fwd_wx[0], fwd_bx[0], fwd_lam[0],
                       bwd_wa[0], bwd_ba[0], bwd_wx[0], bwd_bx[0], bwd_lam[0],
                       g_attn_out[0], g_lru_out[0], w_out[0], g_ffn[0],
                       w_pq[0], sub_k1[0], sub_k2[0], u_emb[0], v_emb[0])
    return _final(x1, parts, g_final.reshape(1, D)).reshape(B, S, D)
```

```python
import functools

import jax
import jax.numpy as jnp
from jax import lax
from jax.experimental import pallas as pl
from jax.experimental.pallas import tpu as pltpu

F32 = jnp.float32
BF16 = jnp.bfloat16

HEAD_DIM = 64
N_KV_HEADS = 4
Q_PER_KV = 4
N_Q_HEADS = N_KV_HEADS * Q_PER_KV
WINDOW = 128
ATTN_BLOCK = 128
ROPE_THETA = 10000.0
LRU_BLOCK = 64
CONV_WIDTH = 4
CONV_PAD_LEFT = 2
LRU_C = 8.0
PEER_HEADS = 8
PEER_HALF = 128
N_KEYS = 128
PEER_TOPK = 16
EPS = 1e-6
NEG = -1e30

LANES = 128
SUBLANES = 8
VMEM_LIMIT = 56 * 1024 * 1024

INPROJ_TM = 512
INPROJ_TN = 512
LRU_CHANNELS = 256
LRU_CHUNK = 256
OUTPROJ_TM = 256
PQ_TM = 256
TOPK_TL = 512
EXPERT_BLOCK = 8192
PEER_TM = 128
CHUNK = 16
CS_STRIDE = 8
GATE_BITS = 18
GATE_SHIFT = 31 - GATE_BITS


def _cparams(sem):
    return pltpu.CompilerParams(dimension_semantics=sem, vmem_limit_bytes=VMEM_LIMIT)


def _rms(x, g):
    return x * lax.rsqrt(jnp.mean(x * x, axis=-1, keepdims=True) + EPS) * g


def _inproj_kernel(x_ref, g_ref, w_ref, cos_ref, sin_ref, qkv_ref, gr_ref, h_ref, *, n_rope_tiles, kv_cols):
    j = pl.program_id(1)

    @pl.when(j == 0)
    def _():
        h_ref[...] = _rms(x_ref[...], g_ref[...]).astype(BF16)

    acc = jnp.dot(h_ref[...], w_ref[...], preferred_element_type=F32)
    tm, tn = acc.shape

    @pl.when(j < n_rope_tiles)
    def _():
        reps = tn // LANES
        cos = jnp.concatenate([cos_ref[...]] * reps, axis=1)
        sin = jnp.concatenate([sin_ref[...]] * reps, axis=1)
        lane = lax.broadcasted_iota(jnp.int32, (tm, tn), 1)
        first = (lane % HEAD_DIM) < (HEAD_DIM // 2)
        partner = jnp.where(first, pltpu.roll(acc, tn - HEAD_DIM // 2, 1), pltpu.roll(acc, HEAD_DIM // 2, 1))
        roped = acc * cos + partner * sin
        is_rope = jnp.logical_or(j < n_rope_tiles - 1, lane < kv_cols)
        qkv_ref[...] = jnp.where(is_rope, roped, acc).astype(BF16)

    @pl.when(j >= n_rope_tiles)
    def _():
        gr_ref[...] = acc


def _inproj(xt, g_mix, w_in_bf, cos_t, sin_t, seq):
    T, D = xt.shape
    n_cols = w_in_bf.shape[1]
    q_cols = N_Q_HEADS * HEAD_DIM
    kv_cols = N_KV_HEADS * HEAD_DIM
    qkv_cols = q_cols + 2 * kv_cols
    tm, tn = INPROJ_TM, INPROJ_TN
    n_rope_tiles = qkv_cols // tn
    assert qkv_cols % tn == 0 and (q_cols % tn == 0) and seq % tm == 0 and T % tm == 0
    pos_blocks = seq // tm
    kern = functools.partial(_inproj_kernel, n_rope_tiles=n_rope_tiles, kv_cols=kv_cols)
    return pl.pallas_call(
        kern,
        grid=(T // tm, n_cols // tn),
        in_specs=[
            pl.BlockSpec((tm, D), lambda i, j: (i, 0)),
            pl.BlockSpec((1, D), lambda i, j: (0, 0)),
            pl.BlockSpec((D, tn), lambda i, j: (0, j)),
            pl.BlockSpec((tm, LANES), lambda i, j: (i % pos_blocks, 0)),
            pl.BlockSpec((tm, LANES), lambda i, j: (i % pos_blocks, 0)),
        ],
        out_specs=[
            pl.BlockSpec((tm, tn), lambda i, j: (i, jnp.minimum(j, n_rope_tiles - 1))),
            pl.BlockSpec((tm, tn), lambda i, j: (i, jnp.maximum(j - n_rope_tiles, 0))),
        ],
        out_shape=[
            jax.ShapeDtypeStruct((T, qkv_cols), BF16),
            jax.ShapeDtypeStruct((T, n_cols - qkv_cols), F32),
        ],
        scratch_shapes=[pltpu.VMEM((tm, D), BF16)],
        compiler_params=_cparams(("parallel", "arbitrary")),
        name="inproj",
    )(xt, g_mix, w_in_bf, cos_t, sin_t)


def _attn_kernel(sink_ref, q_ref, kp_ref, kc_ref, kn_ref, vp_ref, vc_ref, vn_ref, o_ref, *, seq):
    n = pl.program_id(1)
    q = q_ref[...]
    k = jnp.concatenate([kp_ref[...], kc_ref[...], kn_ref[...]], axis=0)
    v = jnp.concatenate([vp_ref[...], vc_ref[...], vn_ref[...]], axis=0)
    nq, nk = ATTN_BLOCK, 3 * ATTN_BLOCK
    qpos = n * ATTN_BLOCK + lax.broadcasted_iota(jnp.int32, (nq, nk), 0)
    kpos = (n - 1) * ATTN_BLOCK + lax.broadcasted_iota(jnp.int32, (nq, nk), 1)
    valid = (jnp.abs(kpos - qpos) <= WINDOW) & (kpos >= 0) & (kpos < seq)
    scale = HEAD_DIM ** -0.5
    outs = []
    for g in range(N_KV_HEADS):
        kg = k[:, g * HEAD_DIM:(g + 1) * HEAD_DIM]
        vg = v[:, g * HEAD_DIM:(g + 1) * HEAD_DIM]
        for r in range(Q_PER_KV):
            h = g * Q_PER_KV + r
            qh = q[:, h * HEAD_DIM:(h + 1) * HEAD_DIM]
            s = lax.dot_general(qh, kg, (((1,), (1,)), ((), ())), preferred_element_type=F32) * scale
            s = jnp.where(valid, s, NEG)
            sk = sink_ref[h]
            m = jnp.maximum(jnp.max(s, axis=-1, keepdims=True), sk)
            p = jnp.exp(s - m)
            den = jnp.sum(p, axis=-1, keepdims=True) + jnp.exp(sk - m)
            o = jnp.dot(p.astype(BF16), vg, preferred_element_type=F32) / den
            outs.append(o)
    o_ref[...] = jnp.concatenate(outs, axis=1)


def _attention(qkv, sink, batch, seq):
    T = qkv.shape[0]
    nb = seq // ATTN_BLOCK
    q_cols = N_Q_HEADS * HEAD_DIM
    kv_cols = N_KV_HEADS * HEAD_DIM
    k_blk = q_cols // kv_cols
    v_blk = k_blk + 1
    blk = ATTN_BLOCK

    def row(b, n):
        return b * nb + n

    def kv_spec(col_blk, shift):
        return pl.BlockSpec((blk, kv_cols), lambda b, n: (row(b, jnp.clip(n + shift, 0, nb - 1)), col_blk))

    return pl.pallas_call(
        functools.partial(_attn_kernel, seq=seq),
        grid=(batch, nb),
        in_specs=[
            pl.BlockSpec(memory_space=pltpu.SMEM),
            pl.BlockSpec((blk, q_cols), lambda b, n: (row(b, n), 0)),
            kv_spec(k_blk, -1), kv_spec(k_blk, 0), kv_spec(k_blk, 1),
            kv_spec(v_blk, -1), kv_spec(v_blk, 0), kv_spec(v_blk, 1),
        ],
        out_specs=pl.BlockSpec((blk, q_cols), lambda b, n: (row(b, n), 0)),
        out_shape=jax.ShapeDtypeStruct((T, q_cols), F32),
        compiler_params=_cparams(("parallel", "parallel")),
        name="attn",
    )(sink, qkv, qkv, qkv, qkv, qkv, qkv, qkv)


def _scan_chunk(a, u, reverse):
    L, C = a.shape
    row = lax.broadcasted_iota(jnp.int32, (L, C), 0)
    A, H = a, u
    d = 1
    while d < L:
        if d < SUBLANES:
            if reverse:
                keep = row < L - d
                As = jnp.where(keep, pltpu.roll(A, L - d, 0), 1.0)
                Hs = jnp.where(keep, pltpu.roll(H, L - d, 0), 0.0)
            else:
                keep = row >= d
                As = jnp.where(keep, pltpu.roll(A, d, 0), 1.0)
                Hs = jnp.where(keep, pltpu.roll(H, d, 0), 0.0)
        else:
            one = jnp.ones((d, C), F32)
            zero = jnp.zeros((d, C), F32)
            if reverse:
                As = jnp.concatenate([A[d:], one], axis=0)
                Hs = jnp.concatenate([H[d:], zero], axis=0)
            else:
                As = jnp.concatenate([one, A[:L - d]], axis=0)
                Hs = jnp.concatenate([zero, H[:L - d]], axis=0)
        H = A * Hs + H
        A = A * As
        d *= 2
    return A, H


def _lru_kernel(xg_ref, xr_ref, cw_ref, cb_ref, wg_ref, bg_ref, lam_ref, y_ref, xp_ref, hf_ref):
    S, C = xr_ref.shape
    L = LRU_CHUNK
    n_chunks = S // L
    halo = SUBLANES

    xp_ref[0:halo, :] = jnp.zeros((halo, C), F32)
    xp_ref[S + halo:S + 2 * halo, :] = jnp.zeros((halo, C), F32)

    def copy_body(ci, carry):
        t0 = pl.multiple_of(ci * L, L)
        xp_ref[pl.ds(t0 + halo, L), :] = xr_ref[pl.ds(t0, L), :]
        return carry

    lax.fori_loop(0, n_chunks, copy_body, 0)

    cw = cw_ref[...]
    cb = cb_ref[...]
    bias = bg_ref[0]
    neg_c_softplus = -LRU_C * jax.nn.softplus(-lam_ref[0])

    def conv_chunk(t0):
        win = xp_ref[pl.ds(t0, L + 2 * halo), :]
        acc = cb
        for j in range(CONV_WIDTH):
            off = halo - CONV_PAD_LEFT + j
            acc = acc + cw[j:j + 1, :] * win[off:off + L, :]
        return acc

    def gate_au(c, direction):
        w = wg_ref[0, :, direction * 2 * C:(direction + 1) * 2 * C]
        z = jnp.dot(c.astype(BF16), w, preferred_element_type=F32) + bias[:, direction * 2 * C:(direction + 1) * 2 * C]
        r = jax.nn.sigmoid(z[:, :C])
        i = jax.nn.sigmoid(z[:, C:])
        log_a = r * neg_c_softplus[direction:direction + 1, :]
        a = jnp.exp(log_a)
        u = jnp.sqrt(1.0 - a * a) * (i * c)
        return a, u

    def fwd_body(ci, carry):
        t0 = pl.multiple_of(ci * L, L)
        c = conv_chunk(t0)
        a, u = gate_au(c, 0)
        A, H = _scan_chunk(a, u, reverse=False)
        h = H + A * carry
        hf_ref[pl.ds(t0, L), :] = h
        return h[L - 1:L, :]

    lax.fori_loop(0, n_chunks, fwd_body, jnp.zeros((1, C), F32))

    def bwd_body(k, carry):
        t0 = pl.multiple_of((n_chunks - 1 - k) * L, L)
        c = conv_chunk(t0)
        a, u = gate_au(c, 1)
        A, H = _scan_chunk(a, u, reverse=True)
        h = H + A * carry
        y_ref[pl.ds(t0, L), :] = jax.nn.gelu(xg_ref[pl.ds(t0, L), :]) * (hf_ref[pl.ds(t0, L), :] + h)
        return h[0:1, :]

    lax.fori_loop(0, n_chunks, bwd_body, jnp.zeros((1, C), F32))


def _block_diag_chunks(w, per_chunk):
    nblk, b, _ = w.shape
    w4 = w.reshape(nblk // per_chunk, per_chunk, b, b)
    eye = jnp.eye(per_chunk, dtype=w.dtype)
    m = w4[:, :, :, None, :] * eye[None, :, None, :, None]
    return m.reshape(nblk // per_chunk, per_chunk * b, per_chunk * b)


def _lru(gr, conv_w, conv_b, gate_ws, gate_bs, lams, batch, seq):
    T, two_w = gr.shape
    W = two_w // 2
    C = LRU_CHANNELS
    n_ch = W // C
    per_chunk = C // LRU_BLOCK
    wg = jnp.concatenate([_block_diag_chunks(w, per_chunk) for w in gate_ws], axis=-1).astype(BF16)
    bg = jnp.concatenate([b.reshape(n_ch, 1, C) for b in gate_bs], axis=-1)
    lam = jnp.stack([l.reshape(n_ch, C) for l in lams], axis=1)
    return pl.pallas_call(
        _lru_kernel,
        grid=(batch, n_ch),
        in_specs=[
            pl.BlockSpec((seq, C), lambda b, c: (b, c)),
            pl.BlockSpec((seq, C), lambda b, c: (b, n_ch + c)),
            pl.BlockSpec((CONV_WIDTH, C), lambda b, c: (0, c)),
            pl.BlockSpec((1, C), lambda b, c: (0, c)),
            pl.BlockSpec((1, C, 4 * C), lambda b, c: (c, 0, 0)),
            pl.BlockSpec((1, 1, 4 * C), lambda b, c: (c, 0, 0)),
            pl.BlockSpec((1, 2, C), lambda b, c: (c, 0, 0)),
        ],
        out_specs=pl.BlockSpec((seq, C), lambda b, c: (b, c)),
        out_shape=jax.ShapeDtypeStruct((T, W), F32),
        scratch_shapes=[pltpu.VMEM((seq + 2 * SUBLANES, C), F32), pltpu.VMEM((seq, C), F32)],
        compiler_params=_cparams(("parallel", "parallel")),
        name="lru",
    )(gr, gr, conv_w, conv_b.reshape(1, W), wg, bg, lam)


def _outproj_kernel(ya_ref, yl_ref, x_ref, ga_ref, gl_ref, w_ref, gf_ref, x1_ref, xn_ref, xs_ref):
    y = jnp.concatenate([_rms(ya_ref[...], ga_ref[...]), _rms(yl_ref[...], gl_ref[...])], axis=1).astype(BF16)
    x1 = x_ref[...] + jnp.dot(y, w_ref[...], preferred_element_type=F32)
    x1_ref[...] = x1
    xn = _rms(x1, gf_ref[...])
    xn_ref[...] = xn.astype(BF16)
    for s in range(xs_ref.shape[1]):
        xs_ref[:, s, :] = xn[:, s * LANES:(s + 1) * LANES]


def _outproj(y_attn, y_lru, xt, g_attn, g_lru, w_out_bf, g_ffn):
    T, D = xt.shape
    wa, wl = y_attn.shape[1], y_lru.shape[1]
    tm = OUTPROJ_TM
    return pl.pallas_call(
        _outproj_kernel,
        grid=(T // tm,),
        in_specs=[
            pl.BlockSpec((tm, wa), lambda i: (i, 0)),
            pl.BlockSpec((tm, wl), lambda i: (i, 0)),
            pl.BlockSpec((tm, D), lambda i: (i, 0)),
            pl.BlockSpec((1, wa), lambda i: (0, 0)),
            pl.BlockSpec((1, wl), lambda i: (0, 0)),
            pl.BlockSpec((wa + wl, D), lambda i: (0, 0)),
            pl.BlockSpec((1, D), lambda i: (0, 0)),
        ],
        out_specs=[pl.BlockSpec((tm, D), lambda i: (i, 0)), pl.BlockSpec((tm, D), lambda i: (i, 0)),
                   pl.BlockSpec((tm, D // LANES, LANES), lambda i: (i, 0, 0))],
        out_shape=[jax.ShapeDtypeStruct((T, D), F32), jax.ShapeDtypeStruct((T, D), BF16),
                   jax.ShapeDtypeStruct((T, D // LANES, LANES), F32)],
        compiler_params=_cparams(("parallel",)),
        name="outproj",
    )(y_attn, y_lru, xt, g_attn, g_lru, w_out_bf, g_ffn)


def _pq_kernel(xn_ref, w_ref, k_ref, s_ref):
    q = jnp.dot(xn_ref[...], w_ref[...], preferred_element_type=F32).astype(BF16)
    for hh in range(2 * PEER_HEADS):
        half, head = divmod(hh, PEER_HEADS)
        col = (head * 2 + half) * PEER_HALF
        qh = q[:, col:col + PEER_HALF]
        s_ref[hh] = lax.dot_general(k_ref[hh], qh, (((1,), (1,)), ((), ())), preferred_element_type=F32)


def _peer_scores(xn, w_pq_bf, keys_bf):
    T, D = xn.shape
    tm = PQ_TM
    nh = keys_bf.shape[0]
    return pl.pallas_call(
        _pq_kernel,
        grid=(T // tm,),
        in_specs=[
            pl.BlockSpec((tm, D), lambda i: (i, 0)),
            pl.BlockSpec(w_pq_bf.shape, lambda i: (0, 0)),
            pl.BlockSpec(keys_bf.shape, lambda i: (0, 0, 0)),
        ],
        out_specs=pl.BlockSpec((nh, N_KEYS, tm), lambda i: (0, 0, i)),
        out_shape=jax.ShapeDtypeStruct((nh, N_KEYS, T), F32),
        compiler_params=_cparams(("parallel",)),
        name="pq",
    )(xn, w_pq_bf, keys_bf)


def _top16_rows(s, payload=None):
    n, tl = s.shape
    row = lax.broadcasted_iota(jnp.int32, (n, tl), 0)
    vals, idxs, pays = [], [], []
    for _ in range(PEER_TOPK):
        m = jnp.max(s, axis=0, keepdims=True)
        idx = jnp.min(jnp.where(s == m, row, n), axis=0, keepdims=True)
        hit = row == idx
        vals.append(m)
        idxs.append(idx)
        if payload is not None:
            pays.append(jnp.max(jnp.where(hit, payload, -1), axis=0, keepdims=True))
        s = jnp.where(hit, -jnp.inf, s)
    cat = lambda xs: jnp.concatenate(xs, axis=0)
    return cat(vals), cat(idxs), (cat(pays) if payload is not None else None)


def _topk_kernel(s1_ref, s2_ref, g_ref, e_ref):
    v1, i1, _ = _top16_rows(s1_ref[0])
    v2, i2, _ = _top16_rows(s2_ref[0])
    K = PEER_TOPK
    tl = v1.shape[1]
    sub = lax.broadcasted_iota(jnp.int32, (SUBLANES, tl), 0)
    cand, cidx = [], []
    for a in range(K // 2):
        n_b = K // (a + 1)
        for b0 in range(0, n_b, SUBLANES):
            keep = sub < (n_b - b0)
            cand.append(jnp.where(keep, v1[a:a + 1, :] + v2[b0:b0 + SUBLANES, :], -jnp.inf))
            cidx.append(i1[a:a + 1, :] * N_KEYS + i2[b0:b0 + SUBLANES, :])
    cand.append(v1[K // 2:, :] + v2[0:1, :])
    cidx.append(i1[K // 2:, :] * N_KEYS + i2[0:1, :])
    sc, _, eidx = _top16_rows(jnp.concatenate(cand, axis=0), jnp.concatenate(cidx, axis=0))
    ex = jnp.exp(sc - sc[0:1, :])
    g_ref[0] = ex / jnp.sum(ex, axis=0, keepdims=True)
    e_ref[0] = eidx


def _peer_topk(scores):
    nh2, nk, T = scores.shape
    nh = nh2 // 2
    tl = TOPK_TL
    return pl.pallas_call(
        _topk_kernel,
        grid=(nh, T // tl),
        in_specs=[
            pl.BlockSpec((1, nk, tl), lambda h, i: (h, 0, i)),
            pl.BlockSpec((1, nk, tl), lambda h, i: (nh + h, 0, i)),
        ],
        out_specs=[
            pl.BlockSpec((1, PEER_TOPK, tl), lambda h, i: (h, 0, i)),
            pl.BlockSpec((1, PEER_TOPK, tl), lambda h, i: (h, 0, i)),
        ],
        out_shape=[
            jax.ShapeDtypeStruct((nh, PEER_TOPK, T), F32),
            jax.ShapeDtypeStruct((nh, PEER_TOPK, T), jnp.int32),
        ],
        compiler_params=_cparams(("parallel", "parallel")),
        name="topk",
    )(scores, scores)


ROW_SUBLANES = 16


def _pack_rows(emb):
    n, d = emb.shape
    half = d // 2
    assert half == SUBLANES * LANES
    bits = lax.bitcast_convert_type(emb.astype(BF16), jnp.uint16).astype(jnp.uint32)
    return (bits[:, :half] | (bits[:, half:] << 16)).reshape(n * SUBLANES, LANES)


def _unpack_rows(word):
    return pltpu.bitcast(word << 16, F32), pltpu.bitcast(word & jnp.uint32(0xFFFF0000), F32)


def _sublane_fold(parts):
    sub = lax.broadcasted_iota(jnp.int32, (SUBLANES, LANES), 0)
    step = 1
    while len(parts) > 1:
        low = (sub & step) == 0
        parts = [jnp.where(low, a, b) + pltpu.roll(jnp.where(low, b, a), step, 0)
                 for a, b in zip(parts[0::2], parts[1::2])]
        step *= 2
    return parts[0]


def _chunk_list_stride(chunks_per_token):
    return -(-PEER_TM * chunks_per_token // 1024) * 1024


def _route(eidx, gates, n_blocks):
    T, P = eidx.shape
    u32 = jnp.uint32
    blocks = jnp.arange(n_blocks, dtype=jnp.int32)
    blk = eidx // EXPERT_BLOCK
    cnt = jnp.sum((blk[:, :, None] == blocks[None, None, :]).astype(jnp.int32), axis=1)
    pcnt = (cnt + CHUNK - 1) // CHUNK * CHUNK
    cand = jnp.arange(CHUNK, dtype=jnp.int32)
    active = cand[None, None, :] < (pcnt - cnt)[:, :, None]
    gate_bits = (lax.bitcast_convert_type(gates, u32) + u32(1 << (GATE_SHIFT - 1))) >> GATE_SHIFT
    pad_mark = u32((1 << GATE_BITS) - 1)
    word = (eidx.astype(u32) << GATE_BITS) | gate_bits
    last_row = ((blocks + 1) * EXPERT_BLOCK - 1).astype(u32)
    pad_word = jnp.where(active, ((last_row << GATE_BITS) | pad_mark)[None, :, None], u32(0xFFFFFFFF))
    srt = lax.sort(jnp.concatenate([word, pad_word.reshape(T, n_blocks * CHUNK)], axis=1), dimension=1)
    is_pad = (srt & pad_mark) == pad_mark
    loc = jnp.where(is_pad, u32(0), ((srt >> GATE_BITS) & u32(EXPERT_BLOCK - 1)) * SUBLANES)
    gate = lax.bitcast_convert_type(jnp.where(is_pad, u32(0), (srt & pad_mark) << GATE_SHIFT), F32)
    ends = jnp.cumsum(pcnt, axis=1) // CHUNK
    cs = jnp.concatenate([jnp.zeros((T, 1), jnp.int32), ends,
                          jnp.zeros((T, CS_STRIDE - n_blocks - 1), jnp.int32)], axis=1)
    n_chunks = jnp.sum((ends - cs[:, :n_blocks]).reshape(T // PEER_TM, PEER_TM, n_blocks), axis=1)

    per_tok = (P + n_blocks * CHUNK) // CHUNK
    slot_j = jnp.arange(per_tok, dtype=jnp.int32)
    slot_blk = jnp.sum((slot_j[None, :, None] >= ends[:, None, :]).astype(jnp.int32), axis=2)
    tok = (jnp.arange(T, dtype=jnp.int32) % PEER_TM)[:, None]
    entry = (tok << 16) | (tok * (per_tok * CHUNK) + slot_j[None, :] * CHUNK)
    assert PEER_TM <= 1 << 8
    order = lax.sort(((slot_blk << 24) | entry).reshape(T // PEER_TM, PEER_TM * per_tok), dimension=1)
    clist = order & ((1 << 24) - 1)
    clist = jnp.pad(clist, ((0, 0), (0, _chunk_list_stride(per_tok) - PEER_TM * per_tok)))

    n_tiles, per_tile = T // PEER_TM, PEER_TM * per_tok
    run = (ends - cs[:, :n_blocks]).reshape(n_tiles, PEER_TM, n_blocks)
    before = (jnp.cumsum(run, axis=1) - run).reshape(T, n_blocks)
    gate_rows = gate.reshape(n_tiles, per_tile, CHUNK)
    rank_q = jnp.arange(per_tile, dtype=jnp.int32)
    gates_c = []
    for b in range(n_blocks):
        rank = jnp.where(slot_blk == b, before[:, b:b + 1] + slot_j[None, :] - cs[:, b:b + 1], -1)
        onehot = (rank.reshape(n_tiles, 1, per_tile) == rank_q[None, :, None]).astype(F32)
        gates_c.append(jnp.einsum('tqs,tsk->tqk', onehot, gate_rows, precision=lax.Precision.HIGHEST))
    gates_c = jnp.stack(gates_c).reshape(n_blocks, n_tiles, per_tile * CHUNK // LANES, LANES)
    return (loc.astype(jnp.int32).reshape(-1), gates_c, cs.reshape(-1), n_chunks.T.reshape(-1), clist.reshape(-1))


def _pdot_kernel(nch_ref, cl_ref, idx_ref, x_ref, u_ref, d_ref, rbuf_ref, *, n_blocks):
    b = pl.program_id(0)
    i = pl.program_id(1)
    n_tiles = pl.num_programs(1)
    n_chunks = nch_ref[b * n_tiles + i]
    first = jnp.int32(0)
    for bb in range(n_blocks - 1):
        first = first + jnp.where(bb < b, nch_ref[bb * n_tiles + i], 0)
    zero_rows = jnp.zeros((SUBLANES, LANES), F32)

    def chunk(q, carry):
        entry = cl_ref[first + q]
        base = entry & 0xFFFF
        x = x_ref[entry >> 16]
        x_lo, x_hi = x[:SUBLANES], x[SUBLANES:]
        for h in range(CHUNK // SUBLANES):
            parts = []
            for k in range(SUBLANES):
                row = pl.multiple_of(idx_ref[base + h * SUBLANES + k], SUBLANES)
                lo, hi = _unpack_rows(u_ref[pl.ds(row, SUBLANES), :])
                parts.append(lo * x_lo + hi * x_hi)
            row0 = pl.multiple_of(q * CHUNK + h * SUBLANES, SUBLANES)
            rbuf_ref[pl.ds(row0, SUBLANES), :] = _sublane_fold(parts)
        return carry

    lax.fori_loop(0, n_chunks // 2, lambda p, c: chunk(2 * p + 1, chunk(2 * p, c)), 0)

    @pl.when(n_chunks % 2 == 1)
    def _():
        chunk(n_chunks - 1, 0)

    per_step = SUBLANES * LANES // CHUNK
    n_steps = (n_chunks + per_step - 1) // per_step
    folds_per_chunk = CHUNK // SUBLANES

    def zero_body(f, carry):
        rbuf_ref[pl.ds(pl.multiple_of(f * SUBLANES, SUBLANES), SUBLANES), :] = zero_rows
        return carry

    lax.fori_loop(n_chunks * folds_per_chunk, n_steps * per_step * folds_per_chunk, zero_body, 0)
    d_ref[...] = jnp.zeros(d_ref.shape, F32)

    def reduce_body(s, carry):
        dots = []
        for j in range(SUBLANES):
            rows = pl.ds(pl.multiple_of((s * SUBLANES + j) * LANES, LANES), LANES)
            dots.append(jnp.sum(rbuf_ref[rows, :].T, axis=0, keepdims=True))
        d_ref[pl.ds(pl.multiple_of(s * SUBLANES, SUBLANES), SUBLANES), :] = jnp.concatenate(dots, axis=0)
        return carry

    lax.fori_loop(0, n_steps, reduce_body, 0)


def _expert_block_spec():
    return pl.BlockSpec((EXPERT_BLOCK * SUBLANES, LANES), lambda b, i: (b, 0), pipeline_mode=pl.Buffered(1))


def _peer_dots(n_chunks, clist, idx, x3, u_rows, slots):
    T = x3.shape[0]
    nb = u_rows.shape[0] // (EXPERT_BLOCK * SUBLANES)
    tm = PEER_TM
    n_tiles = T // tm
    max_pairs = tm * slots
    groups = max_pairs // LANES
    smem_tile = lambda n: pl.BlockSpec((tm * n,), lambda b, i: (i,), memory_space=pltpu.SMEM)
    out = jax.ShapeDtypeStruct((nb, n_tiles, groups, LANES), F32)
    out_spec = pl.BlockSpec((None, None, groups, LANES), lambda b, i: (b, i, 0, 0))
    return pl.pallas_call(
        functools.partial(_pdot_kernel, n_blocks=nb),
        grid=(nb, n_tiles),
        in_specs=[
            pl.BlockSpec(memory_space=pltpu.SMEM),
            pl.BlockSpec((_chunk_list_stride(slots // CHUNK),), lambda b, i: (i,), memory_space=pltpu.SMEM),
            smem_tile(slots),
            pl.BlockSpec((tm, ROW_SUBLANES, LANES), lambda b, i: (i, 0, 0)),
            _expert_block_spec(),
        ],
        out_specs=out_spec,
        out_shape=out,
        scratch_shapes=[pltpu.VMEM((max_pairs, LANES), F32)],
        compiler_params=_cparams(("arbitrary", "arbitrary")),
        name="pdot",
    )(n_chunks, clist, idx, x3, u_rows)


def _act_kernel(d_ref, g_ref, w_ref):
    w_ref[...] = g_ref[...] * jax.nn.gelu(d_ref[...])


def _peer_act(dots, gates):
    rows, lanes = dots.shape
    tm = min(4096, rows)
    spec = pl.BlockSpec((tm, lanes), lambda i: (i, 0))
    return pl.pallas_call(
        _act_kernel,
        grid=(rows // tm,),
        in_specs=[spec, spec],
        out_specs=spec,
        out_shape=jax.ShapeDtypeStruct((rows, lanes), F32),
        compiler_params=_cparams(("parallel",)),
        name="act",
    )(dots, gates)


def _pacc_kernel(nch_ref, cl_ref, idx_ref, w_ref, v_ref, o_ref, wrep_ref, *, n_blocks):
    b = pl.program_id(0)
    i = pl.program_id(1)
    n_tiles = pl.num_programs(1)
    n_chunks = nch_ref[b * n_tiles + i]
    first = jnp.int32(0)
    for bb in range(n_blocks - 1):
        first = first + jnp.where(bb < b, nch_ref[bb * n_tiles + i], 0)

    n_groups = (n_chunks * CHUNK + LANES - 1) // LANES
    unroll = 4

    def expand_body(g4, carry):
        for j in range(unroll):
            g = g4 * unroll + j
            rows = jnp.broadcast_to(w_ref[pl.ds(g, 1), :], (LANES, LANES))
            wrep_ref[pl.ds(pl.multiple_of(g * LANES, LANES), LANES), :] = rows.T
        return carry

    assert w_ref.shape[0] % unroll == 0
    lax.fori_loop(0, (n_groups + unroll - 1) // unroll, expand_body, 0)

    o_ref[...] = jnp.zeros(o_ref.shape, F32)

    def chunk(q, carry):
        entry = cl_ref[first + q]
        base = entry & 0xFFFF
        tok = entry >> 16
        zero = jnp.zeros((SUBLANES, LANES), F32)
        accs = [zero, zero, zero, zero]
        for k in range(CHUNK):
            w = jnp.broadcast_to(wrep_ref[pl.ds(q * CHUNK + k, 1), :], (SUBLANES, LANES))
            lo, hi = _unpack_rows(v_ref[pl.ds(pl.multiple_of(idx_ref[base + k], SUBLANES), SUBLANES), :])
            j = 2 * (k % 2)
            accs[j] = accs[j] + w * lo
            accs[j + 1] = accs[j + 1] + w * hi
        o_ref[tok] = o_ref[tok] + jnp.concatenate([accs[0] + accs[2], accs[1] + accs[3]], axis=0)
        return carry

    lax.fori_loop(0, n_chunks // 2, lambda p, c: chunk(2 * p + 1, chunk(2 * p, c)), 0)

    @pl.when(n_chunks % 2 == 1)
    def _():
        chunk(n_chunks - 1, 0)


def _peer_accumulate(n_chunks, clist, idx, w, v_rows, T, slots):
    nb = v_rows.shape[0] // (EXPERT_BLOCK * SUBLANES)
    tm = PEER_TM
    n_tiles = T // tm
    groups = tm * slots // LANES
    smem_tile = lambda n: pl.BlockSpec((tm * n,), lambda b, i: (i,), memory_space=pltpu.SMEM)
    return pl.pallas_call(
        functools.partial(_pacc_kernel, n_blocks=nb),
        grid=(nb, n_tiles),
        in_specs=[
            pl.BlockSpec(memory_space=pltpu.SMEM),
            pl.BlockSpec((_chunk_list_stride(slots // CHUNK),), lambda b, i: (i,), memory_space=pltpu.SMEM),
            smem_tile(slots),
            pl.BlockSpec((None, None, groups, LANES), lambda b, i: (b, i, 0, 0)),
            _expert_block_spec(),
        ],
        out_specs=pl.BlockSpec((None, tm, ROW_SUBLANES, LANES), lambda b, i: (b, i, 0, 0)),
        out_shape=jax.ShapeDtypeStruct((nb, T, ROW_SUBLANES, LANES), F32),
        scratch_shapes=[pltpu.VMEM((tm * slots, LANES), F32)],
        compiler_params=_cparams(("arbitrary", "arbitrary")),
        name="pacc",
    )(n_chunks, clist, idx, w, v_rows)


def _final_kernel(x1_ref, p_ref, g_ref, o_ref):
    nb, _, n_s, _ = p_ref.shape
    cols = []
    for s in range(n_s):
        col = x1_ref[:, s * LANES:(s + 1) * LANES]
        for b in range(nb):
            col = col + p_ref[b, :, s, :]
        cols.append(col)
    o_ref[...] = _rms(jnp.concatenate(cols, axis=1), g_ref[...])


def _final(x1, peer_parts, g_final):
    T, D = x1.shape
    nb = peer_parts.shape[0]
    tm = 256
    return pl.pallas_call(
        _final_kernel,
        grid=(T // tm,),
        in_specs=[
            pl.BlockSpec((tm, D), lambda i: (i, 0)),
            pl.BlockSpec((nb, tm, D // LANES, LANES), lambda i: (0, i, 0, 0)),
            pl.BlockSpec((1, D), lambda i: (0, 0)),
        ],
        out_specs=pl.BlockSpec((tm, D), lambda i: (i, 0)),
        out_shape=jax.ShapeDtypeStruct((T, D), F32),
        compiler_params=_cparams(("parallel",)),
        name="final",
    )(x1, peer_parts, g_final)


def _rope_tables(seq):
    half = HEAD_DIM // 2
    inv = ROPE_THETA ** (-jnp.arange(half, dtype=F32) / half)
    ang = jnp.arange(seq).astype(F32)[:, None] * inv[None, :]
    cos, sin = jnp.cos(ang), jnp.sin(ang)
    reps = LANES // HEAD_DIM
    cos_t = jnp.concatenate([cos, cos] * reps, axis=1)
    sin_t = jnp.concatenate([-sin, sin] * reps, axis=1)
    return cos_t, sin_t


def _layer(xt, batch, seq, g_mix, w_in, sink, conv_w, conv_b, fwd_wa, fwd_ba, fwd_wx, fwd_bx, fwd_lam,
           bwd_wa, bwd_ba, bwd_wx, bwd_bx, bwd_lam, g_attn_out, g_lru_out, w_out, g_ffn,
           w_pq, sub_k1, sub_k2, u_emb, v_emb):
    T, D = xt.shape
    cos_t, sin_t = _rope_tables(seq)
    qkv, gr = _inproj(xt, g_mix.reshape(1, D), w_in.astype(BF16), cos_t, sin_t, seq)
    y_attn = _attention(qkv, sink, batch, seq)
    y_lru = _lru(gr, conv_w, conv_b, (fwd_wa, fwd_wx, bwd_wa, bwd_wx), (fwd_ba, fwd_bx, bwd_ba, bwd_bx),
                 (fwd_lam, bwd_lam), batch, seq)
    x1, xn, x3 = _outproj(y_attn, y_lru, xt, g_attn_out.reshape(1, -1), g_lru_out.reshape(1, -1),
                      w_out.astype(BF16), g_ffn.reshape(1, D))

    keys = jnp.concatenate([sub_k1, sub_k2], axis=0).astype(BF16)
    scores = _peer_scores(xn, w_pq.astype(BF16), keys)
    gates_t, eidx_t = _peer_topk(scores)
    n_pairs = PEER_HEADS * PEER_TOPK
    gates = gates_t.transpose(2, 0, 1).reshape(T, n_pairs)
    eidx = eidx_t.transpose(2, 0, 1).reshape(T, n_pairs)
    n_exp = u_emb.shape[0]
    nb = n_exp // EXPERT_BLOCK
    slots = n_pairs + nb * CHUNK
    idx, gates_c, cs, n_chunks, clist = _route(eidx, gates, nb)

    u_rows = _pack_rows(u_emb)
    v_rows = _pack_rows(v_emb)
    dots = _peer_dots(n_chunks, clist, idx, x3, u_rows, slots)
    w = _peer_act(dots.reshape(-1, LANES), gates_c.reshape(-1, LANES)).reshape(dots.shape)
    parts = _peer_accumulate(n_chunks, clist, idx, w, v_rows, T, slots)
    return x1, parts


def kernel(x, g_mix, w_in, sink, conv_w, conv_b, fwd_wa, fwd_ba, fwd_wx, fwd_bx, fwd_lam, bwd_wa, bwd_ba, bwd_wx, bwd_bx, bwd_lam, g_attn_out, g_lru_out, w_out, g_ffn, w_pq, sub_k1, sub_k2, u_emb, v_emb, g_final):
    B, S, D = x.shape
    assert g_mix.shape[0] == 1, "single-layer trunk"
    xt = x.reshape(B * S, D)
    x1, parts = _layer(xt, B, S, g_mix[0], w_in[0], sink[0], conv_w[0], conv_b[0],
                       fwd_wa[0], fwd_ba[0], fwd_wx[0], fwd_bx[0], fwd_lam[0],
                       bwd_wa[0], bwd_ba[0], bwd_wx[0], bwd_bx[0], bwd_lam[0],
                       g_attn_out[0], g_lru_out[0], w_out[0], g_ffn[0],
                       w_pq[0], sub_k1[0], sub_k2[0], u_emb[0], v_emb[0])
    return _final(x1, parts, g_final.reshape(1, D)).reshape(B, S, D)
```

```python
import functools

import jax
import jax.numpy as jnp
from jax import lax
from jax.experimental import pallas as pl
from jax.experimental.pallas import tpu as pltpu

F32 = jnp.float32
BF16 = jnp.bfloat16

HEAD_DIM = 64
N_KV_HEADS = 4
Q_PER_KV = 4
N_Q_HEADS = N_KV_HEADS * Q_PER_KV
WINDOW = 128
ATTN_BLOCK = 128
ROPE_THETA = 10000.0
LRU_BLOCK = 64
CONV_WIDTH = 4
CONV_PAD_LEFT = 2
LRU_C = 8.0
PEER_HEADS = 8
PEER_HALF = 128
N_KEYS = 128
PEER_TOPK = 16
EPS = 1e-6
NEG = -1e30

LANES = 128
SUBLANES = 8
VMEM_LIMIT = 56 * 1024 * 1024

INPROJ_TM = 512
INPROJ_TN = 512
LRU_CHANNELS = 256
LRU_CHUNK = 256
OUTPROJ_TM = 256
PQ_TM = 256
TOPK_TL = 512
EXPERT_BLOCK = 8192
PEER_TM = 128
CHUNK = 16
CS_STRIDE = 8
GATE_BITS = 18
GATE_SHIFT = 31 - GATE_BITS


def _cparams(sem):
    return pltpu.CompilerParams(dimension_semantics=sem, vmem_limit_bytes=VMEM_LIMIT)


def _rms(x, g):
    return x * lax.rsqrt(jnp.mean(x * x, axis=-1, keepdims=True) + EPS) * g


def _inproj_kernel(x_ref, g_ref, w_ref, cos_ref, sin_ref, qkv_ref, gr_ref, h_ref, *, n_rope_tiles, kv_cols):
    j = pl.program_id(1)

    @pl.when(j == 0)
    def _():
        h_ref[...] = _rms(x_ref[...], g_ref[...]).astype(BF16)

    acc = jnp.dot(h_ref[...], w_ref[...], preferred_element_type=F32)
    tm, tn = acc.shape

    @pl.when(j < n_rope_tiles)
    def _():
        reps = tn // LANES
        cos = jnp.concatenate([cos_ref[...]] * reps, axis=1)
        sin = jnp.concatenate([sin_ref[...]] * reps, axis=1)
        lane = lax.broadcasted_iota(jnp.int32, (tm, tn), 1)
        first = (lane % HEAD_DIM) < (HEAD_DIM // 2)
        partner = jnp.where(first, pltpu.roll(acc, tn - HEAD_DIM // 2, 1), pltpu.roll(acc, HEAD_DIM // 2, 1))
        roped = acc * cos + partner * sin
        is_rope = jnp.logical_or(j < n_rope_tiles - 1, lane < kv_cols)
        qkv_ref[...] = jnp.where(is_rope, roped, acc).astype(BF16)

    @pl.when(j >= n_rope_tiles)
    def _():
        gr_ref[...] = acc


def _inproj(xt, g_mix, w_in_bf, cos_t, sin_t, seq):
    T, D = xt.shape
    n_cols = w_in_bf.shape[1]
    q_cols = N_Q_HEADS * HEAD_DIM
    kv_cols = N_KV_HEADS * HEAD_DIM
    qkv_cols = q_cols + 2 * kv_cols
    tm, tn = INPROJ_TM, INPROJ_TN
    n_rope_tiles = qkv_cols // tn
    assert qkv_cols % tn == 0 and (q_cols % tn == 0) and seq % tm == 0 and T % tm == 0
    pos_blocks = seq // tm
    kern = functools.partial(_inproj_kernel, n_rope_tiles=n_rope_tiles, kv_cols=kv_cols)
    return pl.pallas_call(
        kern,
        grid=(T // tm, n_cols // tn),
        in_specs=[
            pl.BlockSpec((tm, D), lambda i, j: (i, 0)),
            pl.BlockSpec((1, D), lambda i, j: (0, 0)),
            pl.BlockSpec((D, tn), lambda i, j: (0, j)),
            pl.BlockSpec((tm, LANES), lambda i, j: (i % pos_blocks, 0)),
            pl.BlockSpec((tm, LANES), lambda i, j: (i % pos_blocks, 0)),
        ],
        out_specs=[
            pl.BlockSpec((tm, tn), lambda i, j: (i, jnp.minimum(j, n_rope_tiles - 1))),
            pl.BlockSpec((tm, tn), lambda i, j: (i, jnp.maximum(j - n_rope_tiles, 0))),
        ],
        out_shape=[
            jax.ShapeDtypeStruct((T, qkv_cols), BF16),
            jax.ShapeDtypeStruct((T, n_cols - qkv_cols), F32),
        ],
        scratch_shapes=[pltpu.VMEM((tm, D), BF16)],
        compiler_params=_cparams(("parallel", "arbitrary")),
        name="inproj",
    )(xt, g_mix, w_in_bf, cos_t, sin_t)


def _attn_kernel(sink_ref, q_ref, kp_ref, kc_ref, kn_ref, vp_ref, vc_ref, vn_ref, o_ref, *, seq):
    n = pl.program_id(1)
    q = q_ref[...] * jnp.asarray(HEAD_DIM ** -0.5, BF16)
    k = jnp.concatenate([kp_ref[...], kc_ref[...], kn_ref[...]], axis=0)
    v = jnp.concatenate([vp_ref[...], vc_ref[...], vn_ref[...]], axis=0)
    nq, nk = ATTN_BLOCK, 3 * ATTN_BLOCK
    qpos = n * ATTN_BLOCK + lax.broadcasted_iota(jnp.int32, (nq, nk), 0)
    kpos = (n - 1) * ATTN_BLOCK + lax.broadcasted_iota(jnp.int32, (nq, nk), 1)
    valid = (jnp.abs(kpos - qpos) <= WINDOW) & (kpos >= 0) & (kpos < seq)
    scale = HEAD_DIM ** -0.5
    outs = []
    for g in range(N_KV_HEADS):
        kg = k[:, g * HEAD_DIM:(g + 1) * HEAD_DIM]
        vg = v[:, g * HEAD_DIM:(g + 1) * HEAD_DIM]
        for r in range(Q_PER_KV):
            h = g * Q_PER_KV + r
            qh = q[:, h * HEAD_DIM:(h + 1) * HEAD_DIM]
            s = lax.dot_general(qh, kg, (((1,), (1,)), ((), ())), preferred_element_type=F32)
            s = jnp.where(valid, s, NEG)
            sk = sink_ref[h]
            m = jnp.maximum(jnp.max(s, axis=-1, keepdims=True), sk)
            p = jnp.exp(s - m)
            den = jnp.sum(p, axis=-1, keepdims=True) + jnp.exp(sk - m)
            o = jnp.dot(p.astype(BF16), vg, preferred_element_type=F32) / den
            outs.append(o)
    o_ref[...] = jnp.concatenate(outs, axis=1)


def _attention(qkv, sink, batch, seq):
    T = qkv.shape[0]
    nb = seq // ATTN_BLOCK
    q_cols = N_Q_HEADS * HEAD_DIM
    kv_cols = N_KV_HEADS * HEAD_DIM
    k_blk = q_cols // kv_cols
    v_blk = k_blk + 1
    blk = ATTN_BLOCK

    def row(b, n):
        return b * nb + n

    def kv_spec(col_blk, shift):
        return pl.BlockSpec((blk, kv_cols), lambda b, n: (row(b, jnp.clip(n + shift, 0, nb - 1)), col_blk))

    return pl.pallas_call(
        functools.partial(_attn_kernel, seq=seq),
        grid=(batch, nb),
        in_specs=[
            pl.BlockSpec(memory_space=pltpu.SMEM),
            pl.BlockSpec((blk, q_cols), lambda b, n: (row(b, n), 0)),
            kv_spec(k_blk, -1), kv_spec(k_blk, 0), kv_spec(k_blk, 1),
            kv_spec(v_blk, -1), kv_spec(v_blk, 0), kv_spec(v_blk, 1),
        ],
        out_specs=pl.BlockSpec((blk, q_cols), lambda b, n: (row(b, n), 0)),
        out_shape=jax.ShapeDtypeStruct((T, q_cols), F32),
        compiler_params=_cparams(("parallel", "parallel")),
        name="attn",
    )(sink, qkv, qkv, qkv, qkv, qkv, qkv, qkv)


def _scan_chunk(a, u, reverse):
    L, C = a.shape
    row = lax.broadcasted_iota(jnp.int32, (L, C), 0)
    A, H = a, u
    d = 1
    while d < L:
        if d < SUBLANES:
            if reverse:
                keep = row < L - d
                As = jnp.where(keep, pltpu.roll(A, L - d, 0), 1.0)
                Hs = jnp.where(keep, pltpu.roll(H, L - d, 0), 0.0)
            else:
                keep = row >= d
                As = jnp.where(keep, pltpu.roll(A, d, 0), 1.0)
                Hs = jnp.where(keep, pltpu.roll(H, d, 0), 0.0)
        else:
            one = jnp.ones((d, C), F32)
            zero = jnp.zeros((d, C), F32)
            if reverse:
                As = jnp.concatenate([A[d:], one], axis=0)
                Hs = jnp.concatenate([H[d:], zero], axis=0)
            else:
                As = jnp.concatenate([one, A[:L - d]], axis=0)
                Hs = jnp.concatenate([zero, H[:L - d]], axis=0)
        H = A * Hs + H
        A = A * As
        d *= 2
    return A, H


def _lru_kernel(xg_ref, xr_ref, cw_ref, cb_ref, wg_ref, bg_ref, lam_ref, y_ref, xp_ref, hf_ref):
    S, C = xr_ref.shape
    L = LRU_CHUNK
    n_chunks = S // L
    halo = SUBLANES

    xp_ref[0:halo, :] = jnp.zeros((halo, C), F32)
    xp_ref[S + halo:S + 2 * halo, :] = jnp.zeros((halo, C), F32)

    def copy_body(ci, carry):
        t0 = pl.multiple_of(ci * L, L)
        xp_ref[pl.ds(t0 + halo, L), :] = xr_ref[pl.ds(t0, L), :]
        return carry

    lax.fori_loop(0, n_chunks, copy_body, 0)

    cw = cw_ref[...]
    cb = cb_ref[...]
    bias = bg_ref[0]
    neg_c_softplus = -LRU_C * jax.nn.softplus(-lam_ref[0])

    def conv_chunk(t0):
        win = xp_ref[pl.ds(t0, L + 2 * halo), :]
        acc = cb
        for j in range(CONV_WIDTH):
            off = halo - CONV_PAD_LEFT + j
            acc = acc + cw[j:j + 1, :] * win[off:off + L, :]
        return acc

    def gate_au(c, direction):
        w = wg_ref[0, :, direction * 2 * C:(direction + 1) * 2 * C]
        z = jnp.dot(c.astype(BF16), w, preferred_element_type=F32) + bias[:, direction * 2 * C:(direction + 1) * 2 * C]
        r = jax.nn.sigmoid(z[:, :C])
        i = jax.nn.sigmoid(z[:, C:])
        log_a = r * neg_c_softplus[direction:direction + 1, :]
        a = jnp.exp(log_a)
        u = jnp.sqrt(1.0 - a * a) * (i * c)
        return a, u

    def fwd_body(ci, carry):
        t0 = pl.multiple_of(ci * L, L)
        c = conv_chunk(t0)
        a, u = gate_au(c, 0)
        A, H = _scan_chunk(a, u, reverse=False)
        h = H + A * carry
        hf_ref[pl.ds(t0, L), :] = h
        return h[L - 1:L, :]

    lax.fori_loop(0, n_chunks, fwd_body, jnp.zeros((1, C), F32))

    def bwd_body(k, carry):
        t0 = pl.multiple_of((n_chunks - 1 - k) * L, L)
        c = conv_chunk(t0)
        a, u = gate_au(c, 1)
        A, H = _scan_chunk(a, u, reverse=True)
        h = H + A * carry
        y_ref[pl.ds(t0, L), :] = jax.nn.gelu(xg_ref[pl.ds(t0, L), :]) * (hf_ref[pl.ds(t0, L), :] + h)
        return h[0:1, :]

    lax.fori_loop(0, n_chunks, bwd_body, jnp.zeros((1, C), F32))


def _block_diag_chunks(w, per_chunk):
    nblk, b, _ = w.shape
    w4 = w.reshape(nblk // per_chunk, per_chunk, b, b)
    eye = jnp.eye(per_chunk, dtype=w.dtype)
    m = w4[:, :, :, None, :] * eye[None, :, None, :, None]
    return m.reshape(nblk // per_chunk, per_chunk * b, per_chunk * b)


def _lru(gr, conv_w, conv_b, gate_ws, gate_bs, lams, batch, seq):
    T, two_w = gr.shape
    W = two_w // 2
    C = LRU_CHANNELS
    n_ch = W // C
    per_chunk = C // LRU_BLOCK
    wg = jnp.concatenate([_block_diag_chunks(w, per_chunk) for w in gate_ws], axis=-1).astype(BF16)
    bg = jnp.concatenate([b.reshape(n_ch, 1, C) for b in gate_bs], axis=-1)
    lam = jnp.stack([l.reshape(n_ch, C) for l in lams], axis=1)
    return pl.pallas_call(
        _lru_kernel,
        grid=(batch, n_ch),
        in_specs=[
            pl.BlockSpec((seq, C), lambda b, c: (b, c)),
            pl.BlockSpec((seq, C), lambda b, c: (b, n_ch + c)),
            pl.BlockSpec((CONV_WIDTH, C), lambda b, c: (0, c)),
            pl.BlockSpec((1, C), lambda b, c: (0, c)),
            pl.BlockSpec((1, C, 4 * C), lambda b, c: (c, 0, 0)),
            pl.BlockSpec((1, 1, 4 * C), lambda b, c: (c, 0, 0)),
            pl.BlockSpec((1, 2, C), lambda b, c: (c, 0, 0)),
        ],
        out_specs=pl.BlockSpec((seq, C), lambda b, c: (b, c)),
        out_shape=jax.ShapeDtypeStruct((T, W), F32),
        scratch_shapes=[pltpu.VMEM((seq + 2 * SUBLANES, C), F32), pltpu.VMEM((seq, C), F32)],
        compiler_params=_cparams(("parallel", "parallel")),
        name="lru",
    )(gr, gr, conv_w, conv_b.reshape(1, W), wg, bg, lam)


def _outproj_kernel(ya_ref, yl_ref, x_ref, ga_ref, gl_ref, w_ref, gf_ref, x1_ref, xn_ref, xs_ref):
    y = jnp.concatenate([_rms(ya_ref[...], ga_ref[...]), _rms(yl_ref[...], gl_ref[...])], axis=1).astype(BF16)
    x1 = x_ref[...] + jnp.dot(y, w_ref[...], preferred_element_type=F32)
    x1_ref[...] = x1
    xn = _rms(x1, gf_ref[...])
    xn_ref[...] = xn.astype(BF16)
    for s in range(xs_ref.shape[1]):
        xs_ref[:, s, :] = xn[:, s * LANES:(s + 1) * LANES]


def _outproj(y_attn, y_lru, xt, g_attn, g_lru, w_out_bf, g_ffn):
    T, D = xt.shape
    wa, wl = y_attn.shape[1], y_lru.shape[1]
    tm = OUTPROJ_TM
    return pl.pallas_call(
        _outproj_kernel,
        grid=(T // tm,),
        in_specs=[
            pl.BlockSpec((tm, wa), lambda i: (i, 0)),
            pl.BlockSpec((tm, wl), lambda i: (i, 0)),
            pl.BlockSpec((tm, D), lambda i: (i, 0)),
            pl.BlockSpec((1, wa), lambda i: (0, 0)),
            pl.BlockSpec((1, wl), lambda i: (0, 0)),
            pl.BlockSpec((wa + wl, D), lambda i: (0, 0)),
            pl.BlockSpec((1, D), lambda i: (0, 0)),
        ],
        out_specs=[pl.BlockSpec((tm, D), lambda i: (i, 0)), pl.BlockSpec((tm, D), lambda i: (i, 0)),
                   pl.BlockSpec((tm, D // LANES, LANES), lambda i: (i, 0, 0))],
        out_shape=[jax.ShapeDtypeStruct((T, D), F32), jax.ShapeDtypeStruct((T, D), BF16),
                   jax.ShapeDtypeStruct((T, D // LANES, LANES), F32)],
        compiler_params=_cparams(("parallel",)),
        name="outproj",
    )(y_attn, y_lru, xt, g_attn, g_lru, w_out_bf, g_ffn)


def _pq_kernel(xn_ref, w_ref, k_ref, s_ref):
    q = jnp.dot(xn_ref[...], w_ref[...], preferred_element_type=F32).astype(BF16)
    for hh in range(2 * PEER_HEADS):
        half, head = divmod(hh, PEER_HEADS)
        col = (head * 2 + half) * PEER_HALF
        qh = q[:, col:col + PEER_HALF]
        s_ref[hh] = lax.dot_general(k_ref[hh], qh, (((1,), (1,)), ((), ())), preferred_element_type=F32)


def _peer_scores(xn, w_pq_bf, keys_bf):
    T, D = xn.shape
    tm = PQ_TM
    nh = keys_bf.shape[0]
    return pl.pallas_call(
        _pq_kernel,
        grid=(T // tm,),
        in_specs=[
            pl.BlockSpec((tm, D), lambda i: (i, 0)),
            pl.BlockSpec(w_pq_bf.shape, lambda i: (0, 0)),
            pl.BlockSpec(keys_bf.shape, lambda i: (0, 0, 0)),
        ],
        out_specs=pl.BlockSpec((nh, N_KEYS, tm), lambda i: (0, 0, i)),
        out_shape=jax.ShapeDtypeStruct((nh, N_KEYS, T), F32),
        compiler_params=_cparams(("parallel",)),
        name="pq",
    )(xn, w_pq_bf, keys_bf)


def _top16_rows(s, payload=None):
    n, tl = s.shape
    row = lax.broadcasted_iota(jnp.int32, (n, tl), 0)
    vals, idxs, pays = [], [], []
    for _ in range(PEER_TOPK):
        m = jnp.max(s, axis=0, keepdims=True)
        idx = jnp.min(jnp.where(s == m, row, n), axis=0, keepdims=True)
        hit = row == idx
        vals.append(m)
        idxs.append(idx)
        if payload is not None:
            pays.append(jnp.max(jnp.where(hit, payload, -1), axis=0, keepdims=True))
        s = jnp.where(hit, -jnp.inf, s)
    cat = lambda xs: jnp.concatenate(xs, axis=0)
    return cat(vals), cat(idxs), (cat(pays) if payload is not None else None)


def _topk_kernel(s1_ref, s2_ref, g_ref, e_ref):
    v1, i1, _ = _top16_rows(s1_ref[0])
    v2, i2, _ = _top16_rows(s2_ref[0])
    K = PEER_TOPK
    tl = v1.shape[1]
    sub = lax.broadcasted_iota(jnp.int32, (SUBLANES, tl), 0)
    cand, cidx = [], []
    for a in range(K // 2):
        n_b = K // (a + 1)
        for b0 in range(0, n_b, SUBLANES):
            keep = sub < (n_b - b0)
            cand.append(jnp.where(keep, v1[a:a + 1, :] + v2[b0:b0 + SUBLANES, :], -jnp.inf))
            cidx.append(i1[a:a + 1, :] * N_KEYS + i2[b0:b0 + SUBLANES, :])
    cand.append(v1[K // 2:, :] + v2[0:1, :])
    cidx.append(i1[K // 2:, :] * N_KEYS + i2[0:1, :])
    sc, _, eidx = _top16_rows(jnp.concatenate(cand, axis=0), jnp.concatenate(cidx, axis=0))
    ex = jnp.exp(sc - sc[0:1, :])
    g_ref[0] = ex / jnp.sum(ex, axis=0, keepdims=True)
    e_ref[0] = eidx


def _peer_topk(scores):
    nh2, nk, T = scores.shape
    nh = nh2 // 2
    tl = TOPK_TL
    return pl.pallas_call(
        _topk_kernel,
        grid=(nh, T // tl),
        in_specs=[
            pl.BlockSpec((1, nk, tl), lambda h, i: (h, 0, i)),
            pl.BlockSpec((1, nk, tl), lambda h, i: (nh + h, 0, i)),
        ],
        out_specs=[
            pl.BlockSpec((1, PEER_TOPK, tl), lambda h, i: (h, 0, i)),
            pl.BlockSpec((1, PEER_TOPK, tl), lambda h, i: (h, 0, i)),
        ],
        out_shape=[
            jax.ShapeDtypeStruct((nh, PEER_TOPK, T), F32),
            jax.ShapeDtypeStruct((nh, PEER_TOPK, T), jnp.int32),
        ],
        compiler_params=_cparams(("parallel", "parallel")),
        name="topk",
    )(scores, scores)


ROW_SUBLANES = 16


def _pack_rows(emb):
    n, d = emb.shape
    half = d // 2
    assert half == SUBLANES * LANES
    bits = lax.bitcast_convert_type(emb.astype(BF16), jnp.uint16).astype(jnp.uint32)
    return (bits[:, :half] | (bits[:, half:] << 16)).reshape(n * SUBLANES, LANES)


def _unpack_rows(word):
    return pltpu.bitcast(word << 16, F32), pltpu.bitcast(word & jnp.uint32(0xFFFF0000), F32)


def _sublane_fold(parts):
    sub = lax.broadcasted_iota(jnp.int32, (SUBLANES, LANES), 0)
    step = 1
    while len(parts) > 1:
        low = (sub & step) == 0
        parts = [jnp.where(low, a, b) + pltpu.roll(jnp.where(low, b, a), step, 0)
                 for a, b in zip(parts[0::2], parts[1::2])]
        step *= 2
    return parts[0]


def _chunk_list_stride(chunks_per_token):
    return -(-PEER_TM * chunks_per_token // 1024) * 1024


def _route(eidx, gates, n_blocks):
    T, P = eidx.shape
    u32 = jnp.uint32
    blocks = jnp.arange(n_blocks, dtype=jnp.int32)
    blk = eidx // EXPERT_BLOCK
    cnt = jnp.sum((blk[:, :, None] == blocks[None, None, :]).astype(jnp.int32), axis=1)
    pcnt = (cnt + CHUNK - 1) // CHUNK * CHUNK
    cand = jnp.arange(CHUNK, dtype=jnp.int32)
    active = cand[None, None, :] < (pcnt - cnt)[:, :, None]
    gate_bits = (lax.bitcast_convert_type(gates, u32) + u32(1 << (GATE_SHIFT - 1))) >> GATE_SHIFT
    pad_mark = u32((1 << GATE_BITS) - 1)
    word = (eidx.astype(u32) << GATE_BITS) | gate_bits
    last_row = ((blocks + 1) * EXPERT_BLOCK - 1).astype(u32)
    pad_word = jnp.where(active, ((last_row << GATE_BITS) | pad_mark)[None, :, None], u32(0xFFFFFFFF))
    srt = lax.sort(jnp.concatenate([word, pad_word.reshape(T, n_blocks * CHUNK)], axis=1), dimension=1)
    is_pad = (srt & pad_mark) == pad_mark
    loc = jnp.where(is_pad, u32(0), ((srt >> GATE_BITS) & u32(EXPERT_BLOCK - 1)) * SUBLANES)
    gate = lax.bitcast_convert_type(jnp.where(is_pad, u32(0), (srt & pad_mark) << GATE_SHIFT), F32)
    ends = jnp.cumsum(pcnt, axis=1) // CHUNK
    cs = jnp.concatenate([jnp.zeros((T, 1), jnp.int32), ends,
                          jnp.zeros((T, CS_STRIDE - n_blocks - 1), jnp.int32)], axis=1)
    n_chunks = jnp.sum((ends - cs[:, :n_blocks]).reshape(T // PEER_TM, PEER_TM, n_blocks), axis=1)

    per_tok = (P + n_blocks * CHUNK) // CHUNK
    slot_j = jnp.arange(per_tok, dtype=jnp.int32)
    slot_blk = jnp.sum((slot_j[None, :, None] >= ends[:, None, :]).astype(jnp.int32), axis=2)
    tok = (jnp.arange(T, dtype=jnp.int32) % PEER_TM)[:, None]
    entry = (tok << 16) | (tok * (per_tok * CHUNK) + slot_j[None, :] * CHUNK)
    assert PEER_TM <= 1 << 8
    order = lax.sort(((slot_blk << 24) | entry).reshape(T // PEER_TM, PEER_TM * per_tok), dimension=1)
    clist = order & ((1 << 24) - 1)
    clist = jnp.pad(clist, ((0, 0), (0, _chunk_list_stride(per_tok) - PEER_TM * per_tok)))

    n_tiles, per_tile = T // PEER_TM, PEER_TM * per_tok
    gate_tile = gate.reshape(n_tiles, PEER_TM, per_tok * CHUNK)
    q_pos = jnp.arange(per_tile, dtype=jnp.int32)
    gates_c = []
    for b in range(n_blocks):
        lst = lax.sort((((slot_blk != b).astype(jnp.int32) << 24) | entry).reshape(n_tiles, per_tile), dimension=1)
        tok_q = (lst >> 16) & 0xFF
        j_q = ((lst & 0xFFFF) - tok_q * (per_tok * CHUNK)) // CHUNK
        tok_hot = (tok_q[:, :, None] == jnp.arange(PEER_TM, dtype=jnp.int32)[None, None, :]).astype(F32)
        rows = jnp.einsum('xqt,xtm->xqm', tok_hot, gate_tile, precision=lax.Precision.HIGHEST)
        pick = (j_q[:, :, None] == slot_j[None, None, :]) & (q_pos[None, :, None] < n_chunks[:, b][:, None, None])
        rows = rows.reshape(n_tiles, per_tile, per_tok, CHUNK)
        gates_c.append(jnp.sum(jnp.where(pick[..., None], rows, 0.0), axis=2))
    gates_c = jnp.stack(gates_c).reshape(n_blocks, n_tiles, per_tile * CHUNK // LANES, LANES)
    return loc.astype(jnp.int32).reshape(-1), gates_c, n_chunks.T.reshape(-1), clist.reshape(-1)


def _pdot_kernel(nch_ref, cl_ref, idx_ref, x_ref, u_ref, d_ref, rbuf_ref, *, n_blocks):
    b = pl.program_id(0)
    i = pl.program_id(1)
    n_tiles = pl.num_programs(1)
    n_chunks = nch_ref[b * n_tiles + i]
    first = jnp.int32(0)
    for bb in range(n_blocks - 1):
        first = first + jnp.where(bb < b, nch_ref[bb * n_tiles + i], 0)
    zero_rows = jnp.zeros((SUBLANES, LANES), F32)

    def chunk(q, carry):
        entry = cl_ref[first + q]
        base = entry & 0xFFFF
        x = x_ref[entry >> 16]
        x_lo, x_hi = x[:SUBLANES], x[SUBLANES:]
        for h in range(CHUNK // SUBLANES):
            parts = []
            for k in range(SUBLANES):
                row = pl.multiple_of(idx_ref[base + h * SUBLANES + k], SUBLANES)
                lo, hi = _unpack_rows(u_ref[pl.ds(row, SUBLANES), :])
                parts.append(lo * x_lo + hi * x_hi)
            row0 = pl.multiple_of(q * CHUNK + h * SUBLANES, SUBLANES)
            rbuf_ref[pl.ds(row0, SUBLANES), :] = _sublane_fold(parts)
        return carry

    lax.fori_loop(0, n_chunks // 2, lambda p, c: chunk(2 * p + 1, chunk(2 * p, c)), 0)

    @pl.when(n_chunks % 2 == 1)
    def _():
        chunk(n_chunks - 1, 0)

    per_step = SUBLANES * LANES // CHUNK
    n_steps = (n_chunks + per_step - 1) // per_step
    folds_per_chunk = CHUNK // SUBLANES

    def zero_body(f, carry):
        rbuf_ref[pl.ds(pl.multiple_of(f * SUBLANES, SUBLANES), SUBLANES), :] = zero_rows
        return carry

    lax.fori_loop(n_chunks * folds_per_chunk, n_steps * per_step * folds_per_chunk, zero_body, 0)
    d_ref[...] = jnp.zeros(d_ref.shape, F32)

    def reduce_body(s, carry):
        dots = []
        for j in range(SUBLANES):
            rows = pl.ds(pl.multiple_of((s * SUBLANES + j) * LANES, LANES), LANES)
            dots.append(jnp.sum(rbuf_ref[rows, :].T, axis=0, keepdims=True))
        d_ref[pl.ds(pl.multiple_of(s * SUBLANES, SUBLANES), SUBLANES), :] = jnp.concatenate(dots, axis=0)
        return carry

    lax.fori_loop(0, n_steps, reduce_body, 0)


def _expert_block_spec():
    return pl.BlockSpec((EXPERT_BLOCK * SUBLANES, LANES), lambda b, i: (b, 0), pipeline_mode=pl.Buffered(1))


def _peer_dots(n_chunks, clist, idx, x3, u_rows, slots):
    T = x3.shape[0]
    nb = u_rows.shape[0] // (EXPERT_BLOCK * SUBLANES)
    tm = PEER_TM
    n_tiles = T // tm
    max_pairs = tm * slots
    groups = max_pairs // LANES
    smem_tile = lambda n: pl.BlockSpec((tm * n,), lambda b, i: (i,), memory_space=pltpu.SMEM)
    out = jax.ShapeDtypeStruct((nb, n_tiles, groups, LANES), F32)
    out_spec = pl.BlockSpec((None, None, groups, LANES), lambda b, i: (b, i, 0, 0))
    return pl.pallas_call(
        functools.partial(_pdot_kernel, n_blocks=nb),
        grid=(nb, n_tiles),
        in_specs=[
            pl.BlockSpec(memory_space=pltpu.SMEM),
            pl.BlockSpec((_chunk_list_stride(slots // CHUNK),), lambda b, i: (i,), memory_space=pltpu.SMEM),
            smem_tile(slots),
            pl.BlockSpec((tm, ROW_SUBLANES, LANES), lambda b, i: (i, 0, 0)),
            _expert_block_spec(),
        ],
        out_specs=out_spec,
        out_shape=out,
        scratch_shapes=[pltpu.VMEM((max_pairs, LANES), F32)],
        compiler_params=_cparams(("arbitrary", "arbitrary")),
        name="pdot",
    )(n_chunks, clist, idx, x3, u_rows)


def _act_kernel(d_ref, g_ref, w_ref):
    w_ref[...] = g_ref[...] * jax.nn.gelu(d_ref[...])


def _peer_act(dots, gates):
    rows, lanes = dots.shape
    tm = min(4096, rows)
    spec = pl.BlockSpec((tm, lanes), lambda i: (i, 0))
    return pl.pallas_call(
        _act_kernel,
        grid=(rows // tm,),
        in_specs=[spec, spec],
        out_specs=spec,
        out_shape=jax.ShapeDtypeStruct((rows, lanes), F32),
        compiler_params=_cparams(("parallel",)),
        name="act",
    )(dots, gates)


def _pacc_kernel(nch_ref, cl_ref, idx_ref, w_ref, v_ref, o_ref, wrep_ref, *, n_blocks):
    b = pl.program_id(0)
    i = pl.program_id(1)
    n_tiles = pl.num_programs(1)
    n_chunks = nch_ref[b * n_tiles + i]
    first = jnp.int32(0)
    for bb in range(n_blocks - 1):
        first = first + jnp.where(bb < b, nch_ref[bb * n_tiles + i], 0)

    n_groups = (n_chunks * CHUNK + LANES - 1) // LANES
    unroll = 8

    def expand_body(g4, carry):
        for j in range(unroll):
            g = g4 * unroll + j
            rows = jnp.broadcast_to(w_ref[pl.ds(g, 1), :], (LANES, LANES))
            wrep_ref[pl.ds(pl.multiple_of(g * LANES, LANES), LANES), :] = rows.T
        return carry

    assert w_ref.shape[0] % unroll == 0
    lax.fori_loop(0, (n_groups + unroll - 1) // unroll, expand_body, 0)

    o_ref[...] = jnp.zeros(o_ref.shape, F32)

    def chunk(q, carry):
        entry = cl_ref[first + q]
        base = entry & 0xFFFF
        tok = entry >> 16
        zero = jnp.zeros((SUBLANES, LANES), F32)
        accs = [zero, zero, zero, zero]
        for k in range(CHUNK):
            w = jnp.broadcast_to(wrep_ref[pl.ds(q * CHUNK + k, 1), :], (SUBLANES, LANES))
            lo, hi = _unpack_rows(v_ref[pl.ds(pl.multiple_of(idx_ref[base + k], SUBLANES), SUBLANES), :])
            j = 2 * (k % 2)
            accs[j] = accs[j] + w * lo
            accs[j + 1] = accs[j + 1] + w * hi
        o_ref[tok] = o_ref[tok] + jnp.concatenate([accs[0] + accs[2], accs[1] + accs[3]], axis=0)
        return carry

    lax.fori_loop(0, n_chunks // 2, lambda p, c: chunk(2 * p + 1, chunk(2 * p, c)), 0)

    @pl.when(n_chunks % 2 == 1)
    def _():
        chunk(n_chunks - 1, 0)


def _peer_accumulate(n_chunks, clist, idx, w, v_rows, T, slots):
    nb = v_rows.shape[0] // (EXPERT_BLOCK * SUBLANES)
    tm = PEER_TM
    n_tiles = T // tm
    groups = tm * slots // LANES
    smem_tile = lambda n: pl.BlockSpec((tm * n,), lambda b, i: (i,), memory_space=pltpu.SMEM)
    return pl.pallas_call(
        functools.partial(_pacc_kernel, n_blocks=nb),
        grid=(nb, n_tiles),
        in_specs=[
            pl.BlockSpec(memory_space=pltpu.SMEM),
            pl.BlockSpec((_chunk_list_stride(slots // CHUNK),), lambda b, i: (i,), memory_space=pltpu.SMEM),
            smem_tile(slots),
            pl.BlockSpec((None, None, groups, LANES), lambda b, i: (b, i, 0, 0)),
            _expert_block_spec(),
        ],
        out_specs=pl.BlockSpec((None, tm, ROW_SUBLANES, LANES), lambda b, i: (b, i, 0, 0)),
        out_shape=jax.ShapeDtypeStruct((nb, T, ROW_SUBLANES, LANES), F32),
        scratch_shapes=[pltpu.VMEM((tm * slots, LANES), F32)],
        compiler_params=_cparams(("arbitrary", "arbitrary")),
        name="pacc",
    )(n_chunks, clist, idx, w, v_rows)


def _final_kernel(x1_ref, p_ref, g_ref, o_ref):
    nb, _, n_s, _ = p_ref.shape
    cols = []
    for s in range(n_s):
        col = x1_ref[:, s * LANES:(s + 1) * LANES]
        for b in range(nb):
            col = col + p_ref[b, :, s, :]
        cols.append(col)
    o_ref[...] = _rms(jnp.concatenate(cols, axis=1), g_ref[...])


def _final(x1, peer_parts, g_final):
    T, D = x1.shape
    nb = peer_parts.shape[0]
    tm = 256
    return pl.pallas_call(
        _final_kernel,
        grid=(T // tm,),
        in_specs=[
            pl.BlockSpec((tm, D), lambda i: (i, 0)),
            pl.BlockSpec((nb, tm, D // LANES, LANES), lambda i: (0, i, 0, 0)),
            pl.BlockSpec((1, D), lambda i: (0, 0)),
        ],
        out_specs=pl.BlockSpec((tm, D), lambda i: (i, 0)),
        out_shape=jax.ShapeDtypeStruct((T, D), F32),
        compiler_params=_cparams(("parallel",)),
        name="final",
    )(x1, peer_parts, g_final)


def _rope_tables(seq):
    half = HEAD_DIM // 2
    inv = ROPE_THETA ** (-jnp.arange(half, dtype=F32) / half)
    ang = jnp.arange(seq).astype(F32)[:, None] * inv[None, :]
    cos, sin = jnp.cos(ang), jnp.sin(ang)
    reps = LANES // HEAD_DIM
    cos_t = jnp.concatenate([cos, cos] * reps, axis=1)
    sin_t = jnp.concatenate([-sin, sin] * reps, axis=1)
    return cos_t, sin_t


def _layer(xt, batch, seq, g_mix, w_in, sink, conv_w, conv_b, fwd_wa, fwd_ba, fwd_wx, fwd_bx, fwd_lam,
           bwd_wa, bwd_ba, bwd_wx, bwd_bx, bwd_lam, g_attn_out, g_lru_out, w_out, g_ffn,
           w_pq, sub_k1, sub_k2, u_emb, v_emb):
    T, D = xt.shape
    cos_t, sin_t = _rope_tables(seq)
    qkv, gr = _inproj(xt, g_mix.reshape(1, D), w_in.astype(BF16), cos_t, sin_t, seq)
    y_attn = _attention(qkv, sink, batch, seq)
    y_lru = _lru(gr, conv_w, conv_b, (fwd_wa, fwd_wx, bwd_wa, bwd_wx), (fwd_ba, fwd_bx, bwd_ba, bwd_bx),
                 (fwd_lam, bwd_lam), batch, seq)
    x1, xn, x3 = _outproj(y_attn, y_lru, xt, g_attn_out.reshape(1, -1), g_lru_out.reshape(1, -1),
                      w_out.astype(BF16), g_ffn.reshape(1, D))

    keys = jnp.concatenate([sub_k1, sub_k2], axis=0).astype(BF16)
    scores = _peer_scores(xn, w_pq.astype(BF16), keys)
    gates_t, eidx_t = _peer_topk(scores)
    n_pairs = PEER_HEADS * PEER_TOPK
    gates = gates_t.transpose(2, 0, 1).reshape(T, n_pairs)
    eidx = eidx_t.transpose(2, 0, 1).reshape(T, n_pairs)
    n_exp = u_emb.shape[0]
    nb = n_exp // EXPERT_BLOCK
    slots = n_pairs + nb * CHUNK
    idx, gates_c, n_chunks, clist = _route(eidx, gates, nb)

    u_rows = _pack_rows(u_emb)
    v_rows = _pack_rows(v_emb)
    dots = _peer_dots(n_chunks, clist, idx, x3, u_rows, slots)
    w = _peer_act(dots.reshape(-1, LANES), gates_c.reshape(-1, LANES)).reshape(dots.shape)
    parts = _peer_accumulate(n_chunks, clist, idx, w, v_rows, T, slots)
    return x1, parts


def kernel(x, g_mix, w_in, sink, conv_w, conv_b, fwd_wa, fwd_ba, fwd_wx, fwd_bx, fwd_lam, bwd_wa, bwd_ba, bwd_wx, bwd_bx, bwd_lam, g_attn_out, g_lru_out, w_out, g_ffn, w_pq, sub_k1, sub_k2, u_emb, v_emb, g_final):
    B, S, D = x.shape
    assert g_mix.shape[0] == 1, "single-layer trunk"
    xt = x.reshape(B * S, D)
    x1, parts = _layer(xt, B, S, g_mix[0], w_in[0], sink[0], conv_w[0], conv_b[0],
                       fwd_wa[0], fwd_ba[0], fwd_wx[0], fwd_bx[0], fwd_lam[0],
                       bwd_wa[0], bwd_ba[0], bwd_wx[0], bwd_bx[0], bwd_lam[0],
                       g_attn_out[0], g_lru_out[0], w_out[0], g_ffn[0],
                       w_pq[0], sub_k1[0], sub_k2[0], u_emb[0], v_emb[0])
    return _final(x1, parts, g_final.reshape(1, D)).reshape(B, S, D)
```

```python
import functools

import jax
import jax.numpy as jnp
from jax import lax
from jax.experimental import pallas as pl
from jax.experimental.pallas import tpu as pltpu

F32 = jnp.float32
BF16 = jnp.bfloat16

HEAD_DIM = 64
N_KV_HEADS = 4
Q_PER_KV = 4
N_Q_HEADS = N_KV_HEADS * Q_PER_KV
WINDOW = 128
ATTN_BLOCK = 128
ROPE_THETA = 10000.0
LRU_BLOCK = 64
CONV_WIDTH = 4
CONV_PAD_LEFT = 2
LRU_C = 8.0
PEER_HEADS = 8
PEER_HALF = 128
N_KEYS = 128
PEER_TOPK = 16
EPS = 1e-6
NEG = -1e30

LANES = 128
SUBLANES = 8
VMEM_LIMIT = 56 * 1024 * 1024

INPROJ_TM = 512
INPROJ_TN = 512
LRU_CHANNELS = 256
LRU_CHUNK = 256
OUTPROJ_TM = 256
PQ_TM = 256
TOPK_TL = 512
EXPERT_BLOCK = 8192
PEER_TM = 128
CHUNK = 16
CS_STRIDE = 8
GATE_BITS = 18
GATE_SHIFT = 31 - GATE_BITS


def _cparams(sem):
    return pltpu.CompilerParams(dimension_semantics=sem, vmem_limit_bytes=VMEM_LIMIT)


def _rms(x, g):
    return x * lax.rsqrt(jnp.mean(x * x, axis=-1, keepdims=True) + EPS) * g


def _inproj_kernel(x_ref, g_ref, w_ref, cos_ref, sin_ref, qkv_ref, gr_ref, h_ref, *, n_rope_tiles, kv_cols):
    j = pl.program_id(1)

    @pl.when(j == 0)
    def _():
        h_ref[...] = _rms(x_ref[...], g_ref[...]).astype(BF16)

    acc = jnp.dot(h_ref[...], w_ref[...], preferred_element_type=F32)
    tm, tn = acc.shape

    @pl.when(j < n_rope_tiles)
    def _():
        reps = tn // LANES
        cos = jnp.concatenate([cos_ref[...]] * reps, axis=1)
        sin = jnp.concatenate([sin_ref[...]] * reps, axis=1)
        lane = lax.broadcasted_iota(jnp.int32, (tm, tn), 1)
        first = (lane % HEAD_DIM) < (HEAD_DIM // 2)
        partner = jnp.where(first, pltpu.roll(acc, tn - HEAD_DIM // 2, 1), pltpu.roll(acc, HEAD_DIM // 2, 1))
        roped = acc * cos + partner * sin
        is_rope = jnp.logical_or(j < n_rope_tiles - 1, lane < kv_cols)
        qkv_ref[...] = jnp.where(is_rope, roped, acc).astype(BF16)

    @pl.when(j >= n_rope_tiles)
    def _():
        gr_ref[...] = acc


def _inproj(xt, g_mix, w_in_bf, cos_t, sin_t, seq):
    T, D = xt.shape
    n_cols = w_in_bf.shape[1]
    q_cols = N_Q_HEADS * HEAD_DIM
    kv_cols = N_KV_HEADS * HEAD_DIM
    qkv_cols = q_cols + 2 * kv_cols
    tm, tn = INPROJ_TM, INPROJ_TN
    n_rope_tiles = qkv_cols // tn
    assert qkv_cols % tn == 0 and (q_cols % tn == 0) and seq % tm == 0 and T % tm == 0
    pos_blocks = seq // tm
    kern = functools.partial(_inproj_kernel, n_rope_tiles=n_rope_tiles, kv_cols=kv_cols)
    return pl.pallas_call(
        kern,
        grid=(T // tm, n_cols // tn),
        in_specs=[
            pl.BlockSpec((tm, D), lambda i, j: (i, 0)),
            pl.BlockSpec((1, D), lambda i, j: (0, 0)),
            pl.BlockSpec((D, tn), lambda i, j: (0, j)),
            pl.BlockSpec((tm, LANES), lambda i, j: (i % pos_blocks, 0)),
            pl.BlockSpec((tm, LANES), lambda i, j: (i % pos_blocks, 0)),
        ],
        out_specs=[
            pl.BlockSpec((tm, tn), lambda i, j: (i, jnp.minimum(j, n_rope_tiles - 1))),
            pl.BlockSpec((tm, tn), lambda i, j: (i, jnp.maximum(j - n_rope_tiles, 0))),
        ],
        out_shape=[
            jax.ShapeDtypeStruct((T, qkv_cols), BF16),
            jax.ShapeDtypeStruct((T, n_cols - qkv_cols), F32),
        ],
        scratch_shapes=[pltpu.VMEM((tm, D), BF16)],
        compiler_params=_cparams(("parallel", "arbitrary")),
        name="inproj",
    )(xt, g_mix, w_in_bf, cos_t, sin_t)


def _attn_kernel(sink_ref, q_ref, kp_ref, kc_ref, kn_ref, vp_ref, vc_ref, vn_ref, o_ref, *, seq):
    n = pl.program_id(1)
    q = q_ref[...] * jnp.asarray(HEAD_DIM ** -0.5, BF16)
    k = jnp.concatenate([kp_ref[...], kc_ref[...], kn_ref[...]], axis=0)
    v = jnp.concatenate([vp_ref[...], vc_ref[...], vn_ref[...]], axis=0)
    nq, nk = ATTN_BLOCK, 3 * ATTN_BLOCK
    qpos = n * ATTN_BLOCK + lax.broadcasted_iota(jnp.int32, (nq, nk), 0)
    kpos = (n - 1) * ATTN_BLOCK + lax.broadcasted_iota(jnp.int32, (nq, nk), 1)
    valid = (jnp.abs(kpos - qpos) <= WINDOW) & (kpos >= 0) & (kpos < seq)
    scale = HEAD_DIM ** -0.5
    outs = []
    for g in range(N_KV_HEADS):
        kg = k[:, g * HEAD_DIM:(g + 1) * HEAD_DIM]
        vg = v[:, g * HEAD_DIM:(g + 1) * HEAD_DIM]
        for r in range(Q_PER_KV):
            h = g * Q_PER_KV + r
            qh = q[:, h * HEAD_DIM:(h + 1) * HEAD_DIM]
            s = lax.dot_general(qh, kg, (((1,), (1,)), ((), ())), preferred_element_type=F32)
            s = jnp.where(valid, s, NEG)
            sk = sink_ref[h]
            m = jnp.maximum(jnp.max(s, axis=-1, keepdims=True), sk)
            p = jnp.exp(s - m)
            den = jnp.sum(p, axis=-1, keepdims=True) + jnp.exp(sk - m)
            o = jnp.dot(p.astype(BF16), vg, preferred_element_type=F32) / den
            outs.append(o)
    o_ref[...] = jnp.concatenate(outs, axis=1)


def _attention(qkv, sink, batch, seq):
    T = qkv.shape[0]
    nb = seq // ATTN_BLOCK
    q_cols = N_Q_HEADS * HEAD_DIM
    kv_cols = N_KV_HEADS * HEAD_DIM
    k_blk = q_cols // kv_cols
    v_blk = k_blk + 1
    blk = ATTN_BLOCK

    def row(b, n):
        return b * nb + n

    def kv_spec(col_blk, shift):
        return pl.BlockSpec((blk, kv_cols), lambda b, n: (row(b, jnp.clip(n + shift, 0, nb - 1)), col_blk))

    return pl.pallas_call(
        functools.partial(_attn_kernel, seq=seq),
        grid=(batch, nb),
        in_specs=[
            pl.BlockSpec(memory_space=pltpu.SMEM),
            pl.BlockSpec((blk, q_cols), lambda b, n: (row(b, n), 0)),
            kv_spec(k_blk, -1), kv_spec(k_blk, 0), kv_spec(k_blk, 1),
            kv_spec(v_blk, -1), kv_spec(v_blk, 0), kv_spec(v_blk, 1),
        ],
        out_specs=pl.BlockSpec((blk, q_cols), lambda b, n: (row(b, n), 0)),
        out_shape=jax.ShapeDtypeStruct((T, q_cols), F32),
        compiler_params=_cparams(("parallel", "parallel")),
        name="attn",
    )(sink, qkv, qkv, qkv, qkv, qkv, qkv, qkv)


def _scan_chunk(a, u, reverse):
    L, C = a.shape
    row = lax.broadcasted_iota(jnp.int32, (L, C), 0)
    A, H = a, u
    d = 1
    while d < L:
        if d < SUBLANES:
            if reverse:
                keep = row < L - d
                As = jnp.where(keep, pltpu.roll(A, L - d, 0), 1.0)
                Hs = jnp.where(keep, pltpu.roll(H, L - d, 0), 0.0)
            else:
                keep = row >= d
                As = jnp.where(keep, pltpu.roll(A, d, 0), 1.0)
                Hs = jnp.where(keep, pltpu.roll(H, d, 0), 0.0)
        else:
            one = jnp.ones((d, C), F32)
            zero = jnp.zeros((d, C), F32)
            if reverse:
                As = jnp.concatenate([A[d:], one], axis=0)
                Hs = jnp.concatenate([H[d:], zero], axis=0)
            else:
                As = jnp.concatenate([one, A[:L - d]], axis=0)
                Hs = jnp.concatenate([zero, H[:L - d]], axis=0)
        H = A * Hs + H
        A = A * As
        d *= 2
    return A, H


def _lru_kernel(xg_ref, xr_ref, cw_ref, cb_ref, wg_ref, bg_ref, lam_ref, y_ref, xp_ref, hf_ref):
    S, C = xr_ref.shape
    L = LRU_CHUNK
    n_chunks = S // L
    halo = SUBLANES

    xp_ref[0:halo, :] = jnp.zeros((halo, C), F32)
    xp_ref[S + halo:S + 2 * halo, :] = jnp.zeros((halo, C), F32)

    def copy_body(ci, carry):
        t0 = pl.multiple_of(ci * L, L)
        xp_ref[pl.ds(t0 + halo, L), :] = xr_ref[pl.ds(t0, L), :]
        return carry

    lax.fori_loop(0, n_chunks, copy_body, 0)

    cw = cw_ref[...]
    cb = cb_ref[...]
    bias = bg_ref[0]
    neg_c_softplus = -LRU_C * jax.nn.softplus(-lam_ref[0])

    def conv_chunk(t0):
        win = xp_ref[pl.ds(t0, L + 2 * halo), :]
        acc = cb
        for j in range(CONV_WIDTH):
            off = halo - CONV_PAD_LEFT + j
            acc = acc + cw[j:j + 1, :] * win[off:off + L, :]
        return acc

    def gate_au(c, direction):
        w = wg_ref[0, :, direction * 2 * C:(direction + 1) * 2 * C]
        z = jnp.dot(c.astype(BF16), w, preferred_element_type=F32) + bias[:, direction * 2 * C:(direction + 1) * 2 * C]
        r = jax.nn.sigmoid(z[:, :C])
        i = jax.nn.sigmoid(z[:, C:])
        log_a = r * neg_c_softplus[direction:direction + 1, :]
        a = jnp.exp(log_a)
        u = jnp.sqrt(1.0 - a * a) * (i * c)
        return a, u

    def fwd_body(ci, carry):
        t0 = pl.multiple_of(ci * L, L)
        c = conv_chunk(t0)
        a, u = gate_au(c, 0)
        A, H = _scan_chunk(a, u, reverse=False)
        h = H + A * carry
        hf_ref[pl.ds(t0, L), :] = h
        return h[L - 1:L, :]

    lax.fori_loop(0, n_chunks, fwd_body, jnp.zeros((1, C), F32))

    def bwd_body(k, carry):
        t0 = pl.multiple_of((n_chunks - 1 - k) * L, L)
        c = conv_chunk(t0)
        a, u = gate_au(c, 1)
        A, H = _scan_chunk(a, u, reverse=True)
        h = H + A * carry
        y_ref[pl.ds(t0, L), :] = jax.nn.gelu(xg_ref[pl.ds(t0, L), :]) * (hf_ref[pl.ds(t0, L), :] + h)
        return h[0:1, :]

    lax.fori_loop(0, n_chunks, bwd_body, jnp.zeros((1, C), F32))


def _block_diag_chunks(w, per_chunk):
    nblk, b, _ = w.shape
    w4 = w.reshape(nblk // per_chunk, per_chunk, b, b)
    eye = jnp.eye(per_chunk, dtype=w.dtype)
    m = w4[:, :, :, None, :] * eye[None, :, None, :, None]
    return m.reshape(nblk // per_chunk, per_chunk * b, per_chunk * b)


def _lru(gr, conv_w, conv_b, gate_ws, gate_bs, lams, batch, seq):
    T, two_w = gr.shape
    W = two_w // 2
    C = LRU_CHANNELS
    n_ch = W // C
    per_chunk = C // LRU_BLOCK
    wg = jnp.concatenate([_block_diag_chunks(w, per_chunk) for w in gate_ws], axis=-1).astype(BF16)
    bg = jnp.concatenate([b.reshape(n_ch, 1, C) for b in gate_bs], axis=-1)
    lam = jnp.stack([l.reshape(n_ch, C) for l in lams], axis=1)
    return pl.pallas_call(
        _lru_kernel,
        grid=(batch, n_ch),
        in_specs=[
            pl.BlockSpec((seq, C), lambda b, c: (b, c)),
            pl.BlockSpec((seq, C), lambda b, c: (b, n_ch + c)),
            pl.BlockSpec((CONV_WIDTH, C), lambda b, c: (0, c)),
            pl.BlockSpec((1, C), lambda b, c: (0, c)),
            pl.BlockSpec((1, C, 4 * C), lambda b, c: (c, 0, 0)),
            pl.BlockSpec((1, 1, 4 * C), lambda b, c: (c, 0, 0)),
            pl.BlockSpec((1, 2, C), lambda b, c: (c, 0, 0)),
        ],
        out_specs=pl.BlockSpec((seq, C), lambda b, c: (b, c)),
        out_shape=jax.ShapeDtypeStruct((T, W), F32),
        scratch_shapes=[pltpu.VMEM((seq + 2 * SUBLANES, C), F32), pltpu.VMEM((seq, C), F32)],
        compiler_params=_cparams(("parallel", "parallel")),
        name="lru",
    )(gr, gr, conv_w, conv_b.reshape(1, W), wg, bg, lam)


def _outproj_kernel(ya_ref, yl_ref, x_ref, ga_ref, gl_ref, w_ref, gf_ref, x1_ref, xn_ref, xs_ref):
    y = jnp.concatenate([_rms(ya_ref[...], ga_ref[...]), _rms(yl_ref[...], gl_ref[...])], axis=1).astype(BF16)
    x1 = x_ref[...] + jnp.dot(y, w_ref[...], preferred_element_type=F32)
    x1_ref[...] = x1
    xn = _rms(x1, gf_ref[...])
    xn_ref[...] = xn.astype(BF16)
    for s in range(xs_ref.shape[1]):
        xs_ref[:, s, :] = xn[:, s * LANES:(s + 1) * LANES]


def _outproj(y_attn, y_lru, xt, g_attn, g_lru, w_out_bf, g_ffn):
    T, D = xt.shape
    wa, wl = y_attn.shape[1], y_lru.shape[1]
    tm = OUTPROJ_TM
    return pl.pallas_call(
        _outproj_kernel,
        grid=(T // tm,),
        in_specs=[
            pl.BlockSpec((tm, wa), lambda i: (i, 0)),
            pl.BlockSpec((tm, wl), lambda i: (i, 0)),
            pl.BlockSpec((tm, D), lambda i: (i, 0)),
            pl.BlockSpec((1, wa), lambda i: (0, 0)),
            pl.BlockSpec((1, wl), lambda i: (0, 0)),
            pl.BlockSpec((wa + wl, D), lambda i: (0, 0)),
            pl.BlockSpec((1, D), lambda i: (0, 0)),
        ],
        out_specs=[pl.BlockSpec((tm, D), lambda i: (i, 0)), pl.BlockSpec((tm, D), lambda i: (i, 0)),
                   pl.BlockSpec((tm, D // LANES, LANES), lambda i: (i, 0, 0))],
        out_shape=[jax.ShapeDtypeStruct((T, D), F32), jax.ShapeDtypeStruct((T, D), BF16),
                   jax.ShapeDtypeStruct((T, D // LANES, LANES), F32)],
        compiler_params=_cparams(("parallel",)),
        name="outproj",
    )(y_attn, y_lru, xt, g_attn, g_lru, w_out_bf, g_ffn)


def _pq_kernel(xn_ref, w_ref, k_ref, s_ref):
    q = jnp.dot(xn_ref[...], w_ref[...], preferred_element_type=F32).astype(BF16)
    for hh in range(2 * PEER_HEADS):
        half, head = divmod(hh, PEER_HEADS)
        col = (head * 2 + half) * PEER_HALF
        qh = q[:, col:col + PEER_HALF]
        s_ref[hh] = lax.dot_general(k_ref[hh], qh, (((1,), (1,)), ((), ())), preferred_element_type=F32)


def _peer_scores(xn, w_pq_bf, keys_bf):
    T, D = xn.shape
    tm = PQ_TM
    nh = keys_bf.shape[0]
    return pl.pallas_call(
        _pq_kernel,
        grid=(T // tm,),
        in_specs=[
            pl.BlockSpec((tm, D), lambda i: (i, 0)),
            pl.BlockSpec(w_pq_bf.shape, lambda i: (0, 0)),
            pl.BlockSpec(keys_bf.shape, lambda i: (0, 0, 0)),
        ],
        out_specs=pl.BlockSpec((nh, N_KEYS, tm), lambda i: (0, 0, i)),
        out_shape=jax.ShapeDtypeStruct((nh, N_KEYS, T), F32),
        compiler_params=_cparams(("parallel",)),
        name="pq",
    )(xn, w_pq_bf, keys_bf)


def _top16_rows(s, payload=None):
    n, tl = s.shape
    row = lax.broadcasted_iota(jnp.int32, (n, tl), 0)
    vals, idxs, pays = [], [], []
    for _ in range(PEER_TOPK):
        m = jnp.max(s, axis=0, keepdims=True)
        idx = jnp.min(jnp.where(s == m, row, n), axis=0, keepdims=True)
        hit = row == idx
        vals.append(m)
        idxs.append(idx)
        if payload is not None:
            pays.append(jnp.max(jnp.where(hit, payload, -1), axis=0, keepdims=True))
        s = jnp.where(hit, -jnp.inf, s)
    cat = lambda xs: jnp.concatenate(xs, axis=0)
    return cat(vals), cat(idxs), (cat(pays) if payload is not None else None)


def _topk_kernel(s1_ref, s2_ref, g_ref, e_ref):
    v1, i1, _ = _top16_rows(s1_ref[0])
    v2, i2, _ = _top16_rows(s2_ref[0])
    K = PEER_TOPK
    tl = v1.shape[1]
    sub = lax.broadcasted_iota(jnp.int32, (SUBLANES, tl), 0)
    cand, cidx = [], []
    for a in range(K // 2):
        n_b = K // (a + 1)
        for b0 in range(0, n_b, SUBLANES):
            keep = sub < (n_b - b0)
            cand.append(jnp.where(keep, v1[a:a + 1, :] + v2[b0:b0 + SUBLANES, :], -jnp.inf))
            cidx.append(i1[a:a + 1, :] * N_KEYS + i2[b0:b0 + SUBLANES, :])
    cand.append(v1[K // 2:, :] + v2[0:1, :])
    cidx.append(i1[K // 2:, :] * N_KEYS + i2[0:1, :])
    sc, _, eidx = _top16_rows(jnp.concatenate(cand, axis=0), jnp.concatenate(cidx, axis=0))
    ex = jnp.exp(sc - sc[0:1, :])
    g_ref[0] = ex / jnp.sum(ex, axis=0, keepdims=True)
    e_ref[0] = eidx


def _peer_topk(scores):
    nh2, nk, T = scores.shape
    nh = nh2 // 2
    tl = TOPK_TL
    return pl.pallas_call(
        _topk_kernel,
        grid=(nh, T // tl),
        in_specs=[
            pl.BlockSpec((1, nk, tl), lambda h, i: (h, 0, i)),
            pl.BlockSpec((1, nk, tl), lambda h, i: (nh + h, 0, i)),
        ],
        out_specs=[
            pl.BlockSpec((1, PEER_TOPK, tl), lambda h, i: (h, 0, i)),
            pl.BlockSpec((1, PEER_TOPK, tl), lambda h, i: (h, 0, i)),
        ],
        out_shape=[
            jax.ShapeDtypeStruct((nh, PEER_TOPK, T), F32),
            jax.ShapeDtypeStruct((nh, PEER_TOPK, T), jnp.int32),
        ],
        compiler_params=_cparams(("parallel", "parallel")),
        name="topk",
    )(scores, scores)


ROW_SUBLANES = 16


def _pack_rows(emb):
    n, d = emb.shape
    half = d // 2
    assert half == SUBLANES * LANES
    bits = lax.bitcast_convert_type(emb.astype(BF16), jnp.uint16).astype(jnp.uint32)
    return (bits[:, :half] | (bits[:, half:] << 16)).reshape(n * SUBLANES, LANES)


def _unpack_rows(word):
    return pltpu.bitcast(word << 16, F32), pltpu.bitcast(word & jnp.uint32(0xFFFF0000), F32)


def _sublane_fold(parts):
    sub = lax.broadcasted_iota(jnp.int32, (SUBLANES, LANES), 0)
    step = 1
    while len(parts) > 1:
        low = (sub & step) == 0
        parts = [jnp.where(low, a, b) + pltpu.roll(jnp.where(low, b, a), step, 0)
                 for a, b in zip(parts[0::2], parts[1::2])]
        step *= 2
    return parts[0]


def _chunk_list_stride(chunks_per_token):
    return -(-PEER_TM * chunks_per_token // 1024) * 1024


def _route(eidx, gates, n_blocks):
    T, P = eidx.shape
    u32 = jnp.uint32
    blocks = jnp.arange(n_blocks, dtype=jnp.int32)
    blk = eidx // EXPERT_BLOCK
    cnt = jnp.sum((blk[:, :, None] == blocks[None, None, :]).astype(jnp.int32), axis=1)
    pcnt = (cnt + CHUNK - 1) // CHUNK * CHUNK
    cand = jnp.arange(CHUNK, dtype=jnp.int32)
    active = cand[None, None, :] < (pcnt - cnt)[:, :, None]
    gate_bits = (lax.bitcast_convert_type(gates, u32) + u32(1 << (GATE_SHIFT - 1))) >> GATE_SHIFT
    pad_mark = u32((1 << GATE_BITS) - 1)
    word = (eidx.astype(u32) << GATE_BITS) | gate_bits
    last_row = ((blocks + 1) * EXPERT_BLOCK - 1).astype(u32)
    pad_word = jnp.where(active, ((last_row << GATE_BITS) | pad_mark)[None, :, None], u32(0xFFFFFFFF))
    srt = lax.sort(jnp.concatenate([word, pad_word.reshape(T, n_blocks * CHUNK)], axis=1), dimension=1)
    is_pad = (srt & pad_mark) == pad_mark
    loc = jnp.where(is_pad, u32(0), ((srt >> GATE_BITS) & u32(EXPERT_BLOCK - 1)) * SUBLANES)
    gate = lax.bitcast_convert_type(jnp.where(is_pad, u32(0), (srt & pad_mark) << GATE_SHIFT), F32)
    ends = jnp.cumsum(pcnt, axis=1) // CHUNK
    cs = jnp.concatenate([jnp.zeros((T, 1), jnp.int32), ends,
                          jnp.zeros((T, CS_STRIDE - n_blocks - 1), jnp.int32)], axis=1)
    n_chunks = jnp.sum((ends - cs[:, :n_blocks]).reshape(T // PEER_TM, PEER_TM, n_blocks), axis=1)

    per_tok = (P + n_blocks * CHUNK) // CHUNK
    slot_j = jnp.arange(per_tok, dtype=jnp.int32)
    slot_blk = jnp.sum((slot_j[None, :, None] >= ends[:, None, :]).astype(jnp.int32), axis=2)
    tok = (jnp.arange(T, dtype=jnp.int32) % PEER_TM)[:, None]
    entry = (tok << 16) | (tok * (per_tok * CHUNK) + slot_j[None, :] * CHUNK)
    assert PEER_TM <= 1 << 8
    order = lax.sort(((slot_blk << 24) | entry).reshape(T // PEER_TM, PEER_TM * per_tok), dimension=1)
    clist = order & ((1 << 24) - 1)
    clist = jnp.pad(clist, ((0, 0), (0, _chunk_list_stride(per_tok) - PEER_TM * per_tok)))

    n_tiles, per_tile = T // PEER_TM, PEER_TM * per_tok
    gate_tile = gate.reshape(n_tiles, PEER_TM, per_tok * CHUNK)
    q_pos = jnp.arange(per_tile, dtype=jnp.int32)
    gates_c = []
    for b in range(n_blocks):
        lst = lax.sort((((slot_blk != b).astype(jnp.int32) << 24) | entry).reshape(n_tiles, per_tile), dimension=1)
        tok_q = (lst >> 16) & 0xFF
        j_q = ((lst & 0xFFFF) - tok_q * (per_tok * CHUNK)) // CHUNK
        tok_hot = (tok_q[:, :, None] == jnp.arange(PEER_TM, dtype=jnp.int32)[None, None, :]).astype(F32)
        rows = jnp.einsum('xqt,xtm->xqm', tok_hot, gate_tile, precision=lax.Precision.HIGHEST)
        pick = (j_q[:, :, None] == slot_j[None, None, :]) & (q_pos[None, :, None] < n_chunks[:, b][:, None, None])
        rows = rows.reshape(n_tiles, per_tile, per_tok, CHUNK)
        gates_c.append(jnp.sum(jnp.where(pick[..., None], rows, 0.0), axis=2))
    gates_c = jnp.stack(gates_c).reshape(n_blocks, n_tiles, per_tile * CHUNK // LANES, LANES)
    return loc.astype(jnp.int32).reshape(-1), gates_c, n_chunks.T.reshape(-1), clist.reshape(-1)


def _pdot_kernel(nch_ref, cl_ref, idx_ref, x_ref, u_ref, d_ref, rbuf_ref, *, n_blocks):
    b = pl.program_id(0)
    i = pl.program_id(1)
    n_tiles = pl.num_programs(1)
    n_chunks = nch_ref[b * n_tiles + i]
    first = jnp.int32(0)
    for bb in range(n_blocks - 1):
        first = first + jnp.where(bb < b, nch_ref[bb * n_tiles + i], 0)
    zero_rows = jnp.zeros((SUBLANES, LANES), F32)
    per_group = LANES // CHUNK
    n_full = n_chunks // per_group
    n_groups = (n_chunks + per_group - 1) // per_group
    dummy_row = d_ref.shape[0] - SUBLANES

    rbuf_ref[0:LANES, :] = jnp.zeros((LANES, LANES), F32)
    d_ref[...] = jnp.zeros(d_ref.shape, F32)

    def chunk(q):
        entry = cl_ref[first + q]
        base = entry & 0xFFFF
        x = x_ref[entry >> 16]
        x_lo, x_hi = x[:SUBLANES], x[SUBLANES:]
        for h in range(CHUNK // SUBLANES):
            parts = []
            for k in range(SUBLANES):
                row = pl.multiple_of(idx_ref[base + h * SUBLANES + k], SUBLANES)
                lo, hi = _unpack_rows(u_ref[pl.ds(row, SUBLANES), :])
                parts.append(lo * x_lo + hi * x_hi)
            row0 = pl.multiple_of(LANES + q * CHUNK + h * SUBLANES, SUBLANES)
            rbuf_ref[pl.ds(row0, SUBLANES), :] = _sublane_fold(parts)

    def reduce_rows(first_row, out_row):
        r = rbuf_ref[pl.ds(pl.multiple_of(first_row, LANES), LANES), :]
        d_ref[pl.ds(out_row, 1), :] = jnp.sum(r.T, axis=0, keepdims=True)

    def group_body(g, carry):
        for c in range(per_group):
            chunk(g * per_group + c)
        reduce_rows(g * LANES, jnp.where(g == 0, dummy_row, g - 1))
        return carry

    lax.fori_loop(0, n_full, group_body, 0)

    def tail_body(q, carry):
        chunk(q)
        return carry

    lax.fori_loop(n_full * per_group, n_chunks, tail_body, 0)

    def zero_body(f, carry):
        rbuf_ref[pl.ds(pl.multiple_of(LANES + f * SUBLANES, SUBLANES), SUBLANES), :] = zero_rows
        return carry

    folds_per_chunk = CHUNK // SUBLANES
    lax.fori_loop(n_chunks * folds_per_chunk, n_groups * per_group * folds_per_chunk, zero_body, 0)

    def last_body(g, carry):
        reduce_rows((g + 1) * LANES, g)
        return carry

    lax.fori_loop(jnp.maximum(n_full - 1, 0), n_groups, last_body, 0)


def _expert_block_spec():
    return pl.BlockSpec((EXPERT_BLOCK * SUBLANES, LANES), lambda b, i: (b, 0), pipeline_mode=pl.Buffered(1))


def _peer_dots(n_chunks, clist, idx, x3, u_rows, slots):
    T = x3.shape[0]
    nb = u_rows.shape[0] // (EXPERT_BLOCK * SUBLANES)
    tm = PEER_TM
    n_tiles = T // tm
    max_pairs = tm * slots
    groups = max_pairs // LANES + SUBLANES
    smem_tile = lambda n: pl.BlockSpec((tm * n,), lambda b, i: (i,), memory_space=pltpu.SMEM)
    out = jax.ShapeDtypeStruct((nb, n_tiles, groups, LANES), F32)
    out_spec = pl.BlockSpec((None, None, groups, LANES), lambda b, i: (b, i, 0, 0))
    return pl.pallas_call(
        functools.partial(_pdot_kernel, n_blocks=nb),
        grid=(nb, n_tiles),
        in_specs=[
            pl.BlockSpec(memory_space=pltpu.SMEM),
            pl.BlockSpec((_chunk_list_stride(slots // CHUNK),), lambda b, i: (i,), memory_space=pltpu.SMEM),
            smem_tile(slots),
            pl.BlockSpec((tm, ROW_SUBLANES, LANES), lambda b, i: (i, 0, 0)),
            _expert_block_spec(),
        ],
        out_specs=out_spec,
        out_shape=out,
        scratch_shapes=[pltpu.VMEM((LANES + max_pairs, LANES), F32)],
        compiler_params=_cparams(("arbitrary", "arbitrary")),
        name="pdot",
    )(n_chunks, clist, idx, x3, u_rows)


def _act_kernel(d_ref, g_ref, w_ref):
    w_ref[...] = g_ref[...] * jax.nn.gelu(d_ref[...])


def _peer_act(dots, gates):
    steps, groups, lanes = dots.shape
    per = max(k for k in (16, 8, 4, 2, 1) if steps % k == 0)
    spec = pl.BlockSpec((per, groups, lanes), lambda i: (i, 0, 0))
    return pl.pallas_call(
        _act_kernel,
        grid=(steps // per,),
        in_specs=[spec, spec],
        out_specs=spec,
        out_shape=jax.ShapeDtypeStruct(dots.shape, F32),
        compiler_params=_cparams(("parallel",)),
        name="act",
    )(dots, gates)


def _pacc_kernel(nch_ref, cl_ref, idx_ref, w_ref, v_ref, o_ref, wrep_ref, *, n_blocks):
    b = pl.program_id(0)
    i = pl.program_id(1)
    n_tiles = pl.num_programs(1)
    n_chunks = nch_ref[b * n_tiles + i]
    first = jnp.int32(0)
    for bb in range(n_blocks - 1):
        first = first + jnp.where(bb < b, nch_ref[bb * n_tiles + i], 0)

    per_group = LANES // CHUNK
    n_full = n_chunks // per_group

    def expand(g):
        rows = jnp.broadcast_to(w_ref[pl.ds(g, 1), :], (LANES, LANES))
        wrep_ref[pl.ds(pl.multiple_of(g * LANES, LANES), LANES), :] = rows.T

    expand(0)
    o_ref[...] = jnp.zeros(o_ref.shape, F32)

    def chunk(q):
        entry = cl_ref[first + q]
        base = entry & 0xFFFF
        tok = entry >> 16
        zero = jnp.zeros((SUBLANES, LANES), F32)
        accs = [zero, zero, zero, zero]
        for k in range(CHUNK):
            w = jnp.broadcast_to(wrep_ref[pl.ds(q * CHUNK + k, 1), :], (SUBLANES, LANES))
            lo, hi = _unpack_rows(v_ref[pl.ds(pl.multiple_of(idx_ref[base + k], SUBLANES), SUBLANES), :])
            j = 2 * (k % 2)
            accs[j] = accs[j] + w * lo
            accs[j + 1] = accs[j + 1] + w * hi
        o_ref[tok] = o_ref[tok] + jnp.concatenate([accs[0] + accs[2], accs[1] + accs[3]], axis=0)

    def group_body(g, carry):
        expand(g + 1)
        for c in range(per_group):
            chunk(g * per_group + c)
        return carry

    lax.fori_loop(0, n_full, group_body, 0)

    def tail_body(q, carry):
        chunk(q)
        return carry

    lax.fori_loop(n_full * per_group, n_chunks, tail_body, 0)


def _peer_accumulate(n_chunks, clist, idx, w, v_rows, T, slots):
    nb = v_rows.shape[0] // (EXPERT_BLOCK * SUBLANES)
    tm = PEER_TM
    n_tiles = T // tm
    groups = w.shape[2]
    assert groups == tm * slots // LANES + SUBLANES
    smem_tile = lambda n: pl.BlockSpec((tm * n,), lambda b, i: (i,), memory_space=pltpu.SMEM)
    return pl.pallas_call(
        functools.partial(_pacc_kernel, n_blocks=nb),
        grid=(nb, n_tiles),
        in_specs=[
            pl.BlockSpec(memory_space=pltpu.SMEM),
            pl.BlockSpec((_chunk_list_stride(slots // CHUNK),), lambda b, i: (i,), memory_space=pltpu.SMEM),
            smem_tile(slots),
            pl.BlockSpec((None, None, groups, LANES), lambda b, i: (b, i, 0, 0)),
            _expert_block_spec(),
        ],
        out_specs=pl.BlockSpec((None, tm, ROW_SUBLANES, LANES), lambda b, i: (b, i, 0, 0)),
        out_shape=jax.ShapeDtypeStruct((nb, T, ROW_SUBLANES, LANES), F32),
        scratch_shapes=[pltpu.VMEM((tm * slots + LANES, LANES), F32)],
        compiler_params=_cparams(("arbitrary", "arbitrary")),
        name="pacc",
    )(n_chunks, clist, idx, w, v_rows)


def _final_kernel(x1_ref, p_ref, g_ref, o_ref):
    nb, _, n_s, _ = p_ref.shape
    cols = []
    for s in range(n_s):
        col = x1_ref[:, s * LANES:(s + 1) * LANES]
        for b in range(nb):
            col = col + p_ref[b, :, s, :]
        cols.append(col)
    o_ref[...] = _rms(jnp.concatenate(cols, axis=1), g_ref[...])


def _final(x1, peer_parts, g_final):
    T, D = x1.shape
    nb = peer_parts.shape[0]
    tm = 256
    return pl.pallas_call(
        _final_kernel,
        grid=(T // tm,),
        in_specs=[
            pl.BlockSpec((tm, D), lambda i: (i, 0)),
            pl.BlockSpec((nb, tm, D // LANES, LANES), lambda i: (0, i, 0, 0)),
            pl.BlockSpec((1, D), lambda i: (0, 0)),
        ],
        out_specs=pl.BlockSpec((tm, D), lambda i: (i, 0)),
        out_shape=jax.ShapeDtypeStruct((T, D), F32),
        compiler_params=_cparams(("parallel",)),
        name="final",
    )(x1, peer_parts, g_final)


def _rope_tables(seq):
    half = HEAD_DIM // 2
    inv = ROPE_THETA ** (-jnp.arange(half, dtype=F32) / half)
    ang = jnp.arange(seq).astype(F32)[:, None] * inv[None, :]
    cos, sin = jnp.cos(ang), jnp.sin(ang)
    reps = LANES // HEAD_DIM
    cos_t = jnp.concatenate([cos, cos] * reps, axis=1)
    sin_t = jnp.concatenate([-sin, sin] * reps, axis=1)
    return cos_t, sin_t


def _layer(xt, batch, seq, g_mix, w_in, sink, conv_w, conv_b, fwd_wa, fwd_ba, fwd_wx, fwd_bx, fwd_lam,
           bwd_wa, bwd_ba, bwd_wx, bwd_bx, bwd_lam, g_attn_out, g_lru_out, w_out, g_ffn,
           w_pq, sub_k1, sub_k2, u_emb, v_emb):
    T, D = xt.shape
    cos_t, sin_t = _rope_tables(seq)
    qkv, gr = _inproj(xt, g_mix.reshape(1, D), w_in.astype(BF16), cos_t, sin_t, seq)
    y_attn = _attention(qkv, sink, batch, seq)
    y_lru = _lru(gr, conv_w, conv_b, (fwd_wa, fwd_wx, bwd_wa, bwd_wx), (fwd_ba, fwd_bx, bwd_ba, bwd_bx),
                 (fwd_lam, bwd_lam), batch, seq)
    x1, xn, x3 = _outproj(y_attn, y_lru, xt, g_attn_out.reshape(1, -1), g_lru_out.reshape(1, -1),
                      w_out.astype(BF16), g_ffn.reshape(1, D))

    keys = jnp.concatenate([sub_k1, sub_k2], axis=0).astype(BF16)
    scores = _peer_scores(xn, w_pq.astype(BF16), keys)
    gates_t, eidx_t = _peer_topk(scores)
    n_pairs = PEER_HEADS * PEER_TOPK
    gates = gates_t.transpose(2, 0, 1).reshape(T, n_pairs)
    eidx = eidx_t.transpose(2, 0, 1).reshape(T, n_pairs)
    n_exp = u_emb.shape[0]
    nb = n_exp // EXPERT_BLOCK
    slots = n_pairs + nb * CHUNK
    idx, gates_c, n_chunks, clist = _route(eidx, gates, nb)

    u_rows = _pack_rows(u_emb)
    v_rows = _pack_rows(v_emb)
    dots = _peer_dots(n_chunks, clist, idx, x3, u_rows, slots)
    gates_c = jnp.pad(gates_c, ((0, 0), (0, 0), (0, dots.shape[2] - gates_c.shape[2]), (0, 0)))
    flat = (-1,) + dots.shape[2:]
    w = _peer_act(dots.reshape(flat), gates_c.reshape(flat)).reshape(dots.shape)
    parts = _peer_accumulate(n_chunks, clist, idx, w, v_rows, T, slots)
    return x1, parts


def kernel(x, g_mix, w_in, sink, conv_w, conv_b, fwd_wa, fwd_ba, fwd_wx, fwd_bx, fwd_lam, bwd_wa, bwd_ba, bwd_wx, bwd_bx, bwd_lam, g_attn_out, g_lru_out, w_out, g_ffn, w_pq, sub_k1, sub_k2, u_emb, v_emb, g_final):
    B, S, D = x.shape
    assert g_mix.shape[0] == 1, "single-layer trunk"
    xt = x.reshape(B * S, D)
    x1, parts = _layer(xt, B, S, g_mix[0], w_in[0], sink[0], conv_w[0], conv_b[0],
                       fwd_wa[0], fwd_ba[0], fwd_wx[0], fwd_bx[0], fwd_lam[0],
                       bwd_wa[0], bwd_ba[0], bwd_wx[0], bwd_bx[0], bwd_lam[0],
                       g_attn_out[0], g_lru_out[0], w_out[0], g_ffn[0],
                       w_pq[0], sub_k1[0], sub_k2[0], u_emb[0], v_emb[0])
    return _final(x1, parts, g_final.reshape(1, D)).reshape(B, S, D)
```

```python
import functools

import jax
import jax.numpy as jnp
from jax import lax
from jax.experimental import pallas as pl
from jax.experimental.pallas import tpu as pltpu

F32 = jnp.float32
BF16 = jnp.bfloat16

HEAD_DIM = 64
N_KV_HEADS = 4
Q_PER_KV = 4
N_Q_HEADS = N_KV_HEADS * Q_PER_KV
WINDOW = 128
ATTN_BLOCK = 128
ROPE_THETA = 10000.0
LRU_BLOCK = 64
CONV_WIDTH = 4
CONV_PAD_LEFT = 2
LRU_C = 8.0
PEER_HEADS = 8
PEER_HALF = 128
N_KEYS = 128
PEER_TOPK = 16
EPS = 1e-6
NEG = -1e30

LANES = 128
SUBLANES = 8
VMEM_LIMIT = 56 * 1024 * 1024

INPROJ_TM = 512
INPROJ_TN = 512
LRU_CHANNELS = 256
LRU_CHUNK = 256
OUTPROJ_TM = 256
PQ_TM = 256
TOPK_TL = 512
EXPERT_BLOCK = 8192
PEER_TM = 128
CHUNK = 16
GATE_BITS = 18
GATE_SHIFT = 31 - GATE_BITS


def _cparams(sem):
    return pltpu.CompilerParams(dimension_semantics=sem, vmem_limit_bytes=VMEM_LIMIT)


def _rms(x, g):
    return x * lax.rsqrt(jnp.mean(x * x, axis=-1, keepdims=True) + EPS) * g


def _inproj_kernel(x_ref, g_ref, w_ref, cos_ref, sin_ref, qkv_ref, gr_ref, h_ref, *, n_rope_tiles, kv_cols):
    j = pl.program_id(1)

    @pl.when(j == 0)
    def _():
        h_ref[...] = _rms(x_ref[...], g_ref[...]).astype(BF16)

    acc = jnp.dot(h_ref[...], w_ref[...], preferred_element_type=F32)
    tm, tn = acc.shape

    @pl.when(j < n_rope_tiles)
    def _():
        reps = tn // LANES
        cos = jnp.concatenate([cos_ref[...]] * reps, axis=1)
        sin = jnp.concatenate([sin_ref[...]] * reps, axis=1)
        lane = lax.broadcasted_iota(jnp.int32, (tm, tn), 1)
        first = (lane % HEAD_DIM) < (HEAD_DIM // 2)
        partner = jnp.where(first, pltpu.roll(acc, tn - HEAD_DIM // 2, 1), pltpu.roll(acc, HEAD_DIM // 2, 1))
        roped = acc * cos + partner * sin
        is_rope = jnp.logical_or(j < n_rope_tiles - 1, lane < kv_cols)
        qkv_ref[...] = jnp.where(is_rope, roped, acc).astype(BF16)

    @pl.when(j >= n_rope_tiles)
    def _():
        gr_ref[...] = acc


def _inproj(xt, g_mix, w_in_bf, cos_t, sin_t, seq):
    T, D = xt.shape
    n_cols = w_in_bf.shape[1]
    q_cols = N_Q_HEADS * HEAD_DIM
    kv_cols = N_KV_HEADS * HEAD_DIM
    qkv_cols = q_cols + 2 * kv_cols
    tm, tn = INPROJ_TM, INPROJ_TN
    n_rope_tiles = qkv_cols // tn
    assert qkv_cols % tn == 0 and (q_cols % tn == 0) and seq % tm == 0 and T % tm == 0
    pos_blocks = seq // tm
    kern = functools.partial(_inproj_kernel, n_rope_tiles=n_rope_tiles, kv_cols=kv_cols)
    return pl.pallas_call(
        kern,
        grid=(T // tm, n_cols // tn),
        in_specs=[
            pl.BlockSpec((tm, D), lambda i, j: (i, 0)),
            pl.BlockSpec((1, D), lambda i, j: (0, 0)),
            pl.BlockSpec((D, tn), lambda i, j: (0, j)),
            pl.BlockSpec((tm, LANES), lambda i, j: (i % pos_blocks, 0)),
            pl.BlockSpec((tm, LANES), lambda i, j: (i % pos_blocks, 0)),
        ],
        out_specs=[
            pl.BlockSpec((tm, tn), lambda i, j: (i, jnp.minimum(j, n_rope_tiles - 1))),
            pl.BlockSpec((tm, tn), lambda i, j: (i, jnp.maximum(j - n_rope_tiles, 0))),
        ],
        out_shape=[
            jax.ShapeDtypeStruct((T, qkv_cols), BF16),
            jax.ShapeDtypeStruct((T, n_cols - qkv_cols), F32),
        ],
        scratch_shapes=[pltpu.VMEM((tm, D), BF16)],
        compiler_params=_cparams(("parallel", "arbitrary")),
        name="inproj",
    )(xt, g_mix, w_in_bf, cos_t, sin_t)


def _attn_kernel(sink_ref, q_ref, kp_ref, kc_ref, kn_ref, vp_ref, vc_ref, vn_ref, o_ref, *, seq):
    n = pl.program_id(1)
    q = q_ref[...] * jnp.asarray(HEAD_DIM ** -0.5, BF16)
    k = jnp.concatenate([kp_ref[...], kc_ref[...], kn_ref[...]], axis=0)
    v = jnp.concatenate([vp_ref[...], vc_ref[...], vn_ref[...]], axis=0)
    nq, nk = ATTN_BLOCK, 3 * ATTN_BLOCK
    qpos = n * ATTN_BLOCK + lax.broadcasted_iota(jnp.int32, (nq, nk), 0)
    kpos = (n - 1) * ATTN_BLOCK + lax.broadcasted_iota(jnp.int32, (nq, nk), 1)
    valid = (jnp.abs(kpos - qpos) <= WINDOW) & (kpos >= 0) & (kpos < seq)
    outs = []
    for g in range(N_KV_HEADS):
        kg = k[:, g * HEAD_DIM:(g + 1) * HEAD_DIM]
        vg = v[:, g * HEAD_DIM:(g + 1) * HEAD_DIM]
        for r in range(Q_PER_KV):
            h = g * Q_PER_KV + r
            qh = q[:, h * HEAD_DIM:(h + 1) * HEAD_DIM]
            s = lax.dot_general(qh, kg, (((1,), (1,)), ((), ())), preferred_element_type=F32)
            s = jnp.where(valid, s, NEG)
            sk = sink_ref[h]
            m = jnp.maximum(jnp.max(s, axis=-1, keepdims=True), sk)
            p = jnp.exp(s - m)
            den = jnp.sum(p, axis=-1, keepdims=True) + jnp.exp(sk - m)
            o = jnp.dot(p.astype(BF16), vg, preferred_element_type=F32) / den
            outs.append(o)
    o_ref[...] = jnp.concatenate(outs, axis=1)


def _attention(qkv, sink, batch, seq):
    T = qkv.shape[0]
    nb = seq // ATTN_BLOCK
    q_cols = N_Q_HEADS * HEAD_DIM
    kv_cols = N_KV_HEADS * HEAD_DIM
    k_blk = q_cols // kv_cols
    v_blk = k_blk + 1
    blk = ATTN_BLOCK

    def row(b, n):
        return b * nb + n

    def kv_spec(col_blk, shift):
        return pl.BlockSpec((blk, kv_cols), lambda b, n: (row(b, jnp.clip(n + shift, 0, nb - 1)), col_blk))

    return pl.pallas_call(
        functools.partial(_attn_kernel, seq=seq),
        grid=(batch, nb),
        in_specs=[
            pl.BlockSpec(memory_space=pltpu.SMEM),
            pl.BlockSpec((blk, q_cols), lambda b, n: (row(b, n), 0)),
            kv_spec(k_blk, -1), kv_spec(k_blk, 0), kv_spec(k_blk, 1),
            kv_spec(v_blk, -1), kv_spec(v_blk, 0), kv_spec(v_blk, 1),
        ],
        out_specs=pl.BlockSpec((blk, q_cols), lambda b, n: (row(b, n), 0)),
        out_shape=jax.ShapeDtypeStruct((T, q_cols), F32),
        compiler_params=_cparams(("parallel", "parallel")),
        name="attn",
    )(sink, qkv, qkv, qkv, qkv, qkv, qkv, qkv)


def _scan_chunk(a, u, reverse):
    L, C = a.shape
    row = lax.broadcasted_iota(jnp.int32, (L, C), 0)
    A, H = a, u
    d = 1
    while d < L:
        if d < SUBLANES:
            if reverse:
                keep = row < L - d
                As = jnp.where(keep, pltpu.roll(A, L - d, 0), 1.0)
                Hs = jnp.where(keep, pltpu.roll(H, L - d, 0), 0.0)
            else:
                keep = row >= d
                As = jnp.where(keep, pltpu.roll(A, d, 0), 1.0)
                Hs = jnp.where(keep, pltpu.roll(H, d, 0), 0.0)
        else:
            one = jnp.ones((d, C), F32)
            zero = jnp.zeros((d, C), F32)
            if reverse:
                As = jnp.concatenate([A[d:], one], axis=0)
                Hs = jnp.concatenate([H[d:], zero], axis=0)
            else:
                As = jnp.concatenate([one, A[:L - d]], axis=0)
                Hs = jnp.concatenate([zero, H[:L - d]], axis=0)
        H = A * Hs + H
        A = A * As
        d *= 2
    return A, H


def _lru_kernel(xg_ref, xr_ref, cw_ref, cb_ref, wg_ref, bg_ref, lam_ref, y_ref, xp_ref, hf_ref):
    S, C = xr_ref.shape
    L = LRU_CHUNK
    n_chunks = S // L
    halo = SUBLANES

    xp_ref[0:halo, :] = jnp.zeros((halo, C), F32)
    xp_ref[S + halo:S + 2 * halo, :] = jnp.zeros((halo, C), F32)

    def copy_body(ci, carry):
        t0 = pl.multiple_of(ci * L, L)
        xp_ref[pl.ds(t0 + halo, L), :] = xr_ref[pl.ds(t0, L), :]
        return carry

    lax.fori_loop(0, n_chunks, copy_body, 0)

    cw = cw_ref[...]
    cb = cb_ref[...]
    bias = bg_ref[0]
    neg_c_softplus = -LRU_C * jax.nn.softplus(-lam_ref[0])

    def conv_chunk(t0):
        win = xp_ref[pl.ds(t0, L + 2 * halo), :]
        acc = cb
        for j in range(CONV_WIDTH):
            off = halo - CONV_PAD_LEFT + j
            acc = acc + cw[j:j + 1, :] * win[off:off + L, :]
        return acc

    def gate_au(c, direction):
        w = wg_ref[0, :, direction * 2 * C:(direction + 1) * 2 * C]
        z = jnp.dot(c.astype(BF16), w, preferred_element_type=F32) + bias[:, direction * 2 * C:(direction + 1) * 2 * C]
        r = jax.nn.sigmoid(z[:, :C])
        i = jax.nn.sigmoid(z[:, C:])
        log_a = r * neg_c_softplus[direction:direction + 1, :]
        a = jnp.exp(log_a)
        u = jnp.sqrt(1.0 - a * a) * (i * c)
        return a, u

    def fwd_body(ci, carry):
        t0 = pl.multiple_of(ci * L, L)
        c = conv_chunk(t0)
        a, u = gate_au(c, 0)
        A, H = _scan_chunk(a, u, reverse=False)
        h = H + A * carry
        hf_ref[pl.ds(t0, L), :] = h
        return h[L - 1:L, :]

    lax.fori_loop(0, n_chunks, fwd_body, jnp.zeros((1, C), F32))

    def bwd_body(k, carry):
        t0 = pl.multiple_of((n_chunks - 1 - k) * L, L)
        c = conv_chunk(t0)
        a, u = gate_au(c, 1)
        A, H = _scan_chunk(a, u, reverse=True)
        h = H + A * carry
        y_ref[pl.ds(t0, L), :] = jax.nn.gelu(xg_ref[pl.ds(t0, L), :]) * (hf_ref[pl.ds(t0, L), :] + h)
        return h[0:1, :]

    lax.fori_loop(0, n_chunks, bwd_body, jnp.zeros((1, C), F32))


def _block_diag_chunks(w, per_chunk):
    nblk, b, _ = w.shape
    w4 = w.reshape(nblk // per_chunk, per_chunk, b, b)
    eye = jnp.eye(per_chunk, dtype=w.dtype)
    m = w4[:, :, :, None, :] * eye[None, :, None, :, None]
    return m.reshape(nblk // per_chunk, per_chunk * b, per_chunk * b)


def _lru(gr, conv_w, conv_b, gate_ws, gate_bs, lams, batch, seq):
    T, two_w = gr.shape
    W = two_w // 2
    C = LRU_CHANNELS
    n_ch = W // C
    per_chunk = C // LRU_BLOCK
    wg = jnp.concatenate([_block_diag_chunks(w, per_chunk) for w in gate_ws], axis=-1).astype(BF16)
    bg = jnp.concatenate([b.reshape(n_ch, 1, C) for b in gate_bs], axis=-1)
    lam = jnp.stack([l.reshape(n_ch, C) for l in lams], axis=1)
    return pl.pallas_call(
        _lru_kernel,
        grid=(batch, n_ch),
        in_specs=[
            pl.BlockSpec((seq, C), lambda b, c: (b, c)),
            pl.BlockSpec((seq, C), lambda b, c: (b, n_ch + c)),
            pl.BlockSpec((CONV_WIDTH, C), lambda b, c: (0, c)),
            pl.BlockSpec((1, C), lambda b, c: (0, c)),
            pl.BlockSpec((1, C, 4 * C), lambda b, c: (c, 0, 0)),
            pl.BlockSpec((1, 1, 4 * C), lambda b, c: (c, 0, 0)),
            pl.BlockSpec((1, 2, C), lambda b, c: (c, 0, 0)),
        ],
        out_specs=pl.BlockSpec((seq, C), lambda b, c: (b, c)),
        out_shape=jax.ShapeDtypeStruct((T, W), F32),
        scratch_shapes=[pltpu.VMEM((seq + 2 * SUBLANES, C), F32), pltpu.VMEM((seq, C), F32)],
        compiler_params=_cparams(("parallel", "parallel")),
        name="lru",
    )(gr, gr, conv_w, conv_b.reshape(1, W), wg, bg, lam)


def _outproj_kernel(ya_ref, yl_ref, x_ref, ga_ref, gl_ref, w_ref, gf_ref, x1_ref, xn_ref, xs_ref):
    y = jnp.concatenate([_rms(ya_ref[...], ga_ref[...]), _rms(yl_ref[...], gl_ref[...])], axis=1).astype(BF16)
    x1 = x_ref[...] + jnp.dot(y, w_ref[...], preferred_element_type=F32)
    x1_ref[...] = x1
    xn = _rms(x1, gf_ref[...])
    xn_ref[...] = xn.astype(BF16)
    for s in range(xs_ref.shape[1]):
        xs_ref[:, s, :] = xn[:, s * LANES:(s + 1) * LANES]


def _outproj(y_attn, y_lru, xt, g_attn, g_lru, w_out_bf, g_ffn):
    T, D = xt.shape
    wa, wl = y_attn.shape[1], y_lru.shape[1]
    tm = OUTPROJ_TM
    return pl.pallas_call(
        _outproj_kernel,
        grid=(T // tm,),
        in_specs=[
            pl.BlockSpec((tm, wa), lambda i: (i, 0)),
            pl.BlockSpec((tm, wl), lambda i: (i, 0)),
            pl.BlockSpec((tm, D), lambda i: (i, 0)),
            pl.BlockSpec((1, wa), lambda i: (0, 0)),
            pl.BlockSpec((1, wl), lambda i: (0, 0)),
            pl.BlockSpec((wa + wl, D), lambda i: (0, 0)),
            pl.BlockSpec((1, D), lambda i: (0, 0)),
        ],
        out_specs=[pl.BlockSpec((tm, D), lambda i: (i, 0)), pl.BlockSpec((tm, D), lambda i: (i, 0)),
                   pl.BlockSpec((tm, D // LANES, LANES), lambda i: (i, 0, 0))],
        out_shape=[jax.ShapeDtypeStruct((T, D), F32), jax.ShapeDtypeStruct((T, D), BF16),
                   jax.ShapeDtypeStruct((T, D // LANES, LANES), F32)],
        compiler_params=_cparams(("parallel",)),
        name="outproj",
    )(y_attn, y_lru, xt, g_attn, g_lru, w_out_bf, g_ffn)


def _pq_kernel(xn_ref, w_ref, k_ref, s_ref):
    q = jnp.dot(xn_ref[...], w_ref[...], preferred_element_type=F32).astype(BF16)
    for hh in range(2 * PEER_HEADS):
        half, head = divmod(hh, PEER_HEADS)
        col = (head * 2 + half) * PEER_HALF
        qh = q[:, col:col + PEER_HALF]
        s_ref[hh] = lax.dot_general(k_ref[hh], qh, (((1,), (1,)), ((), ())), preferred_element_type=F32)


def _peer_scores(xn, w_pq_bf, keys_bf):
    T, D = xn.shape
    tm = PQ_TM
    nh = keys_bf.shape[0]
    return pl.pallas_call(
        _pq_kernel,
        grid=(T // tm,),
        in_specs=[
            pl.BlockSpec((tm, D), lambda i: (i, 0)),
            pl.BlockSpec(w_pq_bf.shape, lambda i: (0, 0)),
            pl.BlockSpec(keys_bf.shape, lambda i: (0, 0, 0)),
        ],
        out_specs=pl.BlockSpec((nh, N_KEYS, tm), lambda i: (0, 0, i)),
        out_shape=jax.ShapeDtypeStruct((nh, N_KEYS, T), F32),
        compiler_params=_cparams(("parallel",)),
        name="pq",
    )(xn, w_pq_bf, keys_bf)


def _top16_rows(s, payload=None):
    n, tl = s.shape
    row = lax.broadcasted_iota(jnp.int32, (n, tl), 0)
    vals, idxs, pays = [], [], []
    for _ in range(PEER_TOPK):
        m = jnp.max(s, axis=0, keepdims=True)
        idx = jnp.min(jnp.where(s == m, row, n), axis=0, keepdims=True)
        hit = row == idx
        vals.append(m)
        idxs.append(idx)
        if payload is not None:
            pays.append(jnp.max(jnp.where(hit, payload, -1), axis=0, keepdims=True))
        s = jnp.where(hit, -jnp.inf, s)
    cat = lambda xs: jnp.concatenate(xs, axis=0)
    return cat(vals), cat(idxs), (cat(pays) if payload is not None else None)


def _topk_kernel(s1_ref, s2_ref, g_ref, e_ref):
    v1, i1, _ = _top16_rows(s1_ref[0])
    v2, i2, _ = _top16_rows(s2_ref[0])
    K = PEER_TOPK
    tl = v1.shape[1]
    sub = lax.broadcasted_iota(jnp.int32, (SUBLANES, tl), 0)
    cand, cidx = [], []
    for a in range(K // 2):
        n_b = K // (a + 1)
        for b0 in range(0, n_b, SUBLANES):
            keep = sub < (n_b - b0)
            cand.append(jnp.where(keep, v1[a:a + 1, :] + v2[b0:b0 + SUBLANES, :], -jnp.inf))
            cidx.append(i1[a:a + 1, :] * N_KEYS + i2[b0:b0 + SUBLANES, :])
    cand.append(v1[K // 2:, :] + v2[0:1, :])
    cidx.append(i1[K // 2:, :] * N_KEYS + i2[0:1, :])
    sc, _, eidx = _top16_rows(jnp.concatenate(cand, axis=0), jnp.concatenate(cidx, axis=0))
    ex = jnp.exp(sc - sc[0:1, :])
    g_ref[0] = ex / jnp.sum(ex, axis=0, keepdims=True)
    e_ref[0] = eidx


def _peer_topk(scores):
    nh2, nk, T = scores.shape
    nh = nh2 // 2
    tl = TOPK_TL
    return pl.pallas_call(
        _topk_kernel,
        grid=(nh, T // tl),
        in_specs=[
            pl.BlockSpec((1, nk, tl), lambda h, i: (h, 0, i)),
            pl.BlockSpec((1, nk, tl), lambda h, i: (nh + h, 0, i)),
        ],
        out_specs=[
            pl.BlockSpec((1, PEER_TOPK, tl), lambda h, i: (h, 0, i)),
            pl.BlockSpec((1, PEER_TOPK, tl), lambda h, i: (h, 0, i)),
        ],
        out_shape=[
            jax.ShapeDtypeStruct((nh, PEER_TOPK, T), F32),
            jax.ShapeDtypeStruct((nh, PEER_TOPK, T), jnp.int32),
        ],
        compiler_params=_cparams(("parallel", "parallel")),
        name="topk",
    )(scores, scores)


ROW_SUBLANES = 16


def _pack_rows(emb):
    n, d = emb.shape
    half = d // 2
    assert half == SUBLANES * LANES
    bits = lax.bitcast_convert_type(emb.astype(BF16), jnp.uint16).astype(jnp.uint32)
    return (bits[:, :half] | (bits[:, half:] << 16)).reshape(n * SUBLANES, LANES)


def _unpack_rows(word):
    return pltpu.bitcast(word << 16, F32), pltpu.bitcast(word & jnp.uint32(0xFFFF0000), F32)


def _sublane_fold(parts):
    sub = lax.broadcasted_iota(jnp.int32, (SUBLANES, LANES), 0)
    step = 1
    while len(parts) > 1:
        low = (sub & step) == 0
        parts = [jnp.where(low, a, b) + pltpu.roll(jnp.where(low, b, a), step, 0)
                 for a, b in zip(parts[0::2], parts[1::2])]
        step *= 2
    return parts[0]


def _chunk_list_stride(chunks_per_token):
    return -(-PEER_TM * chunks_per_token // 1024) * 1024


def _route(eidx, gates, n_blocks):
    T, P = eidx.shape
    u32 = jnp.uint32
    blocks = jnp.arange(n_blocks, dtype=jnp.int32)
    blk = eidx // EXPERT_BLOCK
    cnt = jnp.sum((blk[:, :, None] == blocks[None, None, :]).astype(jnp.int32), axis=1)
    pcnt = (cnt + CHUNK - 1) // CHUNK * CHUNK
    gate_bits = (lax.bitcast_convert_type(gates, u32) + u32(1 << (GATE_SHIFT - 1))) >> GATE_SHIFT
    gate_mask = u32((1 << GATE_BITS) - 1)
    n_slots = P + n_blocks * CHUNK
    srt = lax.sort((eidx.astype(u32) << GATE_BITS) | gate_bits, dimension=1)
    srt = jnp.pad(srt, ((0, 0), (0, n_slots - P)))
    run_end = jnp.cumsum(pcnt, axis=1)
    run_start = run_end - pcnt
    shift = run_start - (jnp.cumsum(cnt, axis=1) - cnt)
    slot = jnp.arange(n_slots, dtype=jnp.int32)[None, :]
    seg = jnp.sum((slot[:, :, None] >= run_end[:, None, :]).astype(jnp.int32), axis=2)
    pick = lambda per_run: jnp.sum(jnp.where(seg[:, :, None] == blocks[None, None, :], per_run[:, None, :], 0), axis=2)
    is_pad = (seg == n_blocks) | (slot - pick(run_start) >= pick(cnt))
    slot_shift = pick(shift)
    placed = jnp.zeros_like(srt)
    for d in range((n_blocks - 1) * (CHUNK - 1) + 1):
        placed = jnp.where(slot_shift == d, jnp.roll(srt, d, axis=1), placed)
    loc = jnp.where(is_pad, u32(0), ((placed >> GATE_BITS) & u32(EXPERT_BLOCK - 1)) * SUBLANES)
    gate = lax.bitcast_convert_type(jnp.where(is_pad, u32(0), (placed & gate_mask) << GATE_SHIFT), F32)
    ends = run_end // CHUNK
    n_chunks = jnp.sum((pcnt // CHUNK).reshape(T // PEER_TM, PEER_TM, n_blocks), axis=1)

    per_tok = (P + n_blocks * CHUNK) // CHUNK
    slot_j = jnp.arange(per_tok, dtype=jnp.int32)
    slot_blk = jnp.sum((slot_j[None, :, None] >= ends[:, None, :]).astype(jnp.int32), axis=2)
    tok = (jnp.arange(T, dtype=jnp.int32) % PEER_TM)[:, None]
    entry = (tok << 16) | (tok * (per_tok * CHUNK) + slot_j[None, :] * CHUNK)
    assert PEER_TM <= 1 << 8
    order = lax.sort(((slot_blk << 24) | entry).reshape(T // PEER_TM, PEER_TM * per_tok), dimension=1)
    clist = order & ((1 << 24) - 1)
    clist = jnp.pad(clist, ((0, 0), (0, _chunk_list_stride(per_tok) - PEER_TM * per_tok)))

    n_tiles, per_tile = T // PEER_TM, PEER_TM * per_tok
    gate_tile = gate.reshape(n_tiles, PEER_TM, per_tok * CHUNK)
    q_pos = jnp.arange(per_tile, dtype=jnp.int32)
    gates_c = []
    for b in range(n_blocks):
        lst = lax.sort((((slot_blk != b).astype(jnp.int32) << 24) | entry).reshape(n_tiles, per_tile), dimension=1)
        tok_q = (lst >> 16) & 0xFF
        j_q = ((lst & 0xFFFF) - tok_q * (per_tok * CHUNK)) // CHUNK
        tok_hot = (tok_q[:, :, None] == jnp.arange(PEER_TM, dtype=jnp.int32)[None, None, :]).astype(F32)
        rows = jnp.einsum('xqt,xtm->xqm', tok_hot, gate_tile, precision=lax.Precision.HIGHEST)
        pick = (j_q[:, :, None] == slot_j[None, None, :]) & (q_pos[None, :, None] < n_chunks[:, b][:, None, None])
        rows = rows.reshape(n_tiles, per_tile, per_tok, CHUNK)
        gates_c.append(jnp.sum(jnp.where(pick[..., None], rows, 0.0), axis=2))
    gates_c = jnp.stack(gates_c).reshape(n_blocks, n_tiles, per_tile * CHUNK // LANES, LANES)
    return loc.astype(jnp.int32).reshape(-1), gates_c, n_chunks.T.reshape(-1), clist.reshape(-1)


def _pdot_kernel(nch_ref, cl_ref, idx_ref, x_ref, u_ref, d_ref, rbuf_ref, *, n_blocks):
    b = pl.program_id(0)
    i = pl.program_id(1)
    n_tiles = pl.num_programs(1)
    n_chunks = nch_ref[b * n_tiles + i]
    first = jnp.int32(0)
    for bb in range(n_blocks - 1):
        first = first + jnp.where(bb < b, nch_ref[bb * n_tiles + i], 0)
    zero_rows = jnp.zeros((SUBLANES, LANES), F32)
    per_group = LANES // CHUNK
    n_full = n_chunks // per_group
    n_groups = (n_chunks + per_group - 1) // per_group
    dummy_row = d_ref.shape[0] - SUBLANES

    rbuf_ref[0:LANES, :] = jnp.zeros((LANES, LANES), F32)
    d_ref[...] = jnp.zeros(d_ref.shape, F32)

    def chunk(q):
        entry = cl_ref[first + q]
        base = entry & 0xFFFF
        x = x_ref[entry >> 16]
        x_lo, x_hi = x[:SUBLANES], x[SUBLANES:]
        for h in range(CHUNK // SUBLANES):
            parts = []
            for k in range(SUBLANES):
                row = pl.multiple_of(idx_ref[base + h * SUBLANES + k], SUBLANES)
                lo, hi = _unpack_rows(u_ref[pl.ds(row, SUBLANES), :])
                parts.append(lo * x_lo + hi * x_hi)
            row0 = pl.multiple_of(LANES + q * CHUNK + h * SUBLANES, SUBLANES)
            rbuf_ref[pl.ds(row0, SUBLANES), :] = _sublane_fold(parts)

    def reduce_rows(first_row, out_row):
        r = rbuf_ref[pl.ds(pl.multiple_of(first_row, LANES), LANES), :]
        d_ref[pl.ds(out_row, 1), :] = jnp.sum(r.T, axis=0, keepdims=True)

    def group_body(g, carry):
        for c in range(per_group):
            chunk(g * per_group + c)
        reduce_rows(g * LANES, jnp.where(g == 0, dummy_row, g - 1))
        return carry

    lax.fori_loop(0, n_full, group_body, 0)

    def tail_body(q, carry):
        chunk(q)
        return carry

    lax.fori_loop(n_full * per_group, n_chunks, tail_body, 0)

    def zero_body(f, carry):
        rbuf_ref[pl.ds(pl.multiple_of(LANES + f * SUBLANES, SUBLANES), SUBLANES), :] = zero_rows
        return carry

    folds_per_chunk = CHUNK // SUBLANES
    lax.fori_loop(n_chunks * folds_per_chunk, n_groups * per_group * folds_per_chunk, zero_body, 0)

    def last_body(g, carry):
        reduce_rows((g + 1) * LANES, g)
        return carry

    lax.fori_loop(jnp.maximum(n_full - 1, 0), n_groups, last_body, 0)


def _expert_block_spec():
    return pl.BlockSpec((EXPERT_BLOCK * SUBLANES, LANES), lambda b, i: (b, 0), pipeline_mode=pl.Buffered(1))


def _peer_dots(n_chunks, clist, idx, x3, u_rows, slots):
    T = x3.shape[0]
    nb = u_rows.shape[0] // (EXPERT_BLOCK * SUBLANES)
    tm = PEER_TM
    n_tiles = T // tm
    max_pairs = tm * slots
    groups = max_pairs // LANES + SUBLANES
    smem_tile = lambda n: pl.BlockSpec((tm * n,), lambda b, i: (i,), memory_space=pltpu.SMEM)
    out = jax.ShapeDtypeStruct((nb, n_tiles, groups, LANES), F32)
    out_spec = pl.BlockSpec((None, None, groups, LANES), lambda b, i: (b, i, 0, 0))
    return pl.pallas_call(
        functools.partial(_pdot_kernel, n_blocks=nb),
        grid=(nb, n_tiles),
        in_specs=[
            pl.BlockSpec(memory_space=pltpu.SMEM),
            pl.BlockSpec((_chunk_list_stride(slots // CHUNK),), lambda b, i: (i,), memory_space=pltpu.SMEM),
            smem_tile(slots),
            pl.BlockSpec((tm, ROW_SUBLANES, LANES), lambda b, i: (i, 0, 0)),
            _expert_block_spec(),
        ],
        out_specs=out_spec,
        out_shape=out,
        scratch_shapes=[pltpu.VMEM((LANES + max_pairs, LANES), F32)],
        compiler_params=_cparams(("arbitrary", "arbitrary")),
        name="pdot",
    )(n_chunks, clist, idx, x3, u_rows)


def _pacc_kernel(nch_ref, cl_ref, idx_ref, d_ref, g_ref, v_ref, o_ref, wrep_ref, wrow_ref, *, n_blocks):
    b = pl.program_id(0)
    i = pl.program_id(1)
    n_tiles = pl.num_programs(1)
    n_chunks = nch_ref[b * n_tiles + i]
    first = jnp.int32(0)
    for bb in range(n_blocks - 1):
        first = first + jnp.where(bb < b, nch_ref[bb * n_tiles + i], 0)

    per_group = LANES // CHUNK
    n_full = n_chunks // per_group

    def expand(g):
        tile = (SUBLANES, LANES)
        wrow_ref[...] = (jnp.broadcast_to(g_ref[pl.ds(g, 1), :], tile)
                         * jax.nn.gelu(jnp.broadcast_to(d_ref[pl.ds(g, 1), :], tile)))
        rows = jnp.broadcast_to(wrow_ref[0:1, :], (LANES, LANES))
        wrep_ref[pl.ds(pl.multiple_of(g * LANES, LANES), LANES), :] = rows.T

    expand(0)
    o_ref[...] = jnp.zeros(o_ref.shape, F32)

    def chunk(q):
        entry = cl_ref[first + q]
        base = entry & 0xFFFF
        tok = entry >> 16
        zero = jnp.zeros((SUBLANES, LANES), F32)
        accs = [zero, zero, zero, zero]
        for k in range(CHUNK):
            w = jnp.broadcast_to(wrep_ref[pl.ds(q * CHUNK + k, 1), :], (SUBLANES, LANES))
            lo, hi = _unpack_rows(v_ref[pl.ds(pl.multiple_of(idx_ref[base + k], SUBLANES), SUBLANES), :])
            j = 2 * (k % 2)
            accs[j] = accs[j] + w * lo
            accs[j + 1] = accs[j + 1] + w * hi
        o_ref[tok] = o_ref[tok] + jnp.concatenate([accs[0] + accs[2], accs[1] + accs[3]], axis=0)

    def group_body(g, carry):
        expand(g + 1)
        for c in range(per_group):
            chunk(g * per_group + c)
        return carry

    lax.fori_loop(0, n_full, group_body, 0)

    def tail_body(q, carry):
        chunk(q)
        return carry

    lax.fori_loop(n_full * per_group, n_chunks, tail_body, 0)


def _peer_accumulate(n_chunks, clist, idx, dots, gates_c, v_rows, T, slots):
    nb = v_rows.shape[0] // (EXPERT_BLOCK * SUBLANES)
    tm = PEER_TM
    n_tiles = T // tm
    groups = dots.shape[2]
    assert groups == tm * slots // LANES + SUBLANES and gates_c.shape == dots.shape
    smem_tile = lambda n: pl.BlockSpec((tm * n,), lambda b, i: (i,), memory_space=pltpu.SMEM)
    per_step = lambda: pl.BlockSpec((None, None, groups, LANES), lambda b, i: (b, i, 0, 0))
    return pl.pallas_call(
        functools.partial(_pacc_kernel, n_blocks=nb),
        grid=(nb, n_tiles),
        in_specs=[
            pl.BlockSpec(memory_space=pltpu.SMEM),
            pl.BlockSpec((_chunk_list_stride(slots // CHUNK),), lambda b, i: (i,), memory_space=pltpu.SMEM),
            smem_tile(slots),
            per_step(), per_step(),
            _expert_block_spec(),
        ],
        out_specs=pl.BlockSpec((None, tm, ROW_SUBLANES, LANES), lambda b, i: (b, i, 0, 0)),
        out_shape=jax.ShapeDtypeStruct((nb, T, ROW_SUBLANES, LANES), F32),
        scratch_shapes=[pltpu.VMEM((tm * slots + LANES, LANES), F32), pltpu.VMEM((SUBLANES, LANES), F32)],
        compiler_params=_cparams(("arbitrary", "arbitrary")),
        name="pacc",
    )(n_chunks, clist, idx, dots, gates_c, v_rows)


def _final_kernel(x1_ref, p_ref, g_ref, o_ref):
    nb, _, n_s, _ = p_ref.shape
    cols = []
    for s in range(n_s):
        col = x1_ref[:, s * LANES:(s + 1) * LANES]
        for b in range(nb):
            col = col + p_ref[b, :, s, :]
        cols.append(col)
    o_ref[...] = _rms(jnp.concatenate(cols, axis=1), g_ref[...])


def _final(x1, peer_parts, g_final):
    T, D = x1.shape
    nb = peer_parts.shape[0]
    tm = 256
    return pl.pallas_call(
        _final_kernel,
        grid=(T // tm,),
        in_specs=[
            pl.BlockSpec((tm, D), lambda i: (i, 0)),
            pl.BlockSpec((nb, tm, D // LANES, LANES), lambda i: (0, i, 0, 0)),
            pl.BlockSpec((1, D), lambda i: (0, 0)),
        ],
        out_specs=pl.BlockSpec((tm, D), lambda i: (i, 0)),
        out_shape=jax.ShapeDtypeStruct((T, D), F32),
        compiler_params=_cparams(("parallel",)),
        name="final",
    )(x1, peer_parts, g_final)


def _rope_tables(seq):
    half = HEAD_DIM // 2
    inv = ROPE_THETA ** (-jnp.arange(half, dtype=F32) / half)
    ang = jnp.arange(seq).astype(F32)[:, None] * inv[None, :]
    cos, sin = jnp.cos(ang), jnp.sin(ang)
    reps = LANES // HEAD_DIM
    cos_t = jnp.concatenate([cos, cos] * reps, axis=1)
    sin_t = jnp.concatenate([-sin, sin] * reps, axis=1)
    return cos_t, sin_t


def _layer(xt, batch, seq, g_mix, w_in, sink, conv_w, conv_b, fwd_wa, fwd_ba, fwd_wx, fwd_bx, fwd_lam,
           bwd_wa, bwd_ba, bwd_wx, bwd_bx, bwd_lam, g_attn_out, g_lru_out, w_out, g_ffn,
           w_pq, sub_k1, sub_k2, u_emb, v_emb):
    T, D = xt.shape
    cos_t, sin_t = _rope_tables(seq)
    qkv, gr = _inproj(xt, g_mix.reshape(1, D), w_in.astype(BF16), cos_t, sin_t, seq)
    y_attn = _attention(qkv, sink, batch, seq)
    y_lru = _lru(gr, conv_w, conv_b, (fwd_wa, fwd_wx, bwd_wa, bwd_wx), (fwd_ba, fwd_bx, bwd_ba, bwd_bx),
                 (fwd_lam, bwd_lam), batch, seq)
    x1, xn, x3 = _outproj(y_attn, y_lru, xt, g_attn_out.reshape(1, -1), g_lru_out.reshape(1, -1),
                      w_out.astype(BF16), g_ffn.reshape(1, D))

    keys = jnp.concatenate([sub_k1, sub_k2], axis=0).astype(BF16)
    scores = _peer_scores(xn, w_pq.astype(BF16), keys)
    gates_t, eidx_t = _peer_topk(scores)
    n_pairs = PEER_HEADS * PEER_TOPK
    gates = gates_t.transpose(2, 0, 1).reshape(T, n_pairs)
    eidx = eidx_t.transpose(2, 0, 1).reshape(T, n_pairs)
    n_exp = u_emb.shape[0]
    nb = n_exp // EXPERT_BLOCK
    slots = n_pairs + nb * CHUNK
    idx, gates_c, n_chunks, clist = _route(eidx, gates, nb)

    u_rows = _pack_rows(u_emb)
    v_rows = _pack_rows(v_emb)
    dots = _peer_dots(n_chunks, clist, idx, x3, u_rows, slots)
    gates_c = jnp.pad(gates_c, ((0, 0), (0, 0), (0, dots.shape[2] - gates_c.shape[2]), (0, 0)))
    parts = _peer_accumulate(n_chunks, clist, idx, dots, gates_c, v_rows, T, slots)
    return x1, parts


def kernel(x, g_mix, w_in, sink, conv_w, conv_b, fwd_wa, fwd_ba, fwd_wx, fwd_bx, fwd_lam, bwd_wa, bwd_ba, bwd_wx, bwd_bx, bwd_lam, g_attn_out, g_lru_out, w_out, g_ffn, w_pq, sub_k1, sub_k2, u_emb, v_emb, g_final):
    B, S, D = x.shape
    assert g_mix.shape[0] == 1, "single-layer trunk"
    xt = x.reshape(B * S, D)
    x1, parts = _layer(xt, B, S, g_mix[0], w_in[0], sink[0], conv_w[0], conv_b[0],
                       fwd_wa[0], fwd_ba[0], fwd_wx[0], fwd_bx[0], fwd_lam[0],
                       bwd_wa[0], bwd_ba[0], bwd_wx[0], bwd_bx[0], bwd_lam[0],
                       g_attn_out[0], g_lru_out[0], w_out[0], g_ffn[0],
                       w_pq[0], sub_k1[0], sub_k2[0], u_emb[0], v_emb[0])
    return _final(x1, parts, g_final.reshape(1, D)).reshape(B, S, D)
```

```python
import functools

import jax
import jax.numpy as jnp
from jax import lax
from jax.experimental import pallas as pl
from jax.experimental.pallas import tpu as pltpu

F32 = jnp.float32
BF16 = jnp.bfloat16

HEAD_DIM = 64
N_KV_HEADS = 4
Q_PER_KV = 4
N_Q_HEADS = N_KV_HEADS * Q_PER_KV
WINDOW = 128
ATTN_BLOCK = 128
ROPE_THETA = 10000.0
LRU_BLOCK = 64
CONV_WIDTH = 4
CONV_PAD_LEFT = 2
LRU_C = 8.0
PEER_HEADS = 8
PEER_HALF = 128
N_KEYS = 128
PEER_TOPK = 16
EPS = 1e-6
NEG = -1e30

LANES = 128
SUBLANES = 8
VMEM_LIMIT = 56 * 1024 * 1024

INPROJ_TM = 512
INPROJ_TN = 512
LRU_CHANNELS = 256
LRU_CHUNK = 256
OUTPROJ_TM = 256
PQ_TM = 256
TOPK_TL = 512
EXPERT_BLOCK = 8192
PEER_TM = 128
CHUNK = 16
GATE_BITS = 18
GATE_SHIFT = 31 - GATE_BITS


def _cparams(sem):
    return pltpu.CompilerParams(dimension_semantics=sem, vmem_limit_bytes=VMEM_LIMIT)


def _rms(x, g):
    return x * lax.rsqrt(jnp.mean(x * x, axis=-1, keepdims=True) + EPS) * g


def _inproj_kernel(x_ref, g_ref, w_ref, cos_ref, sin_ref, qkv_ref, gr_ref, h_ref, *, n_rope_tiles, kv_cols):
    j = pl.program_id(1)

    @pl.when(j == 0)
    def _():
        h_ref[...] = _rms(x_ref[...], g_ref[...]).astype(BF16)

    acc = jnp.dot(h_ref[...], w_ref[...], preferred_element_type=F32)
    tm, tn = acc.shape

    @pl.when(j < n_rope_tiles)
    def _():
        reps = tn // LANES
        cos = jnp.concatenate([cos_ref[...]] * reps, axis=1)
        sin = jnp.concatenate([sin_ref[...]] * reps, axis=1)
        lane = lax.broadcasted_iota(jnp.int32, (tm, tn), 1)
        first = (lane % HEAD_DIM) < (HEAD_DIM // 2)
        partner = jnp.where(first, pltpu.roll(acc, tn - HEAD_DIM // 2, 1), pltpu.roll(acc, HEAD_DIM // 2, 1))
        roped = acc * cos + partner * sin
        is_rope = jnp.logical_or(j < n_rope_tiles - 1, lane < kv_cols)
        qkv_ref[...] = jnp.where(is_rope, roped, acc).astype(BF16)

    @pl.when(j >= n_rope_tiles)
    def _():
        gr_ref[...] = acc


def _inproj(xt, g_mix, w_in_bf, cos_t, sin_t, seq):
    T, D = xt.shape
    n_cols = w_in_bf.shape[1]
    q_cols = N_Q_HEADS * HEAD_DIM
    kv_cols = N_KV_HEADS * HEAD_DIM
    qkv_cols = q_cols + 2 * kv_cols
    tm, tn = INPROJ_TM, INPROJ_TN
    n_rope_tiles = qkv_cols // tn
    assert qkv_cols % tn == 0 and (q_cols % tn == 0) and seq % tm == 0 and T % tm == 0
    pos_blocks = seq // tm
    kern = functools.partial(_inproj_kernel, n_rope_tiles=n_rope_tiles, kv_cols=kv_cols)
    return pl.pallas_call(
        kern,
        grid=(T // tm, n_cols // tn),
        in_specs=[
            pl.BlockSpec((tm, D), lambda i, j: (i, 0)),
            pl.BlockSpec((1, D), lambda i, j: (0, 0)),
            pl.BlockSpec((D, tn), lambda i, j: (0, j)),
            pl.BlockSpec((tm, LANES), lambda i, j: (i % pos_blocks, 0)),
            pl.BlockSpec((tm, LANES), lambda i, j: (i % pos_blocks, 0)),
        ],
        out_specs=[
            pl.BlockSpec((tm, tn), lambda i, j: (i, jnp.minimum(j, n_rope_tiles - 1))),
            pl.BlockSpec((tm, tn), lambda i, j: (i, jnp.maximum(j - n_rope_tiles, 0))),
        ],
        out_shape=[
            jax.ShapeDtypeStruct((T, qkv_cols), BF16),
            jax.ShapeDtypeStruct((T, n_cols - qkv_cols), F32),
        ],
        scratch_shapes=[pltpu.VMEM((tm, D), BF16)],
        compiler_params=_cparams(("parallel", "arbitrary")),
        name="inproj",
    )(xt, g_mix, w_in_bf, cos_t, sin_t)


def _attn_kernel(sink_ref, q_ref, kp_ref, kc_ref, kn_ref, vp_ref, vc_ref, vn_ref, o_ref, *, seq):
    n = pl.program_id(1)
    q = q_ref[...] * jnp.asarray(HEAD_DIM ** -0.5, BF16)
    k = jnp.concatenate([kp_ref[...], kc_ref[...], kn_ref[...]], axis=0)
    v = jnp.concatenate([vp_ref[...], vc_ref[...], vn_ref[...]], axis=0)
    nq, nk = ATTN_BLOCK, 3 * ATTN_BLOCK
    qpos = n * ATTN_BLOCK + lax.broadcasted_iota(jnp.int32, (nq, nk), 0)
    kpos = (n - 1) * ATTN_BLOCK + lax.broadcasted_iota(jnp.int32, (nq, nk), 1)
    valid = (jnp.abs(kpos - qpos) <= WINDOW) & (kpos >= 0) & (kpos < seq)
    outs = []
    for g in range(N_KV_HEADS):
        kg = k[:, g * HEAD_DIM:(g + 1) * HEAD_DIM]
        vg = v[:, g * HEAD_DIM:(g + 1) * HEAD_DIM]
        for r in range(Q_PER_KV):
            h = g * Q_PER_KV + r
            qh = q[:, h * HEAD_DIM:(h + 1) * HEAD_DIM]
            s = lax.dot_general(qh, kg, (((1,), (1,)), ((), ())), preferred_element_type=F32)
            s = jnp.where(valid, s, NEG)
            sk = sink_ref[h]
            m = jnp.maximum(jnp.max(s, axis=-1, keepdims=True), sk)
            p = jnp.exp(s - m)
            den = jnp.sum(p, axis=-1, keepdims=True) + jnp.exp(sk - m)
            o = jnp.dot(p.astype(BF16), vg, preferred_element_type=F32) / den
            outs.append(o)
    o_ref[...] = jnp.concatenate(outs, axis=1)


def _attention(qkv, sink, batch, seq):
    T = qkv.shape[0]
    nb = seq // ATTN_BLOCK
    q_cols = N_Q_HEADS * HEAD_DIM
    kv_cols = N_KV_HEADS * HEAD_DIM
    k_blk = q_cols // kv_cols
    v_blk = k_blk + 1
    blk = ATTN_BLOCK

    def row(b, n):
        return b * nb + n

    def kv_spec(col_blk, shift):
        return pl.BlockSpec((blk, kv_cols), lambda b, n: (row(b, jnp.clip(n + shift, 0, nb - 1)), col_blk))

    return pl.pallas_call(
        functools.partial(_attn_kernel, seq=seq),
        grid=(batch, nb),
        in_specs=[
            pl.BlockSpec(memory_space=pltpu.SMEM),
            pl.BlockSpec((blk, q_cols), lambda b, n: (row(b, n), 0)),
            kv_spec(k_blk, -1), kv_spec(k_blk, 0), kv_spec(k_blk, 1),
            kv_spec(v_blk, -1), kv_spec(v_blk, 0), kv_spec(v_blk, 1),
        ],
        out_specs=pl.BlockSpec((blk, q_cols), lambda b, n: (row(b, n), 0)),
        out_shape=jax.ShapeDtypeStruct((T, q_cols), F32),
        compiler_params=_cparams(("parallel", "parallel")),
        name="attn",
    )(sink, qkv, qkv, qkv, qkv, qkv, qkv, qkv)


def _scan_chunk(a, u, reverse):
    L, C = a.shape
    row = lax.broadcasted_iota(jnp.int32, (L, C), 0)
    A, H = a, u
    d = 1
    while d < L:
        if d < SUBLANES:
            if reverse:
                keep = row < L - d
                As = jnp.where(keep, pltpu.roll(A, L - d, 0), 1.0)
                Hs = jnp.where(keep, pltpu.roll(H, L - d, 0), 0.0)
            else:
                keep = row >= d
                As = jnp.where(keep, pltpu.roll(A, d, 0), 1.0)
                Hs = jnp.where(keep, pltpu.roll(H, d, 0), 0.0)
        else:
            one = jnp.ones((d, C), F32)
            zero = jnp.zeros((d, C), F32)
            if reverse:
                As = jnp.concatenate([A[d:], one], axis=0)
                Hs = jnp.concatenate([H[d:], zero], axis=0)
            else:
                As = jnp.concatenate([one, A[:L - d]], axis=0)
                Hs = jnp.concatenate([zero, H[:L - d]], axis=0)
        H = A * Hs + H
        A = A * As
        d *= 2
    return A, H


def _lru_kernel(xg_ref, xr_ref, cw_ref, cb_ref, wg_ref, bg_ref, lam_ref, y_ref, xp_ref, hf_ref):
    S, C = xr_ref.shape
    L = LRU_CHUNK
    n_chunks = S // L
    halo = SUBLANES

    xp_ref[0:halo, :] = jnp.zeros((halo, C), F32)
    xp_ref[S + halo:S + 2 * halo, :] = jnp.zeros((halo, C), F32)

    def copy_body(ci, carry):
        t0 = pl.multiple_of(ci * L, L)
        xp_ref[pl.ds(t0 + halo, L), :] = xr_ref[pl.ds(t0, L), :]
        return carry

    lax.fori_loop(0, n_chunks, copy_body, 0)

    cw = cw_ref[...]
    cb = cb_ref[...]
    bias = bg_ref[0]
    neg_c_softplus = -LRU_C * jax.nn.softplus(-lam_ref[0])

    def conv_chunk(t0):
        win = xp_ref[pl.ds(t0, L + 2 * halo), :]
        acc = cb
        for j in range(CONV_WIDTH):
            off = halo - CONV_PAD_LEFT + j
            acc = acc + cw[j:j + 1, :] * win[off:off + L, :]
        return acc

    def gate_au(c, direction):
        w = wg_ref[0, :, direction * 2 * C:(direction + 1) * 2 * C]
        z = jnp.dot(c.astype(BF16), w, preferred_element_type=F32) + bias[:, direction * 2 * C:(direction + 1) * 2 * C]
        r = jax.nn.sigmoid(z[:, :C])
        i = jax.nn.sigmoid(z[:, C:])
        log_a = r * neg_c_softplus[direction:direction + 1, :]
        a = jnp.exp(log_a)
        u = jnp.sqrt(1.0 - a * a) * (i * c)
        return a, u

    def fwd_body(ci, carry):
        t0 = pl.multiple_of(ci * L, L)
        c = conv_chunk(t0)
        a, u = gate_au(c, 0)
        A, H = _scan_chunk(a, u, reverse=False)
        h = H + A * carry
        hf_ref[pl.ds(t0, L), :] = h
        return h[L - 1:L, :]

    lax.fori_loop(0, n_chunks, fwd_body, jnp.zeros((1, C), F32))

    def bwd_body(k, carry):
        t0 = pl.multiple_of((n_chunks - 1 - k) * L, L)
        c = conv_chunk(t0)
        a, u = gate_au(c, 1)
        A, H = _scan_chunk(a, u, reverse=True)
        h = H + A * carry
        y_ref[pl.ds(t0, L), :] = jax.nn.gelu(xg_ref[pl.ds(t0, L), :]) * (hf_ref[pl.ds(t0, L), :] + h)
        return h[0:1, :]

    lax.fori_loop(0, n_chunks, bwd_body, jnp.zeros((1, C), F32))


def _block_diag_chunks(w, per_chunk):
    nblk, b, _ = w.shape
    w4 = w.reshape(nblk // per_chunk, per_chunk, b, b)
    eye = jnp.eye(per_chunk, dtype=w.dtype)
    m = w4[:, :, :, None, :] * eye[None, :, None, :, None]
    return m.reshape(nblk // per_chunk, per_chunk * b, per_chunk * b)


def _lru(gr, conv_w, conv_b, gate_ws, gate_bs, lams, batch, seq):
    T, two_w = gr.shape
    W = two_w // 2
    C = LRU_CHANNELS
    n_ch = W // C
    per_chunk = C // LRU_BLOCK
    wg = jnp.concatenate([_block_diag_chunks(w, per_chunk) for w in gate_ws], axis=-1).astype(BF16)
    bg = jnp.concatenate([b.reshape(n_ch, 1, C) for b in gate_bs], axis=-1)
    lam = jnp.stack([l.reshape(n_ch, C) for l in lams], axis=1)
    return pl.pallas_call(
        _lru_kernel,
        grid=(batch, n_ch),
        in_specs=[
            pl.BlockSpec((seq, C), lambda b, c: (b, c)),
            pl.BlockSpec((seq, C), lambda b, c: (b, n_ch + c)),
            pl.BlockSpec((CONV_WIDTH, C), lambda b, c: (0, c)),
            pl.BlockSpec((1, C), lambda b, c: (0, c)),
            pl.BlockSpec((1, C, 4 * C), lambda b, c: (c, 0, 0)),
            pl.BlockSpec((1, 1, 4 * C), lambda b, c: (c, 0, 0)),
            pl.BlockSpec((1, 2, C), lambda b, c: (c, 0, 0)),
        ],
        out_specs=pl.BlockSpec((seq, C), lambda b, c: (b, c)),
        out_shape=jax.ShapeDtypeStruct((T, W), F32),
        scratch_shapes=[pltpu.VMEM((seq + 2 * SUBLANES, C), F32), pltpu.VMEM((seq, C), F32)],
        compiler_params=_cparams(("parallel", "parallel")),
        name="lru",
    )(gr, gr, conv_w, conv_b.reshape(1, W), wg, bg, lam)


def _outproj_kernel(ya_ref, yl_ref, x_ref, ga_ref, gl_ref, w_ref, gf_ref, x1_ref, xn_ref, xs_ref):
    y = jnp.concatenate([_rms(ya_ref[...], ga_ref[...]), _rms(yl_ref[...], gl_ref[...])], axis=1).astype(BF16)
    x1 = x_ref[...] + jnp.dot(y, w_ref[...], preferred_element_type=F32)
    x1_ref[...] = x1
    xn = _rms(x1, gf_ref[...])
    xn_ref[...] = xn.astype(BF16)
    for s in range(xs_ref.shape[1]):
        xs_ref[:, s, :] = xn[:, s * LANES:(s + 1) * LANES]


def _outproj(y_attn, y_lru, xt, g_attn, g_lru, w_out_bf, g_ffn):
    T, D = xt.shape
    wa, wl = y_attn.shape[1], y_lru.shape[1]
    tm = OUTPROJ_TM
    return pl.pallas_call(
        _outproj_kernel,
        grid=(T // tm,),
        in_specs=[
            pl.BlockSpec((tm, wa), lambda i: (i, 0)),
            pl.BlockSpec((tm, wl), lambda i: (i, 0)),
            pl.BlockSpec((tm, D), lambda i: (i, 0)),
            pl.BlockSpec((1, wa), lambda i: (0, 0)),
            pl.BlockSpec((1, wl), lambda i: (0, 0)),
            pl.BlockSpec((wa + wl, D), lambda i: (0, 0)),
            pl.BlockSpec((1, D), lambda i: (0, 0)),
        ],
        out_specs=[pl.BlockSpec((tm, D), lambda i: (i, 0)), pl.BlockSpec((tm, D), lambda i: (i, 0)),
                   pl.BlockSpec((tm, D // LANES, LANES), lambda i: (i, 0, 0))],
        out_shape=[jax.ShapeDtypeStruct((T, D), F32), jax.ShapeDtypeStruct((T, D), BF16),
                   jax.ShapeDtypeStruct((T, D // LANES, LANES), F32)],
        compiler_params=_cparams(("parallel",)),
        name="outproj",
    )(y_attn, y_lru, xt, g_attn, g_lru, w_out_bf, g_ffn)


def _pq_kernel(xn_ref, w_ref, k_ref, s_ref):
    q = jnp.dot(xn_ref[...], w_ref[...], preferred_element_type=F32).astype(BF16)
    for hh in range(2 * PEER_HEADS):
        half, head = divmod(hh, PEER_HEADS)
        col = (head * 2 + half) * PEER_HALF
        qh = q[:, col:col + PEER_HALF]
        s_ref[hh] = lax.dot_general(k_ref[hh], qh, (((1,), (1,)), ((), ())), preferred_element_type=F32)


def _peer_scores(xn, w_pq_bf, keys_bf):
    T, D = xn.shape
    tm = PQ_TM
    nh = keys_bf.shape[0]
    return pl.pallas_call(
        _pq_kernel,
        grid=(T // tm,),
        in_specs=[
            pl.BlockSpec((tm, D), lambda i: (i, 0)),
            pl.BlockSpec(w_pq_bf.shape, lambda i: (0, 0)),
            pl.BlockSpec(keys_bf.shape, lambda i: (0, 0, 0)),
        ],
        out_specs=pl.BlockSpec((nh, N_KEYS, tm), lambda i: (0, 0, i)),
        out_shape=jax.ShapeDtypeStruct((nh, N_KEYS, T), F32),
        compiler_params=_cparams(("parallel",)),
        name="pq",
    )(xn, w_pq_bf, keys_bf)


def _top16_rows(s, payload=None):
    n, tl = s.shape
    row = lax.broadcasted_iota(jnp.int32, (n, tl), 0)
    vals, idxs, pays = [], [], []
    for _ in range(PEER_TOPK):
        m = jnp.max(s, axis=0, keepdims=True)
        idx = jnp.min(jnp.where(s == m, row, n), axis=0, keepdims=True)
        hit = row == idx
        vals.append(m)
        idxs.append(idx)
        if payload is not None:
            pays.append(jnp.max(jnp.where(hit, payload, -1), axis=0, keepdims=True))
        s = jnp.where(hit, -jnp.inf, s)
    cat = lambda xs: jnp.concatenate(xs, axis=0)
    return cat(vals), cat(idxs), (cat(pays) if payload is not None else None)


def _topk_kernel(s1_ref, s2_ref, g_ref, e_ref):
    v1, i1, _ = _top16_rows(s1_ref[0])
    v2, i2, _ = _top16_rows(s2_ref[0])
    K = PEER_TOPK
    tl = v1.shape[1]
    sub = lax.broadcasted_iota(jnp.int32, (SUBLANES, tl), 0)
    cand, cidx = [], []
    for a in range(K // 2):
        n_b = K // (a + 1)
        for b0 in range(0, n_b, SUBLANES):
            keep = sub < (n_b - b0)
            cand.append(jnp.where(keep, v1[a:a + 1, :] + v2[b0:b0 + SUBLANES, :], -jnp.inf))
            cidx.append(i1[a:a + 1, :] * N_KEYS + i2[b0:b0 + SUBLANES, :])
    cand.append(v1[K // 2:, :] + v2[0:1, :])
    cidx.append(i1[K // 2:, :] * N_KEYS + i2[0:1, :])
    sc, _, eidx = _top16_rows(jnp.concatenate(cand, axis=0), jnp.concatenate(cidx, axis=0))
    ex = jnp.exp(sc - sc[0:1, :])
    gate_t = (ex / jnp.sum(ex, axis=0, keepdims=True)).T
    eidx_t = eidx.T
    head = pl.program_id(1)
    for k in range(PEER_HEADS):
        @pl.when(head == k)
        def _():
            g_ref[:, k * K:(k + 1) * K] = gate_t
            e_ref[:, k * K:(k + 1) * K] = eidx_t


def _peer_topk(scores):
    nh2, nk, T = scores.shape
    nh = nh2 // 2
    tl = TOPK_TL
    out_spec = pl.BlockSpec((tl, nh * PEER_TOPK), lambda i, h: (i, 0))
    return pl.pallas_call(
        _topk_kernel,
        grid=(T // tl, nh),
        in_specs=[
            pl.BlockSpec((1, nk, tl), lambda i, h: (h, 0, i)),
            pl.BlockSpec((1, nk, tl), lambda i, h: (nh + h, 0, i)),
        ],
        out_specs=[out_spec, out_spec],
        out_shape=[
            jax.ShapeDtypeStruct((T, nh * PEER_TOPK), F32),
            jax.ShapeDtypeStruct((T, nh * PEER_TOPK), jnp.int32),
        ],
        compiler_params=_cparams(("parallel", "arbitrary")),
        name="topk",
    )(scores, scores)


ROW_SUBLANES = 16


def _pack_rows(emb):
    n, d = emb.shape
    half = d // 2
    assert half == SUBLANES * LANES
    bits = lax.bitcast_convert_type(emb.astype(BF16), jnp.uint16).astype(jnp.uint32)
    return (bits[:, :half] | (bits[:, half:] << 16)).reshape(n * SUBLANES, LANES)


def _unpack_rows(word):
    return pltpu.bitcast(word << 16, F32), pltpu.bitcast(word & jnp.uint32(0xFFFF0000), F32)


def _sublane_fold(parts):
    sub = lax.broadcasted_iota(jnp.int32, (SUBLANES, LANES), 0)
    step = 1
    while len(parts) > 1:
        low = (sub & step) == 0
        parts = [jnp.where(low, a, b) + pltpu.roll(jnp.where(low, b, a), step, 0)
                 for a, b in zip(parts[0::2], parts[1::2])]
        step *= 2
    return parts[0]


def _chunk_list_stride(chunks_per_token):
    return -(-PEER_TM * chunks_per_token // 1024) * 1024


def _route(eidx, gates, n_blocks):
    T, P = eidx.shape
    u32 = jnp.uint32
    blocks = jnp.arange(n_blocks, dtype=jnp.int32)
    blk = eidx // EXPERT_BLOCK
    cnt = jnp.sum((blk[:, :, None] == blocks[None, None, :]).astype(jnp.int32), axis=1)
    pcnt = (cnt + CHUNK - 1) // CHUNK * CHUNK
    cand = jnp.arange(CHUNK, dtype=jnp.int32)
    active = cand[None, None, :] < (pcnt - cnt)[:, :, None]
    gate_bits = (lax.bitcast_convert_type(gates, u32) + u32(1 << (GATE_SHIFT - 1))) >> GATE_SHIFT
    pad_mark = u32((1 << GATE_BITS) - 1)
    word = (eidx.astype(u32) << GATE_BITS) | gate_bits
    last_row = ((blocks + 1) * EXPERT_BLOCK - 1).astype(u32)
    pad_word = jnp.where(active, ((last_row << GATE_BITS) | pad_mark)[None, :, None], u32(0xFFFFFFFF))
    srt = lax.sort(jnp.concatenate([word, pad_word.reshape(T, n_blocks * CHUNK)], axis=1), dimension=1)
    is_pad = (srt & pad_mark) == pad_mark
    loc = jnp.where(is_pad, u32(0), ((srt >> GATE_BITS) & u32(EXPERT_BLOCK - 1)) * SUBLANES)
    gate = lax.bitcast_convert_type(jnp.where(is_pad, u32(0), (srt & pad_mark) << GATE_SHIFT), F32)
    ends = jnp.cumsum(pcnt, axis=1) // CHUNK
    n_chunks = jnp.sum((pcnt // CHUNK).reshape(T // PEER_TM, PEER_TM, n_blocks), axis=1)

    per_tok = (P + n_blocks * CHUNK) // CHUNK
    slot_j = jnp.arange(per_tok, dtype=jnp.int32)
    slot_blk = jnp.sum((slot_j[None, :, None] >= ends[:, None, :]).astype(jnp.int32), axis=2)
    tok = (jnp.arange(T, dtype=jnp.int32) % PEER_TM)[:, None]
    entry = (tok << 16) | (tok * (per_tok * CHUNK) + slot_j[None, :] * CHUNK)
    assert PEER_TM <= 1 << 8
    order = lax.sort(((slot_blk << 24) | entry).reshape(T // PEER_TM, PEER_TM * per_tok), dimension=1)
    clist = order & ((1 << 24) - 1)
    clist = jnp.pad(clist, ((0, 0), (0, _chunk_list_stride(per_tok) - PEER_TM * per_tok)))

    n_tiles, per_tile = T // PEER_TM, PEER_TM * per_tok
    gate_rows = gate.reshape(n_tiles, per_tile, CHUNK)
    q_pos = jnp.arange(per_tile, dtype=jnp.int32)
    gates_c = []
    for b in range(n_blocks):
        lst = lax.sort((((slot_blk != b).astype(jnp.int32) << 24) | entry).reshape(n_tiles, per_tile), dimension=1)
        src = (lst & 0xFFFF) // CHUNK
        rows = jnp.take_along_axis(gate_rows, src[:, :, None], axis=1)
        gates_c.append(jnp.where((q_pos[None, :] < n_chunks[:, b][:, None])[:, :, None], rows, 0.0))
    gates_c = jnp.stack(gates_c).reshape(n_blocks, n_tiles, per_tile * CHUNK // LANES, LANES)
    return loc.astype(jnp.int32).reshape(-1), gates_c, n_chunks.T.reshape(-1), clist.reshape(-1)


def _pdot_kernel(nch_ref, cl_ref, idx_ref, x_ref, u_ref, d_ref, rbuf_ref, *, n_blocks):
    b = pl.program_id(0)
    i = pl.program_id(1)
    n_tiles = pl.num_programs(1)
    n_chunks = nch_ref[b * n_tiles + i]
    first = jnp.int32(0)
    for bb in range(n_blocks - 1):
        first = first + jnp.where(bb < b, nch_ref[bb * n_tiles + i], 0)
    zero_rows = jnp.zeros((SUBLANES, LANES), F32)
    per_group = LANES // CHUNK
    n_full = n_chunks // per_group
    n_groups = (n_chunks + per_group - 1) // per_group
    dummy_row = d_ref.shape[0] - SUBLANES

    rbuf_ref[0:LANES, :] = jnp.zeros((LANES, LANES), F32)
    d_ref[...] = jnp.zeros(d_ref.shape, F32)

    def chunk(q):
        entry = cl_ref[first + q]
        base = entry & 0xFFFF
        x = x_ref[entry >> 16]
        x_lo, x_hi = x[:SUBLANES], x[SUBLANES:]
        for h in range(CHUNK // SUBLANES):
            parts = []
            for k in range(SUBLANES):
                row = pl.multiple_of(idx_ref[base + h * SUBLANES + k], SUBLANES)
                lo, hi = _unpack_rows(u_ref[pl.ds(row, SUBLANES), :])
                parts.append(lo * x_lo + hi * x_hi)
            row0 = pl.multiple_of(LANES + q * CHUNK + h * SUBLANES, SUBLANES)
            rbuf_ref[pl.ds(row0, SUBLANES), :] = _sublane_fold(parts)

    def reduce_rows(first_row, out_row):
        r = rbuf_ref[pl.ds(pl.multiple_of(first_row, LANES), LANES), :]
        d_ref[pl.ds(out_row, 1), :] = jnp.sum(r.T, axis=0, keepdims=True)

    def group_body(g, carry):
        for c in range(per_group):
            chunk(g * per_group + c)
        reduce_rows(g * LANES, jnp.where(g == 0, dummy_row, g - 1))
        return carry

    lax.fori_loop(0, n_full, group_body, 0)

    def tail_body(q, carry):
        chunk(q)
        return carry

    lax.fori_loop(n_full * per_group, n_chunks, tail_body, 0)

    def zero_body(f, carry):
        rbuf_ref[pl.ds(pl.multiple_of(LANES + f * SUBLANES, SUBLANES), SUBLANES), :] = zero_rows
        return carry

    folds_per_chunk = CHUNK // SUBLANES
    lax.fori_loop(n_chunks * folds_per_chunk, n_groups * per_group * folds_per_chunk, zero_body, 0)

    def last_body(g, carry):
        reduce_rows((g + 1) * LANES, g)
        return carry

    lax.fori_loop(jnp.maximum(n_full - 1, 0), n_groups, last_body, 0)


def _expert_block_spec():
    return pl.BlockSpec((EXPERT_BLOCK * SUBLANES, LANES), lambda b, i: (b, 0), pipeline_mode=pl.Buffered(1))


def _peer_dots(n_chunks, clist, idx, x3, u_rows, slots):
    T = x3.shape[0]
    nb = u_rows.shape[0] // (EXPERT_BLOCK * SUBLANES)
    tm = PEER_TM
    n_tiles = T // tm
    max_pairs = tm * slots
    groups = max_pairs // LANES + SUBLANES
    smem_tile = lambda n: pl.BlockSpec((tm * n,), lambda b, i: (i,), memory_space=pltpu.SMEM)
    out = jax.ShapeDtypeStruct((nb, n_tiles, groups, LANES), F32)
    out_spec = pl.BlockSpec((None, None, groups, LANES), lambda b, i: (b, i, 0, 0))
    return pl.pallas_call(
        functools.partial(_pdot_kernel, n_blocks=nb),
        grid=(nb, n_tiles),
        in_specs=[
            pl.BlockSpec(memory_space=pltpu.SMEM),
            pl.BlockSpec((_chunk_list_stride(slots // CHUNK),), lambda b, i: (i,), memory_space=pltpu.SMEM),
            smem_tile(slots),
            pl.BlockSpec((tm, ROW_SUBLANES, LANES), lambda b, i: (i, 0, 0)),
            _expert_block_spec(),
        ],
        out_specs=out_spec,
        out_shape=out,
        scratch_shapes=[pltpu.VMEM((LANES + max_pairs, LANES), F32)],
        compiler_params=_cparams(("arbitrary", "arbitrary")),
        name="pdot",
    )(n_chunks, clist, idx, x3, u_rows)


def _pacc_kernel(nch_ref, cl_ref, idx_ref, d_ref, g_ref, v_ref, o_ref, wrep_ref, wrow_ref, *, n_blocks):
    b = pl.program_id(0)
    i = pl.program_id(1)
    n_tiles = pl.num_programs(1)
    n_chunks = nch_ref[b * n_tiles + i]
    first = jnp.int32(0)
    for bb in range(n_blocks - 1):
        first = first + jnp.where(bb < b, nch_ref[bb * n_tiles + i], 0)

    per_group = LANES // CHUNK
    n_full = n_chunks // per_group

    def expand(g):
        tile = (SUBLANES, LANES)
        wrow_ref[...] = (jnp.broadcast_to(g_ref[pl.ds(g, 1), :], tile)
                         * jax.nn.gelu(jnp.broadcast_to(d_ref[pl.ds(g, 1), :], tile)))
        rows = jnp.broadcast_to(wrow_ref[0:1, :], (LANES, LANES))
        wrep_ref[pl.ds(pl.multiple_of(g * LANES, LANES), LANES), :] = rows.T

    expand(0)
    o_ref[...] = jnp.zeros(o_ref.shape, F32)

    def chunk(q):
        entry = cl_ref[first + q]
        base = entry & 0xFFFF
        tok = entry >> 16
        zero = jnp.zeros((SUBLANES, LANES), F32)
        accs = [zero, zero, zero, zero]
        for k in range(CHUNK):
            w = jnp.broadcast_to(wrep_ref[pl.ds(q * CHUNK + k, 1), :], (SUBLANES, LANES))
            lo, hi = _unpack_rows(v_ref[pl.ds(pl.multiple_of(idx_ref[base + k], SUBLANES), SUBLANES), :])
            j = 2 * (k % 2)
            accs[j] = accs[j] + w * lo
            accs[j + 1] = accs[j + 1] + w * hi
        o_ref[tok] = o_ref[tok] + jnp.concatenate([accs[0] + accs[2], accs[1] + accs[3]], axis=0)

    def group_body(g, carry):
        expand(g + 1)
        for c in range(per_group):
            chunk(g * per_group + c)
        return carry

    lax.fori_loop(0, n_full, group_body, 0)

    def tail_body(q, carry):
        chunk(q)
        return carry

    lax.fori_loop(n_full * per_group, n_chunks, tail_body, 0)


def _peer_accumulate(n_chunks, clist, idx, dots, gates_c, v_rows, T, slots):
    nb = v_rows.shape[0] // (EXPERT_BLOCK * SUBLANES)
    tm = PEER_TM
    n_tiles = T // tm
    groups = dots.shape[2]
    assert groups == tm * slots // LANES + SUBLANES and gates_c.shape == dots.shape
    smem_tile = lambda n: pl.BlockSpec((tm * n,), lambda b, i: (i,), memory_space=pltpu.SMEM)
    per_step = lambda: pl.BlockSpec((None, None, groups, LANES), lambda b, i: (b, i, 0, 0))
    return pl.pallas_call(
        functools.partial(_pacc_kernel, n_blocks=nb),
        grid=(nb, n_tiles),
        in_specs=[
            pl.BlockSpec(memory_space=pltpu.SMEM),
            pl.BlockSpec((_chunk_list_stride(slots // CHUNK),), lambda b, i: (i,), memory_space=pltpu.SMEM),
            smem_tile(slots),
            per_step(), per_step(),
            _expert_block_spec(),
        ],
        out_specs=pl.BlockSpec((None, tm, ROW_SUBLANES, LANES), lambda b, i: (b, i, 0, 0)),
        out_shape=jax.ShapeDtypeStruct((nb, T, ROW_SUBLANES, LANES), F32),
        scratch_shapes=[pltpu.VMEM((tm * slots + LANES, LANES), F32), pltpu.VMEM((SUBLANES, LANES), F32)],
        compiler_params=_cparams(("arbitrary", "arbitrary")),
        name="pacc",
    )(n_chunks, clist, idx, dots, gates_c, v_rows)


def _final_kernel(x1_ref, p_ref, g_ref, o_ref):
    nb, _, n_s, _ = p_ref.shape
    cols = []
    for s in range(n_s):
        col = x1_ref[:, s * LANES:(s + 1) * LANES]
        for b in range(nb):
            col = col + p_ref[b, :, s, :]
        cols.append(col)
    o_ref[...] = _rms(jnp.concatenate(cols, axis=1), g_ref[...])


def _final(x1, peer_parts, g_final):
    T, D = x1.shape
    nb = peer_parts.shape[0]
    tm = 256
    return pl.pallas_call(
        _final_kernel,
        grid=(T // tm,),
        in_specs=[
            pl.BlockSpec((tm, D), lambda i: (i, 0)),
            pl.BlockSpec((nb, tm, D // LANES, LANES), lambda i: (0, i, 0, 0)),
            pl.BlockSpec((1, D), lambda i: (0, 0)),
        ],
        out_specs=pl.BlockSpec((tm, D), lambda i: (i, 0)),
        out_shape=jax.ShapeDtypeStruct((T, D), F32),
        compiler_params=_cparams(("parallel",)),
        name="final",
    )(x1, peer_parts, g_final)


def _rope_tables(seq):
    half = HEAD_DIM // 2
    inv = ROPE_THETA ** (-jnp.arange(half, dtype=F32) / half)
    ang = jnp.arange(seq).astype(F32)[:, None] * inv[None, :]
    cos, sin = jnp.cos(ang), jnp.sin(ang)
    reps = LANES // HEAD_DIM
    cos_t = jnp.concatenate([cos, cos] * reps, axis=1)
    sin_t = jnp.concatenate([-sin, sin] * reps, axis=1)
    return cos_t, sin_t


def _layer(xt, batch, seq, g_mix, w_in, sink, conv_w, conv_b, fwd_wa, fwd_ba, fwd_wx, fwd_bx, fwd_lam,
           bwd_wa, bwd_ba, bwd_wx, bwd_bx, bwd_lam, g_attn_out, g_lru_out, w_out, g_ffn,
           w_pq, sub_k1, sub_k2, u_emb, v_emb):
    T, D = xt.shape
    cos_t, sin_t = _rope_tables(seq)
    qkv, gr = _inproj(xt, g_mix.reshape(1, D), w_in.astype(BF16), cos_t, sin_t, seq)
    y_attn = _attention(qkv, sink, batch, seq)
    y_lru = _lru(gr, conv_w, conv_b, (fwd_wa, fwd_wx, bwd_wa, bwd_wx), (fwd_ba, fwd_bx, bwd_ba, bwd_bx),
                 (fwd_lam, bwd_lam), batch, seq)
    x1, xn, x3 = _outproj(y_attn, y_lru, xt, g_attn_out.reshape(1, -1), g_lru_out.reshape(1, -1),
                      w_out.astype(BF16), g_ffn.reshape(1, D))

    keys = jnp.concatenate([sub_k1, sub_k2], axis=0).astype(BF16)
    scores = _peer_scores(xn, w_pq.astype(BF16), keys)
    gates, eidx = _peer_topk(scores)
    n_pairs = PEER_HEADS * PEER_TOPK
    n_exp = u_emb.shape[0]
    nb = n_exp // EXPERT_BLOCK
    slots = n_pairs + nb * CHUNK
    idx, gates_c, n_chunks, clist = _route(eidx, gates, nb)

    u_rows = _pack_rows(u_emb)
    v_rows = _pack_rows(v_emb)
    dots = _peer_dots(n_chunks, clist, idx, x3, u_rows, slots)
    gates_c = jnp.pad(gates_c, ((0, 0), (0, 0), (0, dots.shape[2] - gates_c.shape[2]), (0, 0)))
    parts = _peer_accumulate(n_chunks, clist, idx, dots, gates_c, v_rows, T, slots)
    return x1, parts


def kernel(x, g_mix, w_in, sink, conv_w, conv_b, fwd_wa, fwd_ba, fwd_wx, fwd_bx, fwd_lam, bwd_wa, bwd_ba, bwd_wx, bwd_bx, bwd_lam, g_attn_out, g_lru_out, w_out, g_ffn, w_pq, sub_k1, sub_k2, u_emb, v_emb, g_final):
    B, S, D = x.shape
    assert g_mix.shape[0] == 1, "single-layer trunk"
    xt = x.reshape(B * S, D)
    x1, parts = _layer(xt, B, S, g_mix[0], w_in[0], sink[0], conv_w[0], conv_b[0],
                       fwd_wa[0], fwd_ba[0], fwd_wx[0], fwd_bx[0], fwd_lam[0],
                       bwd_wa[0], bwd_ba[0], bwd_wx[0], bwd_bx[0], bwd_lam[0],
                       g_attn_out[0], g_lru_out[0], w_out[0], g_ffn[0],
                       w_pq[0], sub_k1[0], sub_k2[0], u_emb[0], v_emb[0])
    return _final(x1, parts, g_final.reshape(1, D)).reshape(B, S, D)
```

```python
import functools

import jax
import jax.numpy as jnp
from jax import lax
from jax.experimental import pallas as pl
from jax.experimental.pallas import tpu as pltpu

F32 = jnp.float32
BF16 = jnp.bfloat16

HEAD_DIM = 64
N_KV_HEADS = 4
Q_PER_KV = 4
N_Q_HEADS = N_KV_HEADS * Q_PER_KV
WINDOW = 128
ATTN_BLOCK = 128
ROPE_THETA = 10000.0
LRU_BLOCK = 64
CONV_WIDTH = 4
CONV_PAD_LEFT = 2
LRU_C = 8.0
PEER_HEADS = 8
PEER_HALF = 128
N_KEYS = 128
PEER_TOPK = 16
EPS = 1e-6
NEG = -1e30

LANES = 128
SUBLANES = 8
VMEM_LIMIT = 56 * 1024 * 1024

INPROJ_TM = 512
INPROJ_TN = 512
LRU_CHANNELS = 256
LRU_CHUNK = 256
OUTPROJ_TM = 256
PQ_TM = 256
TOPK_TL = 512
EXPERT_BLOCK = 8192
PEER_TM = 128
CHUNK = 16
GATE_BITS = 18
GATE_SHIFT = 31 - GATE_BITS


def _cparams(sem):
    return pltpu.CompilerParams(dimension_semantics=sem, vmem_limit_bytes=VMEM_LIMIT)


def _rms(x, g):
    return x * lax.rsqrt(jnp.mean(x * x, axis=-1, keepdims=True) + EPS) * g


def _inproj_kernel(x_ref, g_ref, w_ref, cos_ref, sin_ref, qkv_ref, gr_ref, h_ref, *, n_rope_tiles, kv_cols):
    j = pl.program_id(1)

    @pl.when(j == 0)
    def _():
        h_ref[...] = _rms(x_ref[...], g_ref[...]).astype(BF16)

    acc = jnp.dot(h_ref[...], w_ref[...], preferred_element_type=F32)
    tm, tn = acc.shape

    @pl.when(j < n_rope_tiles)
    def _():
        reps = tn // LANES
        cos = jnp.concatenate([cos_ref[...]] * reps, axis=1)
        sin = jnp.concatenate([sin_ref[...]] * reps, axis=1)
        lane = lax.broadcasted_iota(jnp.int32, (tm, tn), 1)
        first = (lane % HEAD_DIM) < (HEAD_DIM // 2)
        partner = jnp.where(first, pltpu.roll(acc, tn - HEAD_DIM // 2, 1), pltpu.roll(acc, HEAD_DIM // 2, 1))
        roped = acc * cos + partner * sin
        is_rope = jnp.logical_or(j < n_rope_tiles - 1, lane < kv_cols)
        qkv_ref[...] = jnp.where(is_rope, roped, acc).astype(BF16)

    @pl.when(j >= n_rope_tiles)
    def _():
        gr_ref[...] = acc


def _inproj(xt, g_mix, w_in_bf, cos_t, sin_t, seq):
    T, D = xt.shape
    n_cols = w_in_bf.shape[1]
    q_cols = N_Q_HEADS * HEAD_DIM
    kv_cols = N_KV_HEADS * HEAD_DIM
    qkv_cols = q_cols + 2 * kv_cols
    tm, tn = INPROJ_TM, INPROJ_TN
    n_rope_tiles = qkv_cols // tn
    assert qkv_cols % tn == 0 and (q_cols % tn == 0) and seq % tm == 0 and T % tm == 0
    pos_blocks = seq // tm
    kern = functools.partial(_inproj_kernel, n_rope_tiles=n_rope_tiles, kv_cols=kv_cols)
    return pl.pallas_call(
        kern,
        grid=(T // tm, n_cols // tn),
        in_specs=[
            pl.BlockSpec((tm, D), lambda i, j: (i, 0)),
            pl.BlockSpec((1, D), lambda i, j: (0, 0)),
            pl.BlockSpec((D, tn), lambda i, j: (0, j)),
            pl.BlockSpec((tm, LANES), lambda i, j: (i % pos_blocks, 0)),
            pl.BlockSpec((tm, LANES), lambda i, j: (i % pos_blocks, 0)),
        ],
        out_specs=[
            pl.BlockSpec((tm, tn), lambda i, j: (i, jnp.minimum(j, n_rope_tiles - 1))),
            pl.BlockSpec((tm, tn), lambda i, j: (i, jnp.maximum(j - n_rope_tiles, 0))),
        ],
        out_shape=[
            jax.ShapeDtypeStruct((T, qkv_cols), BF16),
            jax.ShapeDtypeStruct((T, n_cols - qkv_cols), F32),
        ],
        scratch_shapes=[pltpu.VMEM((tm, D), BF16)],
        compiler_params=_cparams(("parallel", "arbitrary")),
        name="inproj",
    )(xt, g_mix, w_in_bf, cos_t, sin_t)


def _attn_kernel(sink_ref, q_ref, kp_ref, kc_ref, kn_ref, vp_ref, vc_ref, vn_ref, o_ref, *, seq):
    n = pl.program_id(1)
    q = q_ref[...] * jnp.asarray(HEAD_DIM ** -0.5, BF16)
    k = jnp.concatenate([kp_ref[...], kc_ref[...], kn_ref[...]], axis=0)
    v = jnp.concatenate([vp_ref[...], vc_ref[...], vn_ref[...]], axis=0)
    nq, nk = ATTN_BLOCK, 3 * ATTN_BLOCK
    qpos = n * ATTN_BLOCK + lax.broadcasted_iota(jnp.int32, (nq, nk), 0)
    kpos = (n - 1) * ATTN_BLOCK + lax.broadcasted_iota(jnp.int32, (nq, nk), 1)
    valid = (jnp.abs(kpos - qpos) <= WINDOW) & (kpos >= 0) & (kpos < seq)
    outs = []
    for g in range(N_KV_HEADS):
        kg = k[:, g * HEAD_DIM:(g + 1) * HEAD_DIM]
        vg = jnp.concatenate([v[:, g * HEAD_DIM:(g + 1) * HEAD_DIM], jnp.ones((nk, HEAD_DIM), BF16)], axis=1)
        for r in range(Q_PER_KV):
            h = g * Q_PER_KV + r
            qh = q[:, h * HEAD_DIM:(h + 1) * HEAD_DIM]
            s = lax.dot_general(qh, kg, (((1,), (1,)), ((), ())), preferred_element_type=F32)
            s = jnp.where(valid, s, NEG)
            sk = sink_ref[h]
            m = jnp.maximum(jnp.max(s, axis=-1, keepdims=True), sk)
            p = jnp.exp(s - m)
            ov = jnp.dot(p.astype(BF16), vg, preferred_element_type=F32)
            den = ov[:, HEAD_DIM:HEAD_DIM + 1] + jnp.exp(sk - m)
            outs.append(ov[:, :HEAD_DIM] / den)
    o_ref[...] = jnp.concatenate(outs, axis=1)


def _attention(qkv, sink, batch, seq):
    T = qkv.shape[0]
    nb = seq // ATTN_BLOCK
    q_cols = N_Q_HEADS * HEAD_DIM
    kv_cols = N_KV_HEADS * HEAD_DIM
    k_blk = q_cols // kv_cols
    v_blk = k_blk + 1
    blk = ATTN_BLOCK

    def row(b, n):
        return b * nb + n

    def kv_spec(col_blk, shift):
        return pl.BlockSpec((blk, kv_cols), lambda b, n: (row(b, jnp.clip(n + shift, 0, nb - 1)), col_blk))

    return pl.pallas_call(
        functools.partial(_attn_kernel, seq=seq),
        grid=(batch, nb),
        in_specs=[
            pl.BlockSpec(memory_space=pltpu.SMEM),
            pl.BlockSpec((blk, q_cols), lambda b, n: (row(b, n), 0)),
            kv_spec(k_blk, -1), kv_spec(k_blk, 0), kv_spec(k_blk, 1),
            kv_spec(v_blk, -1), kv_spec(v_blk, 0), kv_spec(v_blk, 1),
        ],
        out_specs=pl.BlockSpec((blk, q_cols), lambda b, n: (row(b, n), 0)),
        out_shape=jax.ShapeDtypeStruct((T, q_cols), F32),
        compiler_params=_cparams(("parallel", "parallel")),
        name="attn",
    )(sink, qkv, qkv, qkv, qkv, qkv, qkv, qkv)


def _scan_chunk(a, u, reverse):
    L, C = a.shape
    row = lax.broadcasted_iota(jnp.int32, (L, C), 0)
    A, H = a, u
    d = 1
    while d < L:
        if d < SUBLANES:
            if reverse:
                keep = row < L - d
                As = jnp.where(keep, pltpu.roll(A, L - d, 0), 1.0)
                Hs = jnp.where(keep, pltpu.roll(H, L - d, 0), 0.0)
            else:
                keep = row >= d
                As = jnp.where(keep, pltpu.roll(A, d, 0), 1.0)
                Hs = jnp.where(keep, pltpu.roll(H, d, 0), 0.0)
        else:
            one = jnp.ones((d, C), F32)
            zero = jnp.zeros((d, C), F32)
            if reverse:
                As = jnp.concatenate([A[d:], one], axis=0)
                Hs = jnp.concatenate([H[d:], zero], axis=0)
            else:
                As = jnp.concatenate([one, A[:L - d]], axis=0)
                Hs = jnp.concatenate([zero, H[:L - d]], axis=0)
        H = A * Hs + H
        A = A * As
        d *= 2
    return A, H


def _lru_kernel(xg_ref, xr_ref, cw_ref, cb_ref, wg_ref, bg_ref, lam_ref, y_ref, xp_ref, hf_ref):
    S, C = xr_ref.shape
    L = LRU_CHUNK
    n_chunks = S // L
    halo = SUBLANES

    xp_ref[0:halo, :] = jnp.zeros((halo, C), F32)
    xp_ref[S + halo:S + 2 * halo, :] = jnp.zeros((halo, C), F32)

    def copy_body(ci, carry):
        t0 = pl.multiple_of(ci * L, L)
        xp_ref[pl.ds(t0 + halo, L), :] = xr_ref[pl.ds(t0, L), :]
        return carry

    lax.fori_loop(0, n_chunks, copy_body, 0)

    cw = cw_ref[...]
    cb = cb_ref[...]
    bias = bg_ref[0]
    neg_c_softplus = -LRU_C * jax.nn.softplus(-lam_ref[0])

    def conv_chunk(t0):
        win = xp_ref[pl.ds(t0, L + 2 * halo), :]
        acc = cb
        for j in range(CONV_WIDTH):
            off = halo - CONV_PAD_LEFT + j
            acc = acc + cw[j:j + 1, :] * win[off:off + L, :]
        return acc

    def gate_au(c, direction):
        w = wg_ref[0, :, direction * 2 * C:(direction + 1) * 2 * C]
        z = jnp.dot(c.astype(BF16), w, preferred_element_type=F32) + bias[:, direction * 2 * C:(direction + 1) * 2 * C]
        r = jax.nn.sigmoid(z[:, :C])
        i = jax.nn.sigmoid(z[:, C:])
        log_a = r * neg_c_softplus[direction:direction + 1, :]
        a = jnp.exp(log_a)
        u = jnp.sqrt(1.0 - a * a) * (i * c)
        return a, u

    def fwd_body(ci, carry):
        t0 = pl.multiple_of(ci * L, L)
        c = conv_chunk(t0)
        a, u = gate_au(c, 0)
        A, H = _scan_chunk(a, u, reverse=False)
        h = H + A * carry
        hf_ref[pl.ds(t0, L), :] = h
        return h[L - 1:L, :]

    lax.fori_loop(0, n_chunks, fwd_body, jnp.zeros((1, C), F32))

    def bwd_body(k, carry):
        t0 = pl.multiple_of((n_chunks - 1 - k) * L, L)
        c = conv_chunk(t0)
        a, u = gate_au(c, 1)
        A, H = _scan_chunk(a, u, reverse=True)
        h = H + A * carry
        y_ref[pl.ds(t0, L), :] = jax.nn.gelu(xg_ref[pl.ds(t0, L), :]) * (hf_ref[pl.ds(t0, L), :] + h)
        return h[0:1, :]

    lax.fori_loop(0, n_chunks, bwd_body, jnp.zeros((1, C), F32))


def _block_diag_chunks(w, per_chunk):
    nblk, b, _ = w.shape
    w4 = w.reshape(nblk // per_chunk, per_chunk, b, b)
    eye = jnp.eye(per_chunk, dtype=w.dtype)
    m = w4[:, :, :, None, :] * eye[None, :, None, :, None]
    return m.reshape(nblk // per_chunk, per_chunk * b, per_chunk * b)


def _lru(gr, conv_w, conv_b, gate_ws, gate_bs, lams, batch, seq):
    T, two_w = gr.shape
    W = two_w // 2
    C = LRU_CHANNELS
    n_ch = W // C
    per_chunk = C // LRU_BLOCK
    wg = jnp.concatenate([_block_diag_chunks(w, per_chunk) for w in gate_ws], axis=-1).astype(BF16)
    bg = jnp.concatenate([b.reshape(n_ch, 1, C) for b in gate_bs], axis=-1)
    lam = jnp.stack([l.reshape(n_ch, C) for l in lams], axis=1)
    return pl.pallas_call(
        _lru_kernel,
        grid=(batch, n_ch),
        in_specs=[
            pl.BlockSpec((seq, C), lambda b, c: (b, c)),
            pl.BlockSpec((seq, C), lambda b, c: (b, n_ch + c)),
            pl.BlockSpec((CONV_WIDTH, C), lambda b, c: (0, c)),
            pl.BlockSpec((1, C), lambda b, c: (0, c)),
            pl.BlockSpec((1, C, 4 * C), lambda b, c: (c, 0, 0)),
            pl.BlockSpec((1, 1, 4 * C), lambda b, c: (c, 0, 0)),
            pl.BlockSpec((1, 2, C), lambda b, c: (c, 0, 0)),
        ],
        out_specs=pl.BlockSpec((seq, C), lambda b, c: (b, c)),
        out_shape=jax.ShapeDtypeStruct((T, W), F32),
        scratch_shapes=[pltpu.VMEM((seq + 2 * SUBLANES, C), F32), pltpu.VMEM((seq, C), F32)],
        compiler_params=_cparams(("parallel", "parallel")),
        name="lru",
    )(gr, gr, conv_w, conv_b.reshape(1, W), wg, bg, lam)


def _outproj_kernel(ya_ref, yl_ref, x_ref, ga_ref, gl_ref, w_ref, gf_ref, x1_ref, xn_ref, xs_ref):
    y = jnp.concatenate([_rms(ya_ref[...], ga_ref[...]), _rms(yl_ref[...], gl_ref[...])], axis=1).astype(BF16)
    x1 = x_ref[...] + jnp.dot(y, w_ref[...], preferred_element_type=F32)
    x1_ref[...] = x1
    xn = _rms(x1, gf_ref[...])
    xn_ref[...] = xn.astype(BF16)
    for s in range(xs_ref.shape[1]):
        xs_ref[:, s, :] = xn[:, s * LANES:(s + 1) * LANES]


def _outproj(y_attn, y_lru, xt, g_attn, g_lru, w_out_bf, g_ffn):
    T, D = xt.shape
    wa, wl = y_attn.shape[1], y_lru.shape[1]
    tm = OUTPROJ_TM
    return pl.pallas_call(
        _outproj_kernel,
        grid=(T // tm,),
        in_specs=[
            pl.BlockSpec((tm, wa), lambda i: (i, 0)),
            pl.BlockSpec((tm, wl), lambda i: (i, 0)),
            pl.BlockSpec((tm, D), lambda i: (i, 0)),
            pl.BlockSpec((1, wa), lambda i: (0, 0)),
            pl.BlockSpec((1, wl), lambda i: (0, 0)),
            pl.BlockSpec((wa + wl, D), lambda i: (0, 0)),
            pl.BlockSpec((1, D), lambda i: (0, 0)),
        ],
        out_specs=[pl.BlockSpec((tm, D), lambda i: (i, 0)), pl.BlockSpec((tm, D), lambda i: (i, 0)),
                   pl.BlockSpec((tm, D // LANES, LANES), lambda i: (i, 0, 0))],
        out_shape=[jax.ShapeDtypeStruct((T, D), F32), jax.ShapeDtypeStruct((T, D), BF16),
                   jax.ShapeDtypeStruct((T, D // LANES, LANES), F32)],
        compiler_params=_cparams(("parallel",)),
        name="outproj",
    )(y_attn, y_lru, xt, g_attn, g_lru, w_out_bf, g_ffn)


def _pq_kernel(xn_ref, w_ref, k_ref, s_ref):
    q = jnp.dot(xn_ref[...], w_ref[...], preferred_element_type=F32).astype(BF16)
    for hh in range(2 * PEER_HEADS):
        half, head = divmod(hh, PEER_HEADS)
        col = (head * 2 + half) * PEER_HALF
        qh = q[:, col:col + PEER_HALF]
        s_ref[hh] = lax.dot_general(k_ref[hh], qh, (((1,), (1,)), ((), ())), preferred_element_type=F32)


def _peer_scores(xn, w_pq_bf, keys_bf):
    T, D = xn.shape
    tm = PQ_TM
    nh = keys_bf.shape[0]
    return pl.pallas_call(
        _pq_kernel,
        grid=(T // tm,),
        in_specs=[
            pl.BlockSpec((tm, D), lambda i: (i, 0)),
            pl.BlockSpec(w_pq_bf.shape, lambda i: (0, 0)),
            pl.BlockSpec(keys_bf.shape, lambda i: (0, 0, 0)),
        ],
        out_specs=pl.BlockSpec((nh, N_KEYS, tm), lambda i: (0, 0, i)),
        out_shape=jax.ShapeDtypeStruct((nh, N_KEYS, T), F32),
        compiler_params=_cparams(("parallel",)),
        name="pq",
    )(xn, w_pq_bf, keys_bf)


def _top16_rows(s, payload=None):
    n, tl = s.shape
    row = lax.broadcasted_iota(jnp.int32, (n, tl), 0)
    vals, idxs, pays = [], [], []
    for _ in range(PEER_TOPK):
        m = jnp.max(s, axis=0, keepdims=True)
        idx = jnp.min(jnp.where(s == m, row, n), axis=0, keepdims=True)
        hit = row == idx
        vals.append(m)
        idxs.append(idx)
        if payload is not None:
            pays.append(jnp.max(jnp.where(hit, payload, -1), axis=0, keepdims=True))
        s = jnp.where(hit, -jnp.inf, s)
    cat = lambda xs: jnp.concatenate(xs, axis=0)
    return cat(vals), cat(idxs), (cat(pays) if payload is not None else None)


def _topk_kernel(s1_ref, s2_ref, g_ref, e_ref):
    v1, i1, _ = _top16_rows(s1_ref[0])
    v2, i2, _ = _top16_rows(s2_ref[0])
    K = PEER_TOPK
    tl = v1.shape[1]
    sub = lax.broadcasted_iota(jnp.int32, (SUBLANES, tl), 0)
    cand, cidx = [], []
    for a in range(K // 2):
        n_b = K // (a + 1)
        for b0 in range(0, n_b, SUBLANES):
            keep = sub < (n_b - b0)
            cand.append(jnp.where(keep, v1[a:a + 1, :] + v2[b0:b0 + SUBLANES, :], -jnp.inf))
            cidx.append(i1[a:a + 1, :] * N_KEYS + i2[b0:b0 + SUBLANES, :])
    cand.append(v1[K // 2:, :] + v2[0:1, :])
    cidx.append(i1[K // 2:, :] * N_KEYS + i2[0:1, :])
    sc, _, eidx = _top16_rows(jnp.concatenate(cand, axis=0), jnp.concatenate(cidx, axis=0))
    ex = jnp.exp(sc - sc[0:1, :])
    g_ref[0] = ex / jnp.sum(ex, axis=0, keepdims=True)
    e_ref[0] = eidx


def _peer_topk(scores):
    nh2, nk, T = scores.shape
    nh = nh2 // 2
    tl = TOPK_TL
    return pl.pallas_call(
        _topk_kernel,
        grid=(nh, T // tl),
        in_specs=[
            pl.BlockSpec((1, nk, tl), lambda h, i: (h, 0, i)),
            pl.BlockSpec((1, nk, tl), lambda h, i: (nh + h, 0, i)),
        ],
        out_specs=[
            pl.BlockSpec((1, PEER_TOPK, tl), lambda h, i: (h, 0, i)),
            pl.BlockSpec((1, PEER_TOPK, tl), lambda h, i: (h, 0, i)),
        ],
        out_shape=[
            jax.ShapeDtypeStruct((nh, PEER_TOPK, T), F32),
            jax.ShapeDtypeStruct((nh, PEER_TOPK, T), jnp.int32),
        ],
        compiler_params=_cparams(("parallel", "parallel")),
        name="topk",
    )(scores, scores)


ROW_SUBLANES = 16


def _pack_rows(emb):
    n, d = emb.shape
    half = d // 2
    assert half == SUBLANES * LANES
    bits = lax.bitcast_convert_type(emb.astype(BF16), jnp.uint16).astype(jnp.uint32)
    return (bits[:, :half] | (bits[:, half:] << 16)).reshape(n * SUBLANES, LANES)


def _unpack_rows(word):
    return pltpu.bitcast(word << 16, F32), pltpu.bitcast(word & jnp.uint32(0xFFFF0000), F32)


def _sublane_fold(parts):
    sub = lax.broadcasted_iota(jnp.int32, (SUBLANES, LANES), 0)
    step = 1
    while len(parts) > 1:
        low = (sub & step) == 0
        parts = [jnp.where(low, a, b) + pltpu.roll(jnp.where(low, b, a), step, 0)
                 for a, b in zip(parts[0::2], parts[1::2])]
        step *= 2
    return parts[0]


def _chunk_list_stride(chunks_per_token):
    return -(-PEER_TM * chunks_per_token // 1024) * 1024


def _route(eidx, gates, n_blocks):
    T, P = eidx.shape
    u32 = jnp.uint32
    blocks = jnp.arange(n_blocks, dtype=jnp.int32)
    blk = eidx // EXPERT_BLOCK
    cnt = jnp.sum((blk[:, :, None] == blocks[None, None, :]).astype(jnp.int32), axis=1)
    pcnt = (cnt + CHUNK - 1) // CHUNK * CHUNK
    cand = jnp.arange(CHUNK, dtype=jnp.int32)
    active = cand[None, None, :] < (pcnt - cnt)[:, :, None]
    gate_bits = (lax.bitcast_convert_type(gates, u32) + u32(1 << (GATE_SHIFT - 1))) >> GATE_SHIFT
    pad_mark = u32((1 << GATE_BITS) - 1)
    word = (eidx.astype(u32) << GATE_BITS) | gate_bits
    last_row = ((blocks + 1) * EXPERT_BLOCK - 1).astype(u32)
    pad_word = jnp.where(active, ((last_row << GATE_BITS) | pad_mark)[None, :, None], u32(0xFFFFFFFF))
    srt = lax.sort(jnp.concatenate([word, pad_word.reshape(T, n_blocks * CHUNK)], axis=1), dimension=1)
    is_pad = (srt & pad_mark) == pad_mark
    loc = jnp.where(is_pad, u32(0), ((srt >> GATE_BITS) & u32(EXPERT_BLOCK - 1)) * SUBLANES)
    gate = lax.bitcast_convert_type(jnp.where(is_pad, u32(0), (srt & pad_mark) << GATE_SHIFT), F32)
    ends = jnp.cumsum(pcnt, axis=1) // CHUNK
    n_chunks = jnp.sum((pcnt // CHUNK).reshape(T // PEER_TM, PEER_TM, n_blocks), axis=1)

    per_tok = (P + n_blocks * CHUNK) // CHUNK
    slot_j = jnp.arange(per_tok, dtype=jnp.int32)
    slot_blk = jnp.sum((slot_j[None, :, None] >= ends[:, None, :]).astype(jnp.int32), axis=2)
    tok = (jnp.arange(T, dtype=jnp.int32) % PEER_TM)[:, None]
    entry = (tok << 16) | (tok * (per_tok * CHUNK) + slot_j[None, :] * CHUNK)
    assert PEER_TM <= 1 << 8
    order = lax.sort(((slot_blk << 24) | entry).reshape(T // PEER_TM, PEER_TM * per_tok), dimension=1)
    clist = order & ((1 << 24) - 1)
    clist = jnp.pad(clist, ((0, 0), (0, _chunk_list_stride(per_tok) - PEER_TM * per_tok)))

    n_tiles, per_tile = T // PEER_TM, PEER_TM * per_tok
    gate_rows = gate.reshape(n_tiles, per_tile, CHUNK)
    q_pos = jnp.arange(per_tile, dtype=jnp.int32)
    gates_c = []
    for b in range(n_blocks):
        lst = lax.sort((((slot_blk != b).astype(jnp.int32) << 24) | entry).reshape(n_tiles, per_tile), dimension=1)
        src = (lst & 0xFFFF) // CHUNK
        rows = jnp.take_along_axis(gate_rows, src[:, :, None], axis=1)
        gates_c.append(jnp.where((q_pos[None, :] < n_chunks[:, b][:, None])[:, :, None], rows, 0.0))
    gates_c = jnp.stack(gates_c).reshape(n_blocks, n_tiles, per_tile * CHUNK // LANES, LANES)
    return loc.astype(jnp.int32).reshape(-1), gates_c, n_chunks.T.reshape(-1), clist.reshape(-1)


def _pdot_kernel(nch_ref, cl_ref, idx_ref, x_ref, u_ref, d_ref, rbuf_ref, *, n_blocks):
    b = pl.program_id(0)
    i = pl.program_id(1)
    n_tiles = pl.num_programs(1)
    n_chunks = nch_ref[b * n_tiles + i]
    first = jnp.int32(0)
    for bb in range(n_blocks - 1):
        first = first + jnp.where(bb < b, nch_ref[bb * n_tiles + i], 0)
    zero_rows = jnp.zeros((SUBLANES, LANES), F32)
    per_group = LANES // CHUNK
    n_full = n_chunks // per_group
    n_groups = (n_chunks + per_group - 1) // per_group
    dummy_row = d_ref.shape[0] - SUBLANES

    rbuf_ref[0:LANES, :] = jnp.zeros((LANES, LANES), F32)
    d_ref[...] = jnp.zeros(d_ref.shape, F32)

    def chunk(q):
        entry = cl_ref[first + q]
        base = entry & 0xFFFF
        x = x_ref[entry >> 16]
        x_lo, x_hi = x[:SUBLANES], x[SUBLANES:]
        for h in range(CHUNK // SUBLANES):
            parts = []
            for k in range(SUBLANES):
                row = pl.multiple_of(idx_ref[base + h * SUBLANES + k], SUBLANES)
                lo, hi = _unpack_rows(u_ref[pl.ds(row, SUBLANES), :])
                parts.append(lo * x_lo + hi * x_hi)
            row0 = pl.multiple_of(LANES + q * CHUNK + h * SUBLANES, SUBLANES)
            rbuf_ref[pl.ds(row0, SUBLANES), :] = _sublane_fold(parts)

    def reduce_rows(first_row, out_row):
        r = rbuf_ref[pl.ds(pl.multiple_of(first_row, LANES), LANES), :]
        d_ref[pl.ds(out_row, 1), :] = jnp.sum(r.T, axis=0, keepdims=True)

    def group_body(g, carry):
        for c in range(per_group):
            chunk(g * per_group + c)
        reduce_rows(g * LANES, jnp.where(g == 0, dummy_row, g - 1))
        return carry

    lax.fori_loop(0, n_full, group_body, 0)

    def tail_body(q, carry):
        chunk(q)
        return carry

    lax.fori_loop(n_full * per_group, n_chunks, tail_body, 0)

    def zero_body(f, carry):
        rbuf_ref[pl.ds(pl.multiple_of(LANES + f * SUBLANES, SUBLANES), SUBLANES), :] = zero_rows
        return carry

    folds_per_chunk = CHUNK // SUBLANES
    lax.fori_loop(n_chunks * folds_per_chunk, n_groups * per_group * folds_per_chunk, zero_body, 0)

    def last_body(g, carry):
        reduce_rows((g + 1) * LANES, g)
        return carry

    lax.fori_loop(jnp.maximum(n_full - 1, 0), n_groups, last_body, 0)


def _expert_block_spec():
    return pl.BlockSpec((EXPERT_BLOCK * SUBLANES, LANES), lambda b, i: (b, 0), pipeline_mode=pl.Buffered(1))


def _peer_dots(n_chunks, clist, idx, x3, u_rows, slots):
    T = x3.shape[0]
    nb = u_rows.shape[0] // (EXPERT_BLOCK * SUBLANES)
    tm = PEER_TM
    n_tiles = T // tm
    max_pairs = tm * slots
    groups = max_pairs // LANES + SUBLANES
    smem_tile = lambda n: pl.BlockSpec((tm * n,), lambda b, i: (i,), memory_space=pltpu.SMEM)
    out = jax.ShapeDtypeStruct((nb, n_tiles, groups, LANES), F32)
    out_spec = pl.BlockSpec((None, None, groups, LANES), lambda b, i: (b, i, 0, 0))
    return pl.pallas_call(
        functools.partial(_pdot_kernel, n_blocks=nb),
        grid=(nb, n_tiles),
        in_specs=[
            pl.BlockSpec(memory_space=pltpu.SMEM),
            pl.BlockSpec((_chunk_list_stride(slots // CHUNK),), lambda b, i: (i,), memory_space=pltpu.SMEM),
            smem_tile(slots),
            pl.BlockSpec((tm, ROW_SUBLANES, LANES), lambda b, i: (i, 0, 0)),
            _expert_block_spec(),
        ],
        out_specs=out_spec,
        out_shape=out,
        scratch_shapes=[pltpu.VMEM((LANES + max_pairs, LANES), F32)],
        compiler_params=_cparams(("arbitrary", "arbitrary")),
        name="pdot",
    )(n_chunks, clist, idx, x3, u_rows)


def _pacc_kernel(nch_ref, cl_ref, idx_ref, d_ref, g_ref, v_ref, o_ref, wrep_ref, wrow_ref, *, n_blocks):
    b = pl.program_id(0)
    i = pl.program_id(1)
    n_tiles = pl.num_programs(1)
    n_chunks = nch_ref[b * n_tiles + i]
    first = jnp.int32(0)
    for bb in range(n_blocks - 1):
        first = first + jnp.where(bb < b, nch_ref[bb * n_tiles + i], 0)

    per_group = LANES // CHUNK
    n_full = n_chunks // per_group

    def expand(g):
        tile = (SUBLANES, LANES)
        wrow_ref[...] = (jnp.broadcast_to(g_ref[pl.ds(g, 1), :], tile)
                         * jax.nn.gelu(jnp.broadcast_to(d_ref[pl.ds(g, 1), :], tile)))
        rows = jnp.broadcast_to(wrow_ref[0:1, :], (LANES, LANES))
        wrep_ref[pl.ds(pl.multiple_of(g * LANES, LANES), LANES), :] = rows.T

    expand(0)
    o_ref[...] = jnp.zeros(o_ref.shape, F32)

    def chunk(q):
        entry = cl_ref[first + q]
        base = entry & 0xFFFF
        tok = entry >> 16
        zero = jnp.zeros((SUBLANES, LANES), F32)
        accs = [zero, zero, zero, zero]
        for k in range(CHUNK):
            w = jnp.broadcast_to(wrep_ref[pl.ds(q * CHUNK + k, 1), :], (SUBLANES, LANES))
            lo, hi = _unpack_rows(v_ref[pl.ds(pl.multiple_of(idx_ref[base + k], SUBLANES), SUBLANES), :])
            j = 2 * (k % 2)
            accs[j] = accs[j] + w * lo
            accs[j + 1] = accs[j + 1] + w * hi
        o_ref[tok] = o_ref[tok] + jnp.concatenate([accs[0] + accs[2], accs[1] + accs[3]], axis=0)

    def group_body(g, carry):
        expand(g + 1)
        for c in range(per_group):
            chunk(g * per_group + c)
        return carry

    lax.fori_loop(0, n_full, group_body, 0)

    def tail_body(q, carry):
        chunk(q)
        return carry

    lax.fori_loop(n_full * per_group, n_chunks, tail_body, 0)


def _peer_accumulate(n_chunks, clist, idx, dots, gates_c, v_rows, T, slots):
    nb = v_rows.shape[0] // (EXPERT_BLOCK * SUBLANES)
    tm = PEER_TM
    n_tiles = T // tm
    groups = dots.shape[2]
    assert groups == tm * slots // LANES + SUBLANES and gates_c.shape == dots.shape
    smem_tile = lambda n: pl.BlockSpec((tm * n,), lambda b, i: (i,), memory_space=pltpu.SMEM)
    per_step = lambda: pl.BlockSpec((None, None, groups, LANES), lambda b, i: (b, i, 0, 0))
    return pl.pallas_call(
        functools.partial(_pacc_kernel, n_blocks=nb),
        grid=(nb, n_tiles),
        in_specs=[
            pl.BlockSpec(memory_space=pltpu.SMEM),
            pl.BlockSpec((_chunk_list_stride(slots // CHUNK),), lambda b, i: (i,), memory_space=pltpu.SMEM),
            smem_tile(slots),
            per_step(), per_step(),
            _expert_block_spec(),
        ],
        out_specs=pl.BlockSpec((None, tm, ROW_SUBLANES, LANES), lambda b, i: (b, i, 0, 0)),
        out_shape=jax.ShapeDtypeStruct((nb, T, ROW_SUBLANES, LANES), F32),
        scratch_shapes=[pltpu.VMEM((tm * slots + LANES, LANES), F32), pltpu.VMEM((SUBLANES, LANES), F32)],
        compiler_params=_cparams(("arbitrary", "arbitrary")),
        name="pacc",
    )(n_chunks, clist, idx, dots, gates_c, v_rows)


def _final_kernel(x1_ref, p_ref, g_ref, o_ref):
    nb, _, n_s, _ = p_ref.shape
    cols = []
    for s in range(n_s):
        col = x1_ref[:, s * LANES:(s + 1) * LANES]
        for b in range(nb):
            col = col + p_ref[b, :, s, :]
        cols.append(col)
    o_ref[...] = _rms(jnp.concatenate(cols, axis=1), g_ref[...])


def _final(x1, peer_parts, g_final):
    T, D = x1.shape
    nb = peer_parts.shape[0]
    tm = 256
    return pl.pallas_call(
        _final_kernel,
        grid=(T // tm,),
        in_specs=[
            pl.BlockSpec((tm, D), lambda i: (i, 0)),
            pl.BlockSpec((nb, tm, D // LANES, LANES), lambda i: (0, i, 0, 0)),
            pl.BlockSpec((1, D), lambda i: (0, 0)),
        ],
        out_specs=pl.BlockSpec((tm, D), lambda i: (i, 0)),
        out_shape=jax.ShapeDtypeStruct((T, D), F32),
        compiler_params=_cparams(("parallel",)),
        name="final",
    )(x1, peer_parts, g_final)


def _rope_tables(seq):
    half = HEAD_DIM // 2
    inv = ROPE_THETA ** (-jnp.arange(half, dtype=F32) / half)
    ang = jnp.arange(seq).astype(F32)[:, None] * inv[None, :]
    cos, sin = jnp.cos(ang), jnp.sin(ang)
    reps = LANES // HEAD_DIM
    cos_t = jnp.concatenate([cos, cos] * reps, axis=1)
    sin_t = jnp.concatenate([-sin, sin] * reps, axis=1)
    return cos_t, sin_t


def _layer(xt, batch, seq, g_mix, w_in, sink, conv_w, conv_b, fwd_wa, fwd_ba, fwd_wx, fwd_bx, fwd_lam,
           bwd_wa, bwd_ba, bwd_wx, bwd_bx, bwd_lam, g_attn_out, g_lru_out, w_out, g_ffn,
           w_pq, sub_k1, sub_k2, u_emb, v_emb):
    T, D = xt.shape
    cos_t, sin_t = _rope_tables(seq)
    qkv, gr = _inproj(xt, g_mix.reshape(1, D), w_in.astype(BF16), cos_t, sin_t, seq)
    y_attn = _attention(qkv, sink, batch, seq)
    y_lru = _lru(gr, conv_w, conv_b, (fwd_wa, fwd_wx, bwd_wa, bwd_wx), (fwd_ba, fwd_bx, bwd_ba, bwd_bx),
                 (fwd_lam, bwd_lam), batch, seq)
    x1, xn, x3 = _outproj(y_attn, y_lru, xt, g_attn_out.reshape(1, -1), g_lru_out.reshape(1, -1),
                      w_out.astype(BF16), g_ffn.reshape(1, D))

    keys = jnp.concatenate([sub_k1, sub_k2], axis=0).astype(BF16)
    scores = _peer_scores(xn, w_pq.astype(BF16), keys)
    gates_t, eidx_t = _peer_topk(scores)
    n_pairs = PEER_HEADS * PEER_TOPK
    gates = gates_t.transpose(2, 0, 1).reshape(T, n_pairs)
    eidx = eidx_t.transpose(2, 0, 1).reshape(T, n_pairs)
    n_exp = u_emb.shape[0]
    nb = n_exp // EXPERT_BLOCK
    slots = n_pairs + nb * CHUNK
    idx, gates_c, n_chunks, clist = _route(eidx, gates, nb)

    u_rows = _pack_rows(u_emb)
    v_rows = _pack_rows(v_emb)
    dots = _peer_dots(n_chunks, clist, idx, x3, u_rows, slots)
    gates_c = jnp.pad(gates_c, ((0, 0), (0, 0), (0, dots.shape[2] - gates_c.shape[2]), (0, 0)))
    parts = _peer_accumulate(n_chunks, clist, idx, dots, gates_c, v_rows, T, slots)
    return x1, parts


def kernel(x, g_mix, w_in, sink, conv_w, conv_b, fwd_wa, fwd_ba, fwd_wx, fwd_bx, fwd_lam, bwd_wa, bwd_ba, bwd_wx, bwd_bx, bwd_lam, g_attn_out, g_lru_out, w_out, g_ffn, w_pq, sub_k1, sub_k2, u_emb, v_emb, g_final):
    B, S, D = x.shape
    assert g_mix.shape[0] == 1, "single-layer trunk"
    xt = x.reshape(B * S, D)
    x1, parts = _layer(xt, B, S, g_mix[0], w_in[0], sink[0], conv_w[0], conv_b[0],
                       fwd_wa[0], fwd_ba[0], fwd_wx[0], fwd_bx[0], fwd_lam[0],
                       bwd_wa[0], bwd_ba[0], bwd_wx[0], bwd_bx[0], bwd_lam[0],
                       g_attn_out[0], g_lru_out[0], w_out[0], g_ffn[0],
                       w_pq[0], sub_k1[0], sub_k2[0], u_emb[0], v_emb[0])
    return _final(x1, parts, g_final.reshape(1, D)).reshape(B, S, D)
```

```python
import functools

import jax
import jax.numpy as jnp
from jax import lax
from jax.experimental import pallas as pl
from jax.experimental.pallas import tpu as pltpu

F32 = jnp.float32
BF16 = jnp.bfloat16

HEAD_DIM = 64
N_KV_HEADS = 4
Q_PER_KV = 4
N_Q_HEADS = N_KV_HEADS * Q_PER_KV
WINDOW = 128
ATTN_BLOCK = 128
ROPE_THETA = 10000.0
LRU_BLOCK = 64
CONV_WIDTH = 4
CONV_PAD_LEFT = 2
LRU_C = 8.0
PEER_HEADS = 8
PEER_HALF = 128
N_KEYS = 128
PEER_TOPK = 16
EPS = 1e-6
NEG = -1e30

LANES = 128
SUBLANES = 8
VMEM_LIMIT = 56 * 1024 * 1024

INPROJ_TM = 512
INPROJ_TN = 512
LRU_CHANNELS = 256
LRU_CHUNK = 256
OUTPROJ_TM = 256
PQ_TM = 256
TOPK_TL = 512
EXPERT_BLOCK = 8192
PEER_TM = 128
CHUNK = 16
GATE_BITS = 18
GATE_SHIFT = 31 - GATE_BITS


def _cparams(sem):
    return pltpu.CompilerParams(dimension_semantics=sem, vmem_limit_bytes=VMEM_LIMIT)


def _rms(x, g):
    return x * lax.rsqrt(jnp.mean(x * x, axis=-1, keepdims=True) + EPS) * g


def _inproj_kernel(x_ref, g_ref, w_ref, cos_ref, sin_ref, qkv_ref, gr_ref, h_ref, *, n_rope_tiles, kv_cols):
    j = pl.program_id(1)

    @pl.when(j == 0)
    def _():
        h_ref[...] = _rms(x_ref[...], g_ref[...]).astype(BF16)

    acc = jnp.dot(h_ref[...], w_ref[...], preferred_element_type=F32)
    tm, tn = acc.shape

    @pl.when(j < n_rope_tiles)
    def _():
        reps = tn // LANES
        cos = jnp.concatenate([cos_ref[...]] * reps, axis=1)
        sin = jnp.concatenate([sin_ref[...]] * reps, axis=1)
        lane = lax.broadcasted_iota(jnp.int32, (tm, tn), 1)
        first = (lane % HEAD_DIM) < (HEAD_DIM // 2)
        partner = jnp.where(first, pltpu.roll(acc, tn - HEAD_DIM // 2, 1), pltpu.roll(acc, HEAD_DIM // 2, 1))
        roped = acc * cos + partner * sin
        is_rope = jnp.logical_or(j < n_rope_tiles - 1, lane < kv_cols)
        qkv_ref[...] = jnp.where(is_rope, roped, acc).astype(BF16)

    @pl.when(j >= n_rope_tiles)
    def _():
        gr_ref[...] = acc


def _inproj(xt, g_mix, w_in_bf, cos_t, sin_t, seq):
    T, D = xt.shape
    n_cols = w_in_bf.shape[1]
    q_cols = N_Q_HEADS * HEAD_DIM
    kv_cols = N_KV_HEADS * HEAD_DIM
    qkv_cols = q_cols + 2 * kv_cols
    tm, tn = INPROJ_TM, INPROJ_TN
    n_rope_tiles = qkv_cols // tn
    assert qkv_cols % tn == 0 and (q_cols % tn == 0) and seq % tm == 0 and T % tm == 0
    pos_blocks = seq // tm
    kern = functools.partial(_inproj_kernel, n_rope_tiles=n_rope_tiles, kv_cols=kv_cols)
    return pl.pallas_call(
        kern,
        grid=(T // tm, n_cols // tn),
        in_specs=[
            pl.BlockSpec((tm, D), lambda i, j: (i, 0)),
            pl.BlockSpec((1, D), lambda i, j: (0, 0)),
            pl.BlockSpec((D, tn), lambda i, j: (0, j)),
            pl.BlockSpec((tm, LANES), lambda i, j: (i % pos_blocks, 0)),
            pl.BlockSpec((tm, LANES), lambda i, j: (i % pos_blocks, 0)),
        ],
        out_specs=[
            pl.BlockSpec((tm, tn), lambda i, j: (i, jnp.minimum(j, n_rope_tiles - 1))),
            pl.BlockSpec((tm, tn), lambda i, j: (i, jnp.maximum(j - n_rope_tiles, 0))),
        ],
        out_shape=[
            jax.ShapeDtypeStruct((T, qkv_cols), BF16),
            jax.ShapeDtypeStruct((T, n_cols - qkv_cols), F32),
        ],
        scratch_shapes=[pltpu.VMEM((tm, D), BF16)],
        compiler_params=_cparams(("parallel", "arbitrary")),
        name="inproj",
    )(xt, g_mix, w_in_bf, cos_t, sin_t)


def _attn_kernel(sink_ref, q_ref, kvp_ref, kvc_ref, kvn_ref, o_ref, *, seq):
    n = pl.program_id(1)
    q = q_ref[...] * jnp.asarray(HEAD_DIM ** -0.5, BF16)
    kv = jnp.concatenate([kvp_ref[...], kvc_ref[...], kvn_ref[...]], axis=0)
    kv_cols = N_KV_HEADS * HEAD_DIM
    k, v = kv[:, :kv_cols], kv[:, kv_cols:]
    nq, nk = ATTN_BLOCK, 3 * ATTN_BLOCK
    qpos = n * ATTN_BLOCK + lax.broadcasted_iota(jnp.int32, (nq, nk), 0)
    kpos = (n - 1) * ATTN_BLOCK + lax.broadcasted_iota(jnp.int32, (nq, nk), 1)
    valid = (jnp.abs(kpos - qpos) <= WINDOW) & (kpos >= 0) & (kpos < seq)
    outs = []
    for g in range(N_KV_HEADS):
        kg = k[:, g * HEAD_DIM:(g + 1) * HEAD_DIM]
        vg = jnp.concatenate([v[:, g * HEAD_DIM:(g + 1) * HEAD_DIM], jnp.ones((nk, HEAD_DIM), BF16)], axis=1)
        for r in range(Q_PER_KV):
            h = g * Q_PER_KV + r
            qh = q[:, h * HEAD_DIM:(h + 1) * HEAD_DIM]
            s = lax.dot_general(qh, kg, (((1,), (1,)), ((), ())), preferred_element_type=F32)
            s = jnp.where(valid, s, NEG)
            sk = sink_ref[h]
            m = jnp.maximum(jnp.max(s, axis=-1, keepdims=True), sk)
            p = jnp.exp(s - m)
            ov = jnp.dot(p.astype(BF16), vg, preferred_element_type=F32)
            den = ov[:, HEAD_DIM:HEAD_DIM + 1] + jnp.exp(sk - m)
            outs.append(ov[:, :HEAD_DIM] / den)
    o_ref[...] = jnp.concatenate(outs, axis=1)


def _attention(qkv, sink, batch, seq):
    T = qkv.shape[0]
    nb = seq // ATTN_BLOCK
    q_cols = N_Q_HEADS * HEAD_DIM
    kv_cols = N_KV_HEADS * HEAD_DIM
    kv_blk = q_cols // (2 * kv_cols)
    blk = ATTN_BLOCK

    def row(b, n):
        return b * nb + n

    def kv_spec(shift):
        return pl.BlockSpec((blk, 2 * kv_cols), lambda b, n: (row(b, jnp.clip(n + shift, 0, nb - 1)), kv_blk))

    return pl.pallas_call(
        functools.partial(_attn_kernel, seq=seq),
        grid=(batch, nb),
        in_specs=[
            pl.BlockSpec(memory_space=pltpu.SMEM),
            pl.BlockSpec((blk, q_cols), lambda b, n: (row(b, n), 0)),
            kv_spec(-1), kv_spec(0), kv_spec(1),
        ],
        out_specs=pl.BlockSpec((blk, q_cols), lambda b, n: (row(b, n), 0)),
        out_shape=jax.ShapeDtypeStruct((T, q_cols), F32),
        compiler_params=_cparams(("parallel", "parallel")),
        name="attn",
    )(sink, qkv, qkv, qkv, qkv)


def _scan_chunk(a, u, reverse):
    L, C = a.shape
    row = lax.broadcasted_iota(jnp.int32, (L, C), 0)
    A, H = a, u
    d = 1
    while d < L:
        if d < SUBLANES:
            if reverse:
                keep = row < L - d
                As = jnp.where(keep, pltpu.roll(A, L - d, 0), 1.0)
                Hs = jnp.where(keep, pltpu.roll(H, L - d, 0), 0.0)
            else:
                keep = row >= d
                As = jnp.where(keep, pltpu.roll(A, d, 0), 1.0)
                Hs = jnp.where(keep, pltpu.roll(H, d, 0), 0.0)
        else:
            one = jnp.ones((d, C), F32)
            zero = jnp.zeros((d, C), F32)
            if reverse:
                As = jnp.concatenate([A[d:], one], axis=0)
                Hs = jnp.concatenate([H[d:], zero], axis=0)
            else:
                As = jnp.concatenate([one, A[:L - d]], axis=0)
                Hs = jnp.concatenate([zero, H[:L - d]], axis=0)
        H = A * Hs + H
        A = A * As
        d *= 2
    return A, H


def _lru_kernel(xg_ref, xr_ref, cw_ref, cb_ref, wg_ref, bg_ref, lam_ref, y_ref, xp_ref, hf_ref):
    S, C = xr_ref.shape
    L = LRU_CHUNK
    n_chunks = S // L
    halo = SUBLANES

    xp_ref[0:halo, :] = jnp.zeros((halo, C), F32)
    xp_ref[S + halo:S + 2 * halo, :] = jnp.zeros((halo, C), F32)

    def copy_body(ci, carry):
        t0 = pl.multiple_of(ci * L, L)
        xp_ref[pl.ds(t0 + halo, L), :] = xr_ref[pl.ds(t0, L), :]
        return carry

    lax.fori_loop(0, n_chunks, copy_body, 0)

    cw = cw_ref[...]
    cb = cb_ref[...]
    bias = bg_ref[0]
    neg_c_softplus = -LRU_C * jax.nn.softplus(-lam_ref[0])

    def conv_chunk(t0):
        win = xp_ref[pl.ds(t0, L + 2 * halo), :]
        acc = cb
        for j in range(CONV_WIDTH):
            off = halo - CONV_PAD_LEFT + j
            acc = acc + cw[j:j + 1, :] * win[off:off + L, :]
        return acc

    def gate_au(c, direction):
        w = wg_ref[0, :, direction * 2 * C:(direction + 1) * 2 * C]
        z = jnp.dot(c.astype(BF16), w, preferred_element_type=F32) + bias[:, direction * 2 * C:(direction + 1) * 2 * C]
        r = jax.nn.sigmoid(z[:, :C])
        i = jax.nn.sigmoid(z[:, C:])
        log_a = r * neg_c_softplus[direction:direction + 1, :]
        a = jnp.exp(log_a)
        u = jnp.sqrt(1.0 - a * a) * (i * c)
        return a, u

    def fwd_body(ci, carry):
        t0 = pl.multiple_of(ci * L, L)
        c = conv_chunk(t0)
        a, u = gate_au(c, 0)
        A, H = _scan_chunk(a, u, reverse=False)
        h = H + A * carry
        hf_ref[pl.ds(t0, L), :] = h
        return h[L - 1:L, :]

    lax.fori_loop(0, n_chunks, fwd_body, jnp.zeros((1, C), F32))

    def bwd_body(k, carry):
        t0 = pl.multiple_of((n_chunks - 1 - k) * L, L)
        c = conv_chunk(t0)
        a, u = gate_au(c, 1)
        A, H = _scan_chunk(a, u, reverse=True)
        h = H + A * carry
        y_ref[pl.ds(t0, L), :] = jax.nn.gelu(xg_ref[pl.ds(t0, L), :]) * (hf_ref[pl.ds(t0, L), :] + h)
        return h[0:1, :]

    lax.fori_loop(0, n_chunks, bwd_body, jnp.zeros((1, C), F32))


def _block_diag_chunks(w, per_chunk):
    nblk, b, _ = w.shape
    w4 = w.reshape(nblk // per_chunk, per_chunk, b, b)
    eye = jnp.eye(per_chunk, dtype=w.dtype)
    m = w4[:, :, :, None, :] * eye[None, :, None, :, None]
    return m.reshape(nblk // per_chunk, per_chunk * b, per_chunk * b)


def _lru(gr, conv_w, conv_b, gate_ws, gate_bs, lams, batch, seq):
    T, two_w = gr.shape
    W = two_w // 2
    C = LRU_CHANNELS
    n_ch = W // C
    per_chunk = C // LRU_BLOCK
    wg = jnp.concatenate([_block_diag_chunks(w, per_chunk) for w in gate_ws], axis=-1).astype(BF16)
    bg = jnp.concatenate([b.reshape(n_ch, 1, C) for b in gate_bs], axis=-1)
    lam = jnp.stack([l.reshape(n_ch, C) for l in lams], axis=1)
    return pl.pallas_call(
        _lru_kernel,
        grid=(batch, n_ch),
        in_specs=[
            pl.BlockSpec((seq, C), lambda b, c: (b, c)),
            pl.BlockSpec((seq, C), lambda b, c: (b, n_ch + c)),
            pl.BlockSpec((CONV_WIDTH, C), lambda b, c: (0, c)),
            pl.BlockSpec((1, C), lambda b, c: (0, c)),
            pl.BlockSpec((1, C, 4 * C), lambda b, c: (c, 0, 0)),
            pl.BlockSpec((1, 1, 4 * C), lambda b, c: (c, 0, 0)),
            pl.BlockSpec((1, 2, C), lambda b, c: (c, 0, 0)),
        ],
        out_specs=pl.BlockSpec((seq, C), lambda b, c: (b, c)),
        out_shape=jax.ShapeDtypeStruct((T, W), F32),
        scratch_shapes=[pltpu.VMEM((seq + 2 * SUBLANES, C), F32), pltpu.VMEM((seq, C), F32)],
        compiler_params=_cparams(("parallel", "parallel")),
        name="lru",
    )(gr, gr, conv_w, conv_b.reshape(1, W), wg, bg, lam)


def _outproj_kernel(ya_ref, yl_ref, x_ref, ga_ref, gl_ref, w_ref, gf_ref, x1_ref, xn_ref, xs_ref):
    y = jnp.concatenate([_rms(ya_ref[...], ga_ref[...]), _rms(yl_ref[...], gl_ref[...])], axis=1).astype(BF16)
    x1 = x_ref[...] + jnp.dot(y, w_ref[...], preferred_element_type=F32)
    x1_ref[...] = x1
    xn = _rms(x1, gf_ref[...])
    xn_ref[...] = xn.astype(BF16)
    for s in range(xs_ref.shape[1]):
        xs_ref[:, s, :] = xn[:, s * LANES:(s + 1) * LANES]


def _outproj(y_attn, y_lru, xt, g_attn, g_lru, w_out_bf, g_ffn):
    T, D = xt.shape
    wa, wl = y_attn.shape[1], y_lru.shape[1]
    tm = OUTPROJ_TM
    return pl.pallas_call(
        _outproj_kernel,
        grid=(T // tm,),
        in_specs=[
            pl.BlockSpec((tm, wa), lambda i: (i, 0)),
            pl.BlockSpec((tm, wl), lambda i: (i, 0)),
            pl.BlockSpec((tm, D), lambda i: (i, 0)),
            pl.BlockSpec((1, wa), lambda i: (0, 0)),
            pl.BlockSpec((1, wl), lambda i: (0, 0)),
            pl.BlockSpec((wa + wl, D), lambda i: (0, 0)),
            pl.BlockSpec((1, D), lambda i: (0, 0)),
        ],
        out_specs=[pl.BlockSpec((tm, D), lambda i: (i, 0)), pl.BlockSpec((tm, D), lambda i: (i, 0)),
                   pl.BlockSpec((tm, D // LANES, LANES), lambda i: (i, 0, 0))],
        out_shape=[jax.ShapeDtypeStruct((T, D), F32), jax.ShapeDtypeStruct((T, D), BF16),
                   jax.ShapeDtypeStruct((T, D // LANES, LANES), F32)],
        compiler_params=_cparams(("parallel",)),
        name="outproj",
    )(y_attn, y_lru, xt, g_attn, g_lru, w_out_bf, g_ffn)


def _pq_kernel(xn_ref, w_ref, k_ref, s_ref):
    q = jnp.dot(xn_ref[...], w_ref[...], preferred_element_type=F32).astype(BF16)
    for hh in range(2 * PEER_HEADS):
        half, head = divmod(hh, PEER_HEADS)
        col = (head * 2 + half) * PEER_HALF
        qh = q[:, col:col + PEER_HALF]
        s_ref[hh] = lax.dot_general(k_ref[hh], qh, (((1,), (1,)), ((), ())), preferred_element_type=F32)


def _peer_scores(xn, w_pq_bf, keys_bf):
    T, D = xn.shape
    tm = PQ_TM
    nh = keys_bf.shape[0]
    return pl.pallas_call(
        _pq_kernel,
        grid=(T // tm,),
        in_specs=[
            pl.BlockSpec((tm, D), lambda i: (i, 0)),
            pl.BlockSpec(w_pq_bf.shape, lambda i: (0, 0)),
            pl.BlockSpec(keys_bf.shape, lambda i: (0, 0, 0)),
        ],
        out_specs=pl.BlockSpec((nh, N_KEYS, tm), lambda i: (0, 0, i)),
        out_shape=jax.ShapeDtypeStruct((nh, N_KEYS, T), F32),
        compiler_params=_cparams(("parallel",)),
        name="pq",
    )(xn, w_pq_bf, keys_bf)


def _top16_rows(s, payload=None):
    n, tl = s.shape
    row = lax.broadcasted_iota(jnp.int32, (n, tl), 0)
    vals, idxs, pays = [], [], []
    for _ in range(PEER_TOPK):
        m = jnp.max(s, axis=0, keepdims=True)
        idx = jnp.min(jnp.where(s == m, row, n), axis=0, keepdims=True)
        hit = row == idx
        vals.append(m)
        idxs.append(idx)
        if payload is not None:
            pays.append(jnp.max(jnp.where(hit, payload, -1), axis=0, keepdims=True))
        s = jnp.where(hit, -jnp.inf, s)
    cat = lambda xs: jnp.concatenate(xs, axis=0)
    return cat(vals), cat(idxs), (cat(pays) if payload is not None else None)


def _topk_kernel(s1_ref, s2_ref, g_ref, e_ref):
    v1, i1, _ = _top16_rows(s1_ref[0])
    v2, i2, _ = _top16_rows(s2_ref[0])
    K = PEER_TOPK
    tl = v1.shape[1]
    sub = lax.broadcasted_iota(jnp.int32, (SUBLANES, tl), 0)
    cand, cidx = [], []
    for a in range(K // 2):
        n_b = K // (a + 1)
        for b0 in range(0, n_b, SUBLANES):
            keep = sub < (n_b - b0)
            cand.append(jnp.where(keep, v1[a:a + 1, :] + v2[b0:b0 + SUBLANES, :], -jnp.inf))
            cidx.append(i1[a:a + 1, :] * N_KEYS + i2[b0:b0 + SUBLANES, :])
    cand.append(v1[K // 2:, :] + v2[0:1, :])
    cidx.append(i1[K // 2:, :] * N_KEYS + i2[0:1, :])
    sc, _, eidx = _top16_rows(jnp.concatenate(cand, axis=0), jnp.concatenate(cidx, axis=0))
    ex = jnp.exp(sc - sc[0:1, :])
    g_ref[0] = ex / jnp.sum(ex, axis=0, keepdims=True)
    e_ref[0] = eidx


def _peer_topk(scores):
    nh2, nk, T = scores.shape
    nh = nh2 // 2
    tl = TOPK_TL
    return pl.pallas_call(
        _topk_kernel,
        grid=(nh, T // tl),
        in_specs=[
            pl.BlockSpec((1, nk, tl), lambda h, i: (h, 0, i)),
            pl.BlockSpec((1, nk, tl), lambda h, i: (nh + h, 0, i)),
        ],
        out_specs=[
            pl.BlockSpec((1, PEER_TOPK, tl), lambda h, i: (h, 0, i)),
            pl.BlockSpec((1, PEER_TOPK, tl), lambda h, i: (h, 0, i)),
        ],
        out_shape=[
            jax.ShapeDtypeStruct((nh, PEER_TOPK, T), F32),
            jax.ShapeDtypeStruct((nh, PEER_TOPK, T), jnp.int32),
        ],
        compiler_params=_cparams(("parallel", "parallel")),
        name="topk",
    )(scores, scores)


ROW_SUBLANES = 16


def _pack_rows(emb):
    n, d = emb.shape
    half = d // 2
    assert half == SUBLANES * LANES
    bits = lax.bitcast_convert_type(emb.astype(BF16), jnp.uint16).astype(jnp.uint32)
    return (bits[:, :half] | (bits[:, half:] << 16)).reshape(n * SUBLANES, LANES)


def _unpack_rows(word):
    return pltpu.bitcast(word << 16, F32), pltpu.bitcast(word & jnp.uint32(0xFFFF0000), F32)


def _sublane_fold(parts):
    sub = lax.broadcasted_iota(jnp.int32, (SUBLANES, LANES), 0)
    step = 1
    while len(parts) > 1:
        low = (sub & step) == 0
        parts = [jnp.where(low, a, b) + pltpu.roll(jnp.where(low, b, a), step, 0)
                 for a, b in zip(parts[0::2], parts[1::2])]
        step *= 2
    return parts[0]


def _chunk_list_stride(chunks_per_token):
    return -(-PEER_TM * chunks_per_token // 1024) * 1024


def _route(eidx, gates, n_blocks):
    T, P = eidx.shape
    u32 = jnp.uint32
    blocks = jnp.arange(n_blocks, dtype=jnp.int32)
    blk = eidx // EXPERT_BLOCK
    cnt = jnp.sum((blk[:, :, None] == blocks[None, None, :]).astype(jnp.int32), axis=1)
    pcnt = (cnt + CHUNK - 1) // CHUNK * CHUNK
    cand = jnp.arange(CHUNK, dtype=jnp.int32)
    active = cand[None, None, :] < (pcnt - cnt)[:, :, None]
    gate_bits = (lax.bitcast_convert_type(gates, u32) + u32(1 << (GATE_SHIFT - 1))) >> GATE_SHIFT
    pad_mark = u32((1 << GATE_BITS) - 1)
    word = (eidx.astype(u32) << GATE_BITS) | gate_bits
    last_row = ((blocks + 1) * EXPERT_BLOCK - 1).astype(u32)
    pad_word = jnp.where(active, ((last_row << GATE_BITS) | pad_mark)[None, :, None], u32(0xFFFFFFFF))
    srt = lax.sort(jnp.concatenate([word, pad_word.reshape(T, n_blocks * CHUNK)], axis=1), dimension=1)
    is_pad = (srt & pad_mark) == pad_mark
    loc = jnp.where(is_pad, u32(0), ((srt >> GATE_BITS) & u32(EXPERT_BLOCK - 1)) * SUBLANES)
    gate = lax.bitcast_convert_type(jnp.where(is_pad, u32(0), (srt & pad_mark) << GATE_SHIFT), F32)
    ends = jnp.cumsum(pcnt, axis=1) // CHUNK
    n_chunks = jnp.sum((pcnt // CHUNK).reshape(T // PEER_TM, PEER_TM, n_blocks), axis=1)

    per_tok = (P + n_blocks * CHUNK) // CHUNK
    slot_j = jnp.arange(per_tok, dtype=jnp.int32)
    slot_blk = jnp.sum((slot_j[None, :, None] >= ends[:, None, :]).astype(jnp.int32), axis=2)
    tok = (jnp.arange(T, dtype=jnp.int32) % PEER_TM)[:, None]
    entry = (tok << 16) | (tok * (per_tok * CHUNK) + slot_j[None, :] * CHUNK)
    assert PEER_TM <= 1 << 8
    order = lax.sort(((slot_blk << 24) | entry).reshape(T // PEER_TM, PEER_TM * per_tok), dimension=1)
    clist = order & ((1 << 24) - 1)
    clist = jnp.pad(clist, ((0, 0), (0, _chunk_list_stride(per_tok) - PEER_TM * per_tok)))

    n_tiles, per_tile = T // PEER_TM, PEER_TM * per_tok
    gate_rows = gate.reshape(n_tiles, per_tile, CHUNK)
    q_pos = jnp.arange(per_tile, dtype=jnp.int32)
    gates_c = []
    for b in range(n_blocks):
        lst = lax.sort((((slot_blk != b).astype(jnp.int32) << 24) | entry).reshape(n_tiles, per_tile), dimension=1)
        src = (lst & 0xFFFF) // CHUNK
        rows = jnp.take_along_axis(gate_rows, src[:, :, None], axis=1)
        gates_c.append(jnp.where((q_pos[None, :] < n_chunks[:, b][:, None])[:, :, None], rows, 0.0))
    gates_c = jnp.stack(gates_c).reshape(n_blocks, n_tiles, per_tile * CHUNK // LANES, LANES)
    return loc.astype(jnp.int32).reshape(-1), gates_c, n_chunks.T.reshape(-1), clist.reshape(-1)


def _pdot_kernel(nch_ref, cl_ref, idx_ref, x_ref, u_ref, d_ref, rbuf_ref, *, n_blocks):
    b = pl.program_id(0)
    i = pl.program_id(1)
    n_tiles = pl.num_programs(1)
    n_chunks = nch_ref[b * n_tiles + i]
    first = jnp.int32(0)
    for bb in range(n_blocks - 1):
        first = first + jnp.where(bb < b, nch_ref[bb * n_tiles + i], 0)
    zero_rows = jnp.zeros((SUBLANES, LANES), F32)
    per_group = LANES // CHUNK
    n_full = n_chunks // per_group
    n_groups = (n_chunks + per_group - 1) // per_group
    dummy_row = d_ref.shape[0] - SUBLANES

    rbuf_ref[0:LANES, :] = jnp.zeros((LANES, LANES), F32)
    d_ref[...] = jnp.zeros(d_ref.shape, F32)

    def chunk(q):
        entry = cl_ref[first + q]
        base = entry & 0xFFFF
        x = x_ref[entry >> 16]
        x_lo, x_hi = x[:SUBLANES], x[SUBLANES:]
        for h in range(CHUNK // SUBLANES):
            parts = []
            for k in range(SUBLANES):
                row = pl.multiple_of(idx_ref[base + h * SUBLANES + k], SUBLANES)
                lo, hi = _unpack_rows(u_ref[pl.ds(row, SUBLANES), :])
                parts.append(lo * x_lo + hi * x_hi)
            row0 = pl.multiple_of(LANES + q * CHUNK + h * SUBLANES, SUBLANES)
            rbuf_ref[pl.ds(row0, SUBLANES), :] = _sublane_fold(parts)

    def reduce_rows(first_row, out_row):
        r = rbuf_ref[pl.ds(pl.multiple_of(first_row, LANES), LANES), :]
        d_ref[pl.ds(out_row, 1), :] = jnp.sum(r.T, axis=0, keepdims=True)

    def group_body(g, carry):
        for c in range(per_group):
            chunk(g * per_group + c)
        reduce_rows(g * LANES, jnp.where(g == 0, dummy_row, g - 1))
        return carry

    lax.fori_loop(0, n_full, group_body, 0)

    def tail_body(q, carry):
        chunk(q)
        return carry

    lax.fori_loop(n_full * per_group, n_chunks, tail_body, 0)

    def zero_body(f, carry):
        rbuf_ref[pl.ds(pl.multiple_of(LANES + f * SUBLANES, SUBLANES), SUBLANES), :] = zero_rows
        return carry

    folds_per_chunk = CHUNK // SUBLANES
    lax.fori_loop(n_chunks * folds_per_chunk, n_groups * per_group * folds_per_chunk, zero_body, 0)

    def last_body(g, carry):
        reduce_rows((g + 1) * LANES, g)
        return carry

    lax.fori_loop(jnp.maximum(n_full - 1, 0), n_groups, last_body, 0)


def _expert_block_spec():
    return pl.BlockSpec((EXPERT_BLOCK * SUBLANES, LANES), lambda b, i: (b, 0), pipeline_mode=pl.Buffered(1))


def _peer_dots(n_chunks, clist, idx, x3, u_rows, slots):
    T = x3.shape[0]
    nb = u_rows.shape[0] // (EXPERT_BLOCK * SUBLANES)
    tm = PEER_TM
    n_tiles = T // tm
    max_pairs = tm * slots
    groups = max_pairs // LANES + SUBLANES
    smem_tile = lambda n: pl.BlockSpec((tm * n,), lambda b, i: (i,), memory_space=pltpu.SMEM)
    out = jax.ShapeDtypeStruct((nb, n_tiles, groups, LANES), F32)
    out_spec = pl.BlockSpec((None, None, groups, LANES), lambda b, i: (b, i, 0, 0))
    return pl.pallas_call(
        functools.partial(_pdot_kernel, n_blocks=nb),
        grid=(nb, n_tiles),
        in_specs=[
            pl.BlockSpec(memory_space=pltpu.SMEM),
            pl.BlockSpec((_chunk_list_stride(slots // CHUNK),), lambda b, i: (i,), memory_space=pltpu.SMEM),
            smem_tile(slots),
            pl.BlockSpec((tm, ROW_SUBLANES, LANES), lambda b, i: (i, 0, 0)),
            _expert_block_spec(),
        ],
        out_specs=out_spec,
        out_shape=out,
        scratch_shapes=[pltpu.VMEM((LANES + max_pairs, LANES), F32)],
        compiler_params=_cparams(("arbitrary", "arbitrary")),
        name="pdot",
    )(n_chunks, clist, idx, x3, u_rows)


def _pacc_kernel(nch_ref, cl_ref, idx_ref, d_ref, g_ref, v_ref, o_ref, wrep_ref, wrow_ref, *, n_blocks):
    b = pl.program_id(0)
    i = pl.program_id(1)
    n_tiles = pl.num_programs(1)
    n_chunks = nch_ref[b * n_tiles + i]
    first = jnp.int32(0)
    for bb in range(n_blocks - 1):
        first = first + jnp.where(bb < b, nch_ref[bb * n_tiles + i], 0)

    per_group = LANES // CHUNK
    n_full = n_chunks // per_group

    def expand(g):
        tile = (SUBLANES, LANES)
        wrow_ref[...] = (jnp.broadcast_to(g_ref[pl.ds(g, 1), :], tile)
                         * jax.nn.gelu(jnp.broadcast_to(d_ref[pl.ds(g, 1), :], tile)))
        rows = jnp.broadcast_to(wrow_ref[0:1, :], (LANES, LANES))
        wrep_ref[pl.ds(pl.multiple_of(g * LANES, LANES), LANES), :] = rows.T

    expand(0)
    o_ref[...] = jnp.zeros(o_ref.shape, F32)

    def chunk(q):
        entry = cl_ref[first + q]
        base = entry & 0xFFFF
        tok = entry >> 16
        zero = jnp.zeros((SUBLANES, LANES), F32)
        accs = [zero, zero, zero, zero]
        for k in range(CHUNK):
            w = jnp.broadcast_to(wrep_ref[pl.ds(q * CHUNK + k, 1), :], (SUBLANES, LANES))
            lo, hi = _unpack_rows(v_ref[pl.ds(pl.multiple_of(idx_ref[base + k], SUBLANES), SUBLANES), :])
            j = 2 * (k % 2)
            accs[j] = accs[j] + w * lo
            accs[j + 1] = accs[j + 1] + w * hi
        o_ref[tok] = o_ref[tok] + jnp.concatenate([accs[0] + accs[2], accs[1] + accs[3]], axis=0)

    def group_body(g, carry):
        expand(g + 1)
        for c in range(per_group):
            chunk(g * per_group + c)
        return carry

    lax.fori_loop(0, n_full, group_body, 0)

    def tail_body(q, carry):
        chunk(q)
        return carry

    lax.fori_loop(n_full * per_group, n_chunks, tail_body, 0)


def _peer_accumulate(n_chunks, clist, idx, dots, gates_c, v_rows, T, slots):
    nb = v_rows.shape[0] // (EXPERT_BLOCK * SUBLANES)
    tm = PEER_TM
    n_tiles = T // tm
    groups = dots.shape[2]
    assert groups == tm * slots // LANES + SUBLANES and gates_c.shape == dots.shape
    smem_tile = lambda n: pl.BlockSpec((tm * n,), lambda b, i: (i,), memory_space=pltpu.SMEM)
    per_step = lambda: pl.BlockSpec((None, None, groups, LANES), lambda b, i: (b, i, 0, 0))
    return pl.pallas_call(
        functools.partial(_pacc_kernel, n_blocks=nb),
        grid=(nb, n_tiles),
        in_specs=[
            pl.BlockSpec(memory_space=pltpu.SMEM),
            pl.BlockSpec((_chunk_list_stride(slots // CHUNK),), lambda b, i: (i,), memory_space=pltpu.SMEM),
            smem_tile(slots),
            per_step(), per_step(),
            _expert_block_spec(),
        ],
        out_specs=pl.BlockSpec((None, tm, ROW_SUBLANES, LANES), lambda b, i: (b, i, 0, 0)),
        out_shape=jax.ShapeDtypeStruct((nb, T, ROW_SUBLANES, LANES), F32),
        scratch_shapes=[pltpu.VMEM((tm * slots + LANES, LANES), F32), pltpu.VMEM((SUBLANES, LANES), F32)],
        compiler_params=_cparams(("arbitrary", "arbitrary")),
        name="pacc",
    )(n_chunks, clist, idx, dots, gates_c, v_rows)


def _final_kernel(x1_ref, p_ref, g_ref, o_ref):
    nb, _, n_s, _ = p_ref.shape
    cols = []
    for s in range(n_s):
        col = x1_ref[:, s * LANES:(s + 1) * LANES]
        for b in range(nb):
            col = col + p_ref[b, :, s, :]
        cols.append(col)
    o_ref[...] = _rms(jnp.concatenate(cols, axis=1), g_ref[...])


def _final(x1, peer_parts, g_final):
    T, D = x1.shape
    nb = peer_parts.shape[0]
    tm = 256
    return pl.pallas_call(
        _final_kernel,
        grid=(T // tm,),
        in_specs=[
            pl.BlockSpec((tm, D), lambda i: (i, 0)),
            pl.BlockSpec((nb, tm, D // LANES, LANES), lambda i: (0, i, 0, 0)),
            pl.BlockSpec((1, D), lambda i: (0, 0)),
        ],
        out_specs=pl.BlockSpec((tm, D), lambda i: (i, 0)),
        out_shape=jax.ShapeDtypeStruct((T, D), F32),
        compiler_params=_cparams(("parallel",)),
        name="final",
    )(x1, peer_parts, g_final)


def _rope_tables(seq):
    half = HEAD_DIM // 2
    inv = ROPE_THETA ** (-jnp.arange(half, dtype=F32) / half)
    ang = jnp.arange(seq).astype(F32)[:, None] * inv[None, :]
    cos, sin = jnp.cos(ang), jnp.sin(ang)
    reps = LANES // HEAD_DIM
    cos_t = jnp.concatenate([cos, cos] * reps, axis=1)
    sin_t = jnp.concatenate([-sin, sin] * reps, axis=1)
    return cos_t, sin_t


def _layer(xt, batch, seq, g_mix, w_in, sink, conv_w, conv_b, fwd_wa, fwd_ba, fwd_wx, fwd_bx, fwd_lam,
           bwd_wa, bwd_ba, bwd_wx, bwd_bx, bwd_lam, g_attn_out, g_lru_out, w_out, g_ffn,
           w_pq, sub_k1, sub_k2, u_emb, v_emb):
    T, D = xt.shape
    cos_t, sin_t = _rope_tables(seq)
    qkv, gr = _inproj(xt, g_mix.reshape(1, D), w_in.astype(BF16), cos_t, sin_t, seq)
    y_attn = _attention(qkv, sink, batch, seq)
    y_lru = _lru(gr, conv_w, conv_b, (fwd_wa, fwd_wx, bwd_wa, bwd_wx), (fwd_ba, fwd_bx, bwd_ba, bwd_bx),
                 (fwd_lam, bwd_lam), batch, seq)
    x1, xn, x3 = _outproj(y_attn, y_lru, xt, g_attn_out.reshape(1, -1), g_lru_out.reshape(1, -1),
                      w_out.astype(BF16), g_ffn.reshape(1, D))

    keys = jnp.concatenate([sub_k1, sub_k2], axis=0).astype(BF16)
    scores = _peer_scores(xn, w_pq.astype(BF16), keys)
    gates_t, eidx_t = _peer_topk(scores)
    n_pairs = PEER_HEADS * PEER_TOPK
    gates = gates_t.transpose(2, 0, 1).reshape(T, n_pairs)
    eidx = eidx_t.transpose(2, 0, 1).reshape(T, n_pairs)
    n_exp = u_emb.shape[0]
    nb = n_exp // EXPERT_BLOCK
    slots = n_pairs + nb * CHUNK
    idx, gates_c, n_chunks, clist = _route(eidx, gates, nb)

    u_rows = _pack_rows(u_emb)
    v_rows = _pack_rows(v_emb)
    dots = _peer_dots(n_chunks, clist, idx, x3, u_rows, slots)
    gates_c = jnp.pad(gates_c, ((0, 0), (0, 0), (0, dots.shape[2] - gates_c.shape[2]), (0, 0)))
    parts = _peer_accumulate(n_chunks, clist, idx, dots, gates_c, v_rows, T, slots)
    return x1, parts


def kernel(x, g_mix, w_in, sink, conv_w, conv_b, fwd_wa, fwd_ba, fwd_wx, fwd_bx, fwd_lam, bwd_wa, bwd_ba, bwd_wx, bwd_bx, bwd_lam, g_attn_out, g_lru_out, w_out, g_ffn, w_pq, sub_k1, sub_k2, u_emb, v_emb, g_final):
    B, S, D = x.shape
    assert g_mix.shape[0] == 1, "single-layer trunk"
    xt = x.reshape(B * S, D)
    x1, parts = _layer(xt, B, S, g_mix[0], w_in[0], sink[0], conv_w[0], conv_b[0],
                       fwd_wa[0], fwd_ba[0], fwd_wx[0], fwd_bx[0], fwd_lam[0],
                       bwd_wa[0], bwd_ba[0], bwd_wx[0], bwd_bx[0], bwd_lam[0],
                       g_attn_out[0], g_lru_out[0], w_out[0], g_ffn[0],
                       w_pq[0], sub_k1[0], sub_k2[0], u_emb[0], v_emb[0])
    return _final(x1, parts, g_final.reshape(1, D)).reshape(B, S, D)
```
